```python
import jax, jax.numpy as jnp
from jax import lax
import numpy as np

D_MODEL = 1024
BATCH = 16
SEQ = 256
DEPTH = 4
DEC_BATCH = 8
DEC_SEQ = 1024
PAST_LEN = 256

GRID_W = 64
N_HEADS = 4
DM = 2 * D_MODEL
DV = DM // N_HEADS
DQK = DV // 2
N_FGROUPS = 4
DF = D_MODEL
DG = DF // N_FGROUPS
D_FF = ((8 * D_MODEL // 3 + 255) // 256) * 256
CHUNK = 64
N_GATES = 4 * N_HEADS
ALPHA = (2 * DEPTH) ** 0.25
BETA = (8 * DEPTH) ** -0.25
LN_EPS = 1e-5

Q_END = N_HEADS * DQK
K_END = Q_END + N_HEADS * DQK
V_END = K_END + DM
O_END = V_END + DM
G_END = O_END + N_GATES
F_END = G_END + DF
GA_END = F_END + D_MODEL
N_IN = GA_END + D_MODEL
SPLITS = [Q_END, K_END, V_END, O_END, G_END, F_END, GA_END]

kernel_name = 'hybrid_mlstm_fnet_diffusion_step'


def layer_norm(x, g, b):
    xf = x.astype(jnp.float32)
    mu = xf.mean(-1, keepdims=True)
    var = jnp.square(xf - mu).mean(-1, keepdims=True)
    return ((xf - mu) * lax.rsqrt(var + LN_EPS) * g + b).astype(x.dtype)


def mlstm_chunkwise(q, k, v, ig, lf, C0, n0, m0):
    B, H, S, _ = q.shape
    nc = S // CHUNK

    def to_chunks(t):
        return jnp.moveaxis(t.reshape(B, H, nc, CHUNK, *t.shape[3:]), 2, 0)

    causal = jnp.tril(jnp.ones((CHUNK, CHUNK), bool))

    def step(carry, xs):
        C, n, m = carry
        qc, kc, vc, ic, fc = xs
        b = jnp.cumsum(fc, axis=-1)
        a = b + m[..., None]
        d = jnp.where(causal, b[..., :, None] - b[..., None, :] + ic[..., None, :], -jnp.inf)
        mt = jnp.maximum(a, d.max(-1))
        w = jnp.exp(d - mt[..., None])
        inter = jnp.exp(a - mt)
        s = jnp.einsum('bhtd,bhsd->bhts', qc, kc) * w
        num = jnp.einsum('bhts,bhsv->bhtv', s, vc) + inter[..., None] * jnp.einsum('bhvd,bhtd->bhtv', C, qc)
        den = s.sum(-1) + inter * jnp.einsum('bhd,bhtd->bht', n, qc)
        h = num / jnp.maximum(jnp.abs(den), jnp.exp(-mt))[..., None]
        bl = b[..., -1]
        g = bl[..., None] - b + ic
        ml = jnp.maximum(bl + m, g.max(-1))
        decay = jnp.exp(bl + m - ml)
        wg = jnp.exp(g - ml[..., None])
        C_new = decay[..., None, None] * C + jnp.einsum('bhsv,bhsd->bhvd', vc * wg[..., None], kc)
        n_new = decay[..., None] * n + jnp.einsum('bhs,bhsd->bhd', wg, kc)
        return (C_new, n_new, ml), h

    (C, n, m), hs = lax.scan(step, (C0, n0, m0),
                             (to_chunks(q), to_chunks(k), to_chunks(v), to_chunks(ig), to_chunks(lf)))
    h = jnp.moveaxis(hs, 0, 2).reshape(B, H, S, DV)
    return h, (C, n, m)


def mlstm_bidir(q, k, v, gates, init):
    hs, finals = [], []
    for d in range(2):
        ig = gates[..., d, 0]
        lf = jax.nn.log_sigmoid(gates[..., d, 1])
        if d == 0:
            h, fin = mlstm_chunkwise(q, k, v, ig, lf, *init[d])
        else:
            h, fin = mlstm_chunkwise(jnp.flip(q, 2), jnp.flip(k, 2), jnp.flip(v, 2),
                                     jnp.flip(ig, 2), jnp.flip(lf, 2), *init[d])
            h = jnp.flip(h, 2)
        hs.append(h)
        finals.append(fin)
    return hs[0] + hs[1], finals


def fourier_seq(u):
    B, S, _ = u.shape
    ug = u.astype(jnp.float32).reshape(B, S, N_FGROUPS, DG)
    return jnp.fft.fftn(ug, axes=(1, 3), norm='ortho').real.reshape(B, S, DF).astype(u.dtype)


def fourier_grid(u):
    B, S, _ = u.shape
    rows = S // GRID_W
    ug = u.astype(jnp.float32).reshape(B, rows, GRID_W, N_FGROUPS, DG)
    return jnp.fft.fftn(ug, axes=(1, 2, 4), norm='ortho').real.reshape(B, S, DF).astype(u.dtype)


def mixer(h, l, p, init, fourier_fn):
    B, S, _ = h.shape
    proj = h @ p['w_in'][l]
    q, k, v, o, gt, uf, ga, gb = jnp.split(proj, SPLITS, axis=-1)

    def heads(t, dim):
        return t.reshape(B, S, N_HEADS, dim).transpose(0, 2, 1, 3).astype(jnp.float32)

    qh = heads(q, DQK)
    kh = heads(k, DQK) * DQK ** -0.5
    vh = heads(v, DV)
    gates = (gt + p['b_gate'][l]).astype(jnp.float32).reshape(B, S, 2, 2, N_HEADS).transpose(0, 4, 1, 2, 3)
    h_m, finals = mlstm_bidir(qh, kh, vh, gates, init)
    hf = h_m.transpose(0, 2, 1, 3)
    mu = hf.mean(-1, keepdims=True)
    var = jnp.square(hf - mu).mean(-1, keepdims=True)
    hn = (hf - mu) * lax.rsqrt(var + LN_EPS) * p['mh_gain'][l].reshape(N_HEADS, DV)
    og = jax.nn.sigmoid(o.reshape(B, S, N_HEADS, DV).astype(jnp.float32))
    y_a = (hn * og).reshape(B, S, DM).astype(h.dtype) @ p['w_branch_a'][l]
    y_b = fourier_fn(uf) @ p['w_branch_b'][l]
    merged = jax.nn.sigmoid(ga) * y_a + jax.nn.sigmoid(gb) * y_b
    return merged @ p['w_out'][l], finals


def trunk_layer(x, mod, l, p, init, fourier_fn):
    sh1, sc1, g1, sh2, sc2, g2 = jnp.split(mod, 6, axis=-1)
    h = x * (1 + sc1) + sh1
    out, finals = mixer(h, l, p, init, fourier_fn)
    x = layer_norm(ALPHA * x + g1 * out, p['ln_gain'][l, 0], p['ln_bias'][l, 0])
    h2 = x * (1 + sc2) + sh2
    a, u = jnp.split(h2 @ p['w_ffn_in'][l], 2, axis=-1)
    f = (jax.nn.silu(a) * u) @ p['w_ffn_out'][l]
    x = layer_norm(ALPHA * x + g2 * f, p['ln_gain'][l, 1], p['ln_bias'][l, 1])
    return x, finals


def setup_inputs(seed: int = 0) -> dict:
    key = jax.random.key(seed)
    ks = jax.random.split(key, 20)

    def nrm(k, shape, scale):
        return jax.random.normal(k, shape, jnp.float32) * scale

    x_prompt = nrm(ks[0], (BATCH, SEQ, D_MODEL), 1.0)
    x_sample = nrm(ks[1], (DEC_BATCH, DEC_SEQ, D_MODEL), 1.0)
    c = nrm(ks[2], (DEC_BATCH, D_MODEL), 1.0)
    state_C = nrm(ks[3], (DEC_BATCH, DEPTH, 2, N_HEADS, DV, DQK), 0.1)
    state_n = nrm(ks[4], (DEC_BATCH, DEPTH, 2, N_HEADS, DQK), 0.1)
    state_m = jax.random.uniform(ks[5], (DEC_BATCH, DEPTH, 2, N_HEADS), jnp.float32, 0.0, 2.0)
    c_ctx = nrm(ks[6], (D_MODEL,), 1.0)
    w_mod = nrm(ks[7], (DEPTH, D_MODEL, 6 * D_MODEL), D_MODEL ** -0.5)
    b_mod = nrm(ks[8], (DEPTH, 6 * D_MODEL), 0.02)
    w_in = nrm(ks[9], (DEPTH, D_MODEL, N_IN), D_MODEL ** -0.5)
    gate_base = jnp.array([0.0, 3.0], jnp.float32).reshape(1, 1, 2, 1)
    b_gate = (gate_base + nrm(ks[10], (DEPTH, 2, 2, N_HEADS), 0.1)).reshape(DEPTH, N_GATES)
    mh_gain = 1.0 + nrm(ks[11], (DEPTH, DM), 0.02)
    w_branch_a = nrm(ks[12], (DEPTH, DM, D_MODEL), DM ** -0.5)
    w_branch_b = nrm(ks[13], (DEPTH, DF, D_MODEL), DF ** -0.5)
    w_out = nrm(ks[14], (DEPTH, D_MODEL, D_MODEL), BETA * D_MODEL ** -0.5)
    ln_gain = 1.0 + nrm(ks[15], (DEPTH, 2, D_MODEL), 0.02)
    ln_bias = nrm(ks[16], (DEPTH, 2, D_MODEL), 0.02)
    w_ffn_in = nrm(ks[17], (DEPTH, D_MODEL, 2 * D_FF), D_MODEL ** -0.5)
    w_ffn_out = nrm(ks[18], (DEPTH, D_FF, D_MODEL), BETA * D_FF ** -0.5)
    return {'x_prompt': x_prompt, 'x_sample': x_sample, 'c': c,
            'state_C': state_C, 'state_n': state_n, 'state_m': state_m,
            'c_ctx': c_ctx, 'w_mod': w_mod, 'b_mod': b_mod, 'w_in': w_in, 'b_gate': b_gate,
            'mh_gain': mh_gain, 'w_branch_a': w_branch_a, 'w_branch_b': w_branch_b, 'w_out': w_out,
            'ln_gain': ln_gain, 'ln_bias': ln_bias, 'w_ffn_in': w_ffn_in, 'w_ffn_out': w_ffn_out}


def reference(x_prompt, x_sample, c, state_C, state_n, state_m, c_ctx, w_mod, b_mod, w_in, b_gate,
              mh_gain, w_branch_a, w_branch_b, w_out, ln_gain, ln_bias, w_ffn_in, w_ffn_out):
    p = {'w_in': w_in, 'b_gate': b_gate, 'mh_gain': mh_gain, 'w_branch_a': w_branch_a,
         'w_branch_b': w_branch_b, 'w_out': w_out, 'ln_gain': ln_gain, 'ln_bias': ln_bias,
         'w_ffn_in': w_ffn_in, 'w_ffn_out': w_ffn_out}
    f32 = jnp.float32
    bp = x_prompt.shape[0]
    zero_state = (jnp.zeros((bp, N_HEADS, DV, DQK), f32), jnp.zeros((bp, N_HEADS, DQK), f32),
                  jnp.zeros((bp, N_HEADS), f32))
    xp = x_prompt
    xs = x_sample
    new_C, new_n, new_m = [], [], []
    for l in range(DEPTH):
        mod_ctx = (jax.nn.silu(c_ctx) @ w_mod[l] + b_mod[l])[None, None, :]
        xp, fin = trunk_layer(xp, mod_ctx, l, p, (zero_state, zero_state), fourier_seq)
        new_C.append(jnp.stack([fin[0][0], fin[1][0]], axis=1))
        new_n.append(jnp.stack([fin[0][1], fin[1][1]], axis=1))
        new_m.append(jnp.stack([fin[0][2], fin[1][2]], axis=1))
        mod_lat = (jax.nn.silu(c) @ w_mod[l] + b_mod[l])[:, None, :]
        init_lat = tuple((state_C[:, l, d].astype(f32), state_n[:, l, d].astype(f32),
                          state_m[:, l, d].astype(f32)) for d in range(2))
        xs, _ = trunk_layer(xs, mod_lat, l, p, init_lat, fourier_grid)
    new_state_C = jnp.stack(new_C, axis=1)
    new_state_n = jnp.stack(new_n, axis=1)
    new_state_m = jnp.stack(new_m, axis=1)
    return (xp, xs, new_state_C, new_state_n, new_state_m)
```

```python
import functools

import numpy as np
import jax
import jax.numpy as jnp
from jax import lax
from jax.experimental import pallas as pl
from jax.experimental.pallas import tpu as pltpu

F32 = jnp.float32
BF16 = jnp.bfloat16

N_HEADS = 4
N_FGROUPS = 4
GRID_W = 64
LN_EPS = 1e-5
MLSTM_CHUNK = 256
GATE_LANES = 128
GATE_STRIDE = 8
VMEM_LIMIT = 56 * 1024 * 1024

NT_DIMS = (((1,), (1,)), ((), ()))
TN_DIMS = (((0,), (0,)), ((), ()))


def _dot(a, b):
    return jnp.dot(a, b, preferred_element_type=F32)


def _dot_nt(a, b):
    return lax.dot_general(a, b, NT_DIMS, preferred_element_type=F32)


def _dot_tn(a, b):
    return lax.dot_general(a, b, TN_DIMS, preferred_element_type=F32)


def _split3(x):
    hi = x.astype(BF16)
    r = x - hi.astype(F32)
    mid = r.astype(BF16)
    lo = (r - mid.astype(F32)).astype(BF16)
    return hi, mid, lo


def _layer_norm(y, g, b):
    mu = jnp.mean(y, axis=-1, keepdims=True)
    yc = y - mu
    var = jnp.mean(yc * yc, axis=-1, keepdims=True)
    return yc * lax.rsqrt(var + LN_EPS) * g + b


def _log_sigmoid(x):
    return jnp.minimum(x, 0.0) - jnp.log(1.0 + jnp.exp(-jnp.abs(x)))


def _params(*sem):
    return pltpu.CompilerParams(dimension_semantics=sem, vmem_limit_bytes=VMEM_LIMIT)


def _mod_kernel(c_ref, w_ref, b_ref, o_ref):
    c = c_ref[...]
    s = (c * jax.nn.sigmoid(c)).astype(BF16)
    o_ref[...] = _dot(s, w_ref[...].astype(BF16)) + b_ref[...]


def _modulation(cvec, w_mod, b_mod):
    depth, d, n6 = w_mod.shape
    rows = cvec.shape[0]
    tn = 1536
    return pl.pallas_call(
        _mod_kernel,
        grid=(depth, n6 // tn),
        in_specs=[
            pl.BlockSpec((rows, d), lambda l, j: (0, 0)),
            pl.BlockSpec((None, d, tn), lambda l, j: (l, 0, j)),
            pl.BlockSpec((None, 1, tn), lambda l, j: (l, 0, j)),
        ],
        out_specs=pl.BlockSpec((None, rows, tn), lambda l, j: (l, 0, j)),
        out_shape=jax.ShapeDtypeStruct((depth, rows, n6), F32),
        compiler_params=_params("parallel", "parallel"),
        name="modulation",
    )(cvec, w_mod, b_mod.reshape(depth, 1, n6))


def _inproj_kernel(x_ref, mod_ref, w_ref, cs_ref, wg_ref, bg_ref, proj_ref, gcol_ref, grow_ref, h_ref, *, d):
    j = pl.program_id(1)

    @pl.when(j == 0)
    def _():
        m = mod_ref[...]
        sh = m[:, 0:d]
        sc = m[:, d:2 * d]
        hb = (x_ref[...] * (1.0 + sc) + sh).astype(BF16)
        h_ref[...] = hb
        g = _dot(hb, wg_ref[...]) + bg_ref[...]
        lane = lax.broadcasted_iota(jnp.int32, g.shape, 1)
        g = jnp.where((lane & 1) == 1, _log_sigmoid(g), g)
        gcol_ref[...] = g
        gt = g.T
        for hh in range(N_HEADS):
            grow_ref[hh] = gt[hh * GATE_STRIDE:(hh + 1) * GATE_STRIDE, :]

    proj_ref[...] = (_dot(h_ref[...], w_ref[...]) * cs_ref[...]).astype(BF16)


def _inproj(x, mod_l, mod_row, w_main, colscale, w_gate, b_gate, tm, tn):
    rows, d = x.shape
    n = w_main.shape[1]
    return pl.pallas_call(
        functools.partial(_inproj_kernel, d=d),
        grid=(rows // tm, n // tn),
        in_specs=[
            pl.BlockSpec((tm, d), lambda i, j: (i, 0)),
            pl.BlockSpec((None, 1, 6 * d), lambda i, j: (mod_row(i, tm), 0, 0)),
            pl.BlockSpec((d, tn), lambda i, j: (0, j)),
            pl.BlockSpec((1, tn), lambda i, j: (0, j)),
            pl.BlockSpec((d, GATE_LANES), lambda i, j: (0, 0)),
            pl.BlockSpec((1, GATE_LANES), lambda i, j: (0, 0)),
        ],
        out_specs=[
            pl.BlockSpec((tm, tn), lambda i, j: (i, j)),
            pl.BlockSpec((tm, GATE_LANES), lambda i, j: (i, 0)),
            pl.BlockSpec((N_HEADS, GATE_STRIDE, tm), lambda i, j: (0, 0, i)),
        ],
        out_shape=[
            jax.ShapeDtypeStruct((rows, n), BF16),
            jax.ShapeDtypeStruct((rows, GATE_LANES), F32),
            jax.ShapeDtypeStruct((N_HEADS, GATE_STRIDE, rows), F32),
        ],
        scratch_shapes=[pltpu.VMEM((tm, d), BF16)],
        compiler_params=_params("parallel", "arbitrary"),
        name="inproj",
    )(x, mod_l, w_main, colscale, w_gate, b_gate)


def _mlstm_kernel(*refs, seq, has_init, layer):
    if has_init:
        (q_ref, k_ref, v_ref, o_ref, gcol_ref, grow_ref, gain_ref, c0_ref, n0_ref, m0_ref,
         out_ref, cst_ref) = refs
    else:
        (q_ref, k_ref, v_ref, o_ref, gcol_ref, grow_ref, gain_ref,
         out_ref, cfin_ref, nfin_ref, mfin_ref) = refs
    L = MLSTM_CHUNK
    nc = seq // L
    b_idx = pl.program_id(0)
    h_idx = pl.program_id(1)
    dqk = q_ref.shape[1]

    rows_i = lax.broadcasted_iota(jnp.int32, (L, L), 0)
    cols_i = lax.broadcasted_iota(jnp.int32, (L, L), 1)
    lower = rows_i >= cols_i
    upper = rows_i <= cols_i
    lower_b = lower.astype(BF16)
    upper_b = upper.astype(BF16)
    lane = lax.broadcasted_iota(jnp.int32, (L, GATE_LANES), 1)
    lane0 = h_idx * GATE_STRIDE

    def pick(x, jj):
        return jnp.sum(jnp.where(lane == lane0 + jj, x, 0.0), axis=1, keepdims=True)

    def tri_left(tri, x):
        hi, mid, lo = _split3(x)
        return _dot(tri, hi) + _dot(tri, mid) + _dot(tri, lo)

    def tri_right(x, tri):
        hi, mid, lo = _split3(x)
        return _dot(hi, tri) + _dot(mid, tri) + _dot(lo, tri)

    def gate_terms(c):
        gc = gcol_ref[pl.ds(c * L, L), :]
        gr = grow_ref[:, pl.ds(c * L, L)]
        pre_c = tri_left(lower_b, gc)
        suf_c = tri_left(upper_b, gc)
        pre_r = tri_right(gr, upper_b)
        suf_r = tri_right(gr, lower_b)
        fwd = (pick(pre_c, 1), pre_r[1:2, :], pick(gc, 0), gr[0:1, :], pre_r[1:2, L - 1:L])
        bwd = (pick(suf_c, 3), suf_r[3:4, :], pick(gc, 2), gr[2:3, :], suf_r[3:4, 0:1])
        return fwd, bwd

    def state_update(c, terms, m_prev, n_prev, c_prev):
        b_col, _, i_col, _, b_tot = terms
        g_col = b_tot - b_col + i_col
        carried = b_tot + m_prev
        m_new = jnp.maximum(carried, jnp.max(g_col, axis=0, keepdims=True))
        wg = jnp.exp(g_col - m_new)
        kw = k_ref[pl.ds(c * L, L), :].astype(F32) * wg
        c_new = _dot_tn(v_ref[pl.ds(c * L, L), :], kw.astype(BF16))
        n_new = jnp.sum(kw, axis=0, keepdims=True)
        if c_prev is not None:
            decay = jnp.exp(carried - m_new)
            c_new = decay * c_prev + c_new
            n_new = decay * n_prev + n_new
        return m_new, n_new, c_new

    terms = [gate_terms(c) for c in range(nc)]

    m_pre = [[None] * nc for _ in range(2)]
    n_pre = [[None] * nc for _ in range(2)]
    for d in range(2):
        order = list(range(nc)) if d == 0 else list(range(nc - 1, -1, -1))
        if has_init:
            m = jnp.full((1, 1), m0_ref[b_idx, layer, d, h_idx], F32)
            n = n0_ref[d, pl.ds(h_idx, 1), :]
            cst_ref[d, order[0]] = c0_ref[d]
        else:
            m = jnp.zeros((1, 1), F32)
            n = None
        for pos, c in enumerate(order):
            m_pre[d][c] = m
            n_pre[d][c] = n
            last = pos == nc - 1
            if last and has_init:
                break
            c_prev = cst_ref[d, c] if has_init else None
            m, n, c_new = state_update(c, terms[c][d], m, n, c_prev)
            if not last:
                cst_ref[d, order[pos + 1]] = c_new
            else:
                cfin_ref[d] = c_new
                nfin_ref[pl.ds(d, 1), :] = n
                mfin_ref[pl.ds(d, 1), :] = jnp.broadcast_to(m, (1, GATE_LANES))

    gain = gain_ref[...]
    for c in range(nc):
        q = q_ref[pl.ds(c * L, L), :]
        k = k_ref[pl.ds(c * L, L), :]
        v = v_ref[pl.ds(c * L, L), :]
        qk = _dot_nt(q, k)
        qf = q.astype(F32)
        p_sum = None
        h_inter = None
        for d in range(2):
            b_col, b_row, _, i_row, _ = terms[c][d]
            mask = lower if d == 0 else upper
            m_prev = m_pre[d][c]
            a = b_col + m_prev
            dm = jnp.where(mask, b_col - b_row + i_row, -jnp.inf)
            mt = jnp.maximum(a, jnp.max(dm, axis=1, keepdims=True))
            s = qk * jnp.exp(dm - mt)
            den = jnp.sum(s, axis=1, keepdims=True)
            if has_init:
                inter = jnp.exp(a - mt)
                den = den + inter * jnp.sum(qf * n_pre[d][c], axis=1, keepdims=True)
            r = 1.0 / jnp.maximum(jnp.abs(den), jnp.exp(-mt))
            p = s * r
            p_sum = p if p_sum is None else p_sum + p
            if has_init:
                hi = (inter * r) * _dot_nt(q, cst_ref[d, c].astype(BF16))
                h_inter = hi if h_inter is None else h_inter + hi
        h = _dot(p_sum.astype(BF16), v)
        if has_init:
            h = h + h_inter
        mu = jnp.mean(h, axis=-1, keepdims=True)
        hc = h - mu
        var = jnp.mean(hc * hc, axis=-1, keepdims=True)
        hn = hc * lax.rsqrt(var + LN_EPS) * gain
        og = jax.nn.sigmoid(o_ref[pl.ds(c * L, L), :].astype(F32))
        out_ref[pl.ds(c * L, L), :] = (hn * og).astype(BF16)


def _mlstm(proj, gcol, grow, gain_l, seq, layer, init=None):
    rows = proj.shape[0]
    n_seq = rows // seq
    dm = gain_l.shape[0] * gain_l.shape[2]
    dv = dm // N_HEADS
    dqk = dv // 2
    qb = 0
    kb = (N_HEADS * dqk) // dqk
    vb = (2 * N_HEADS * dqk) // dv
    ob = (2 * N_HEADS * dqk + dm) // dv
    has_init = init is not None
    assert seq % MLSTM_CHUNK == 0 and (has_init or seq == MLSTM_CHUNK)
    in_specs = [
        pl.BlockSpec((seq, dqk), lambda b, h: (b, qb + h)),
        pl.BlockSpec((seq, dqk), lambda b, h: (b, kb + h)),
        pl.BlockSpec((seq, dv), lambda b, h: (b, vb + h)),
        pl.BlockSpec((seq, dv), lambda b, h: (b, ob + h)),
        pl.BlockSpec((seq, GATE_LANES), lambda b, h: (b, 0)),
        pl.BlockSpec((None, GATE_STRIDE, seq), lambda b, h: (h, 0, b)),
        pl.BlockSpec((None, 1, dv), lambda b, h: (h, 0, 0)),
    ]
    args = [proj, proj, proj, proj, gcol, grow, gain_l]
    out_specs = [pl.BlockSpec((seq, dv), lambda b, h: (b, h))]
    out_shape = [jax.ShapeDtypeStruct((rows, dm), BF16)]
    scratch = []
    if has_init:
        state_c, state_n, state_m = init
        in_specs += [
            pl.BlockSpec((None, None, 2, None, dv, dqk), lambda b, h: (b, layer, 0, h, 0, 0)),
            pl.BlockSpec((None, None, 2, N_HEADS, dqk), lambda b, h: (b, layer, 0, 0, 0)),
            pl.BlockSpec(memory_space=pltpu.MemorySpace.SMEM),
        ]
        args += [state_c, state_n, state_m]
        scratch = [pltpu.VMEM((2, seq // MLSTM_CHUNK, dv, dqk), F32)]
    else:
        out_specs += [
            pl.BlockSpec((None, 2, None, dv, dqk), lambda b, h: (b, 0, h, 0, 0)),
            pl.BlockSpec((None, None, 2, dqk), lambda b, h: (b, h, 0, 0)),
            pl.BlockSpec((None, None, 2, GATE_LANES), lambda b, h: (b, h, 0, 0)),
        ]
        out_shape += [
            jax.ShapeDtypeStruct((n_seq, 2, N_HEADS, dv, dqk), F32),
            jax.ShapeDtypeStruct((n_seq, N_HEADS, 2, dqk), F32),
            jax.ShapeDtypeStruct((n_seq, N_HEADS, 2, GATE_LANES), F32),
        ]
    return pl.pallas_call(
        functools.partial(_mlstm_kernel, seq=seq, has_init=has_init, layer=layer),
        grid=(n_seq, N_HEADS),
        in_specs=in_specs,
        out_specs=out_specs,
        out_shape=out_shape,
        scratch_shapes=scratch,
        compiler_params=_params("parallel", "parallel"),
        name="mlstm_lat" if has_init else "mlstm_ctx",
    )(*args)


def _fourier_kernel(u_ref, cc_ref, sc_ref, ms_ref, o_ref, ab_ref, *, seq):
    u = u_ref[...]
    ab_ref[0:seq, :] = _dot(u, cc_ref[...]).astype(BF16)
    ab_ref[seq:2 * seq, :] = _dot(u, sc_ref[...]).astype(BF16)
    o_ref[...] = _dot(ms_ref[...], ab_ref[...]).astype(BF16)


def _dft_tables(seq, dg, grid_w):
    ch = np.arange(dg)
    ang_c = 2.0 * np.pi * ((np.outer(ch, ch) % dg) / dg)
    sc_c = 1.0 / np.sqrt(dg)
    cc = np.cos(ang_c) * sc_c
    sc = np.sin(ang_c) * sc_c
    t = np.arange(seq)
    if grid_w is None:
        frac = (np.outer(t, t) % seq) / seq
    else:
        rows = seq // grid_w
        r, w = t // grid_w, t % grid_w
        frac = (np.outer(r, r) % rows) / rows + (np.outer(w, w) % grid_w) / grid_w
    ang_s = 2.0 * np.pi * frac
    sc_s = 1.0 / np.sqrt(seq)
    ms = np.concatenate([np.cos(ang_s) * sc_s, -np.sin(ang_s) * sc_s], axis=1)
    return (jnp.asarray(cc, F32).astype(BF16), jnp.asarray(sc, F32).astype(BF16),
            jnp.asarray(ms, F32).astype(BF16))


def _fourier(proj, tables, seq, col_block0, df):
    rows = proj.shape[0]
    cc, sc, ms = tables
    dg = cc.shape[0]
    return pl.pallas_call(
        functools.partial(_fourier_kernel, seq=seq),
        grid=(rows // seq, df // dg),
        in_specs=[
            pl.BlockSpec((seq, dg), lambda b, g: (b, col_block0 + g)),
            pl.BlockSpec((dg, dg), lambda b, g: (0, 0)),
            pl.BlockSpec((dg, dg), lambda b, g: (0, 0)),
            pl.BlockSpec((seq, 2 * seq), lambda b, g: (0, 0)),
        ],
        out_specs=pl.BlockSpec((seq, dg), lambda b, g: (b, g)),
        out_shape=jax.ShapeDtypeStruct((rows, df), BF16),
        scratch_shapes=[pltpu.VMEM((2 * seq, dg), BF16)],
        compiler_params=_params("parallel", "parallel"),
        name="fourier",
    )(proj, cc, sc, ms)


def _merge_kernel(hg_ref, fy_ref, ga_ref, gb_ref, x_ref, mod_ref, wa_ref, wb_ref, wo_ref, lg_ref, lb_ref,
                  o_ref, *, d, alpha):
    ya = _dot(hg_ref[...], wa_ref[...])
    yb = _dot(fy_ref[...], wb_ref[...])
    merged = jax.nn.sigmoid(ga_ref[...].astype(F32)) * ya + jax.nn.sigmoid(gb_ref[...].astype(F32)) * yb
    out = _dot(merged.astype(BF16), wo_ref[...])
    g1 = mod_ref[...][:, 2 * d:3 * d]
    o_ref[...] = _layer_norm(alpha * x_ref[...] + g1 * out, lg_ref[...], lb_ref[...])


def _merge(hg, fy, proj, x, mod_l, mod_row, wa, wb, wo, ln_g, ln_b, tm, ga_block, alpha):
    rows, d = x.shape
    dm = hg.shape[1]
    df = fy.shape[1]
    const = lambda i: (0, 0)
    return pl.pallas_call(
        functools.partial(_merge_kernel, d=d, alpha=alpha),
        grid=(rows // tm,),
        in_specs=[
            pl.BlockSpec((tm, dm), lambda i: (i, 0)),
            pl.BlockSpec((tm, df), lambda i: (i, 0)),
            pl.BlockSpec((tm, d), lambda i: (i, ga_block)),
            pl.BlockSpec((tm, d), lambda i: (i, ga_block + 1)),
            pl.BlockSpec((tm, d), lambda i: (i, 0)),
            pl.BlockSpec((None, 1, 6 * d), lambda i: (mod_row(i, tm), 0, 0)),
            pl.BlockSpec((dm, d), const),
            pl.BlockSpec((df, d), const),
            pl.BlockSpec((d, d), const),
            pl.BlockSpec((1, d), const),
            pl.BlockSpec((1, d), const),
        ],
        out_specs=pl.BlockSpec((tm, d), lambda i: (i, 0)),
        out_shape=jax.ShapeDtypeStruct((rows, d), F32),
        compiler_params=_params("parallel"),
        name="merge",
    )(hg, fy, proj, proj, x, mod_l, wa, wb, wo, ln_g, ln_b)


def _ffn_kernel(x_ref, mod_ref, wa_ref, wu_ref, w2_ref, lg_ref, lb_ref, o_ref, h_ref, acc_ref, *, d, alpha):
    k = pl.program_id(1)

    @pl.when(k == 0)
    def _():
        m = mod_ref[...]
        h_ref[...] = (x_ref[...] * (1.0 + m[:, 4 * d:5 * d]) + m[:, 3 * d:4 * d]).astype(BF16)
        acc_ref[...] = jnp.zeros_like(acc_ref)

    hb = h_ref[...]
    a = _dot(hb, wa_ref[...])
    u = _dot(hb, wu_ref[...])
    acc_ref[...] += _dot((a * jax.nn.sigmoid(a) * u).astype(BF16), w2_ref[...])

    @pl.when(k == pl.num_programs(1) - 1)
    def _():
        g2 = mod_ref[...][:, 5 * d:6 * d]
        o_ref[...] = _layer_norm(alpha * x_ref[...] + g2 * acc_ref[...], lg_ref[...], lb_ref[...])


def _ffn(x, mod_l, mod_row, w1, w2, ln_g, ln_b, tm, tk, alpha):
    rows, d = x.shape
    dff = w2.shape[0]
    nk = dff // tk
    return pl.pallas_call(
        functools.partial(_ffn_kernel, d=d, alpha=alpha),
        grid=(rows // tm, nk),
        in_specs=[
            pl.BlockSpec((tm, d), lambda i, k: (i, 0)),
            pl.BlockSpec((None, 1, 6 * d), lambda i, k: (mod_row(i, tm), 0, 0)),
            pl.BlockSpec((d, tk), lambda i, k: (0, k)),
            pl.BlockSpec((d, tk), lambda i, k: (0, nk + k)),
            pl.BlockSpec((tk, d), lambda i, k: (k, 0)),
            pl.BlockSpec((1, d), lambda i, k: (0, 0)),
            pl.BlockSpec((1, d), lambda i, k: (0, 0)),
        ],
        out_specs=pl.BlockSpec((tm, d), lambda i, k: (i, 0)),
        out_shape=jax.ShapeDtypeStruct((rows, d), F32),
        scratch_shapes=[pltpu.VMEM((tm, d), BF16), pltpu.VMEM((tm, d), F32)],
        compiler_params=_params("parallel", "arbitrary"),
        name="ffn",
    )(x, mod_l, w1, w1, w2, ln_g, ln_b)


def kernel(x_prompt, x_sample, c, state_C, state_n, state_m, c_ctx, w_mod, b_mod, w_in, b_gate, mh_gain,
           w_branch_a, w_branch_b, w_out, ln_gain, ln_bias, w_ffn_in, w_ffn_out):
    batch, seq_ctx, d = x_prompt.shape
    dec_batch, seq_lat, _ = x_sample.shape
    depth = w_in.shape[0]
    dm = w_branch_a.shape[1]
    df = w_branch_b.shape[1]
    dv = dm // N_HEADS
    dqk = dv // 2
    dg = df // N_FGROUPS
    n_gates = 4 * N_HEADS
    q_end = N_HEADS * dqk
    k_end = 2 * q_end
    o_end = k_end + 2 * dm
    g_end = o_end + n_gates
    alpha = float((2 * depth) ** 0.25)

    mod_rows = 16
    cvec = jnp.zeros((mod_rows, d), F32).at[0].set(c_ctx).at[1:1 + dec_batch].set(c)
    mod = _modulation(cvec, w_mod, b_mod).reshape(depth, mod_rows, 1, 6 * d)

    w_main = jnp.concatenate([w_in[:, :, :o_end], w_in[:, :, g_end:]], axis=2).astype(BF16)
    n_main = w_main.shape[2]
    colscale = jnp.ones((1, n_main), F32).at[:, q_end:k_end].set(dqk ** -0.5)
    wg = w_in[:, :, o_end:g_end].reshape(depth, d, 2, 2, N_HEADS).transpose(0, 1, 4, 2, 3)
    wg = wg.reshape(depth, d, N_HEADS, 4)
    wg = jnp.pad(wg, ((0, 0), (0, 0), (0, 0), (0, GATE_STRIDE - 4))).reshape(depth, d, N_HEADS * GATE_STRIDE)
    wg = jnp.pad(wg, ((0, 0), (0, 0), (0, GATE_LANES - N_HEADS * GATE_STRIDE))).astype(BF16)
    bg = b_gate.reshape(depth, 2, 2, N_HEADS).transpose(0, 3, 1, 2).reshape(depth, N_HEADS, 4)
    bg = jnp.pad(bg, ((0, 0), (0, 0), (0, GATE_STRIDE - 4))).reshape(depth, 1, N_HEADS * GATE_STRIDE)
    bg = jnp.pad(bg, ((0, 0), (0, 0), (0, GATE_LANES - N_HEADS * GATE_STRIDE)))
    wa = w_branch_a.astype(BF16)
    wb = w_branch_b.astype(BF16)
    wo = w_out.astype(BF16)
    w1 = w_ffn_in.astype(BF16)
    w2 = w_ffn_out.astype(BF16)
    gain = mh_gain.reshape(depth, N_HEADS, 1, dv)

    tables_ctx = _dft_tables(seq_ctx, dg, None)
    tables_lat = _dft_tables(seq_lat, dg, GRID_W)
    f_block0 = o_end // dg
    ga_block = (o_end + df) // d

    def ctx_row(i, tm):
        return 0

    def lat_row(i, tm):
        return 1 + (i * tm) // seq_lat

    def layer(x, l, seq, mod_row, tables, init):
        proj, gcol, grow = _inproj(x, mod[l], mod_row, w_main[l], colscale, wg[l], bg[l], tm=512, tn=1024)
        res = _mlstm(proj, gcol, grow, gain[l], seq, l, init)
        fy = _fourier(proj, tables, seq, f_block0, df)
        x = _merge(res[0], fy, proj, x, mod[l], mod_row, wa[l], wb[l], wo[l],
                   ln_gain[l, 0][None], ln_bias[l, 0][None], tm=512, ga_block=ga_block, alpha=alpha)
        x = _ffn(x, mod[l], mod_row, w1[l], w2[l], ln_gain[l, 1][None], ln_bias[l, 1][None],
                 tm=512, tk=256, alpha=alpha)
        return x, res[1:]

    xp = x_prompt.reshape(batch * seq_ctx, d)
    xs = x_sample.reshape(dec_batch * seq_lat, d)
    new_c, new_n, new_m = [], [], []
    for l in range(depth):
        xp, (cfin, nfin, mfin) = layer(xp, l, seq_ctx, ctx_row, tables_ctx, None)
        new_c.append(cfin)
        new_n.append(nfin.transpose(0, 2, 1, 3))
        new_m.append(mfin[..., 0].transpose(0, 2, 1))
        xs, _ = layer(xs, l, seq_lat, lat_row, tables_lat, (state_C, state_n, state_m))
    return (xp.reshape(batch, seq_ctx, d), xs.reshape(dec_batch, seq_lat, d),
            jnp.stack(new_c, axis=1), jnp.stack(new_n, axis=1), jnp.stack(new_m, axis=1))
```

```python
import functools
import math

import numpy as np
import jax
import jax.numpy as jnp
from jax import lax
from jax.experimental import pallas as pl
from jax.experimental.pallas import tpu as pltpu

F32 = jnp.float32
BF16 = jnp.bfloat16

N_HEADS = 4
N_FGROUPS = 4
GRID_W = 64
LN_EPS = 1e-5
MLSTM_CHUNK = 256
GATE_LANES = 128
GATE_STRIDE = 8
ROW_TILE = 512
PROJ_COLS = 1024
FFN_COLS = 256
VMEM_LIMIT = 56 * 1024 * 1024
LOG2E = math.log2(math.e)

NT_DIMS = (((1,), (1,)), ((), ()))
TN_DIMS = (((0,), (0,)), ((), ()))


def _dot(a, b):
    return jnp.dot(a, b, preferred_element_type=F32)


def _dot_nt(a, b):
    return lax.dot_general(a, b, NT_DIMS, preferred_element_type=F32)


def _dot_tn(a, b):
    return lax.dot_general(a, b, TN_DIMS, preferred_element_type=F32)


def _split3(x):
    hi = x.astype(BF16)
    r = x - hi.astype(F32)
    mid = r.astype(BF16)
    lo = (r - mid.astype(F32)).astype(BF16)
    return hi, mid, lo


def _layer_norm(y, g, b):
    mu = jnp.mean(y, axis=-1, keepdims=True)
    yc = y - mu
    var = jnp.mean(yc * yc, axis=-1, keepdims=True)
    return yc * lax.rsqrt(var + LN_EPS) * g + b


def _log_sigmoid(x):
    return jnp.minimum(x, 0.0) - jnp.log(1.0 + jnp.exp(-jnp.abs(x)))


def _params(*sem):
    return pltpu.CompilerParams(dimension_semantics=sem, vmem_limit_bytes=VMEM_LIMIT)


def _resident(shape, index_map):
    return pl.BlockSpec(shape, index_map, pipeline_mode=pl.Buffered(1))


def _mod_kernel(c_ref, w_ref, b_ref, o_ref):
    c = c_ref[...]
    s = (c * jax.nn.sigmoid(c)).astype(BF16)
    o_ref[...] = _dot(s, w_ref[...].astype(BF16)) + b_ref[...]


def _modulation(cvec, w_mod, b_mod):
    depth, d, n6 = w_mod.shape
    rows = cvec.shape[0]
    tn = 1536
    return pl.pallas_call(
        _mod_kernel,
        grid=(depth, n6 // tn),
        in_specs=[
            pl.BlockSpec((rows, d), lambda l, j: (0, 0)),
            pl.BlockSpec((None, d, tn), lambda l, j: (l, 0, j)),
            pl.BlockSpec((None, 1, tn), lambda l, j: (l, 0, j)),
        ],
        out_specs=pl.BlockSpec((None, rows, tn), lambda l, j: (l, 0, j)),
        out_shape=jax.ShapeDtypeStruct((depth, rows, n6), F32),
        compiler_params=_params("parallel", "parallel"),
        name="modulation",
    )(cvec, w_mod, b_mod.reshape(depth, 1, n6))


def _inproj_kernel(x_ref, mod_ref, w_ref, wg_ref, bg_ref, proj_ref, gcol_ref, grow_ref, h_ref, *, d, epilogues):
    tm = x_ref.shape[0]
    L = MLSTM_CHUNK
    m = mod_ref[...]
    hb = (x_ref[...] * (1.0 + m[:, d:2 * d]) + m[:, 0:d]).astype(BF16)
    h_ref[...] = hb

    g = _dot(hb, wg_ref[...]) + bg_ref[...]
    lane = lax.broadcasted_iota(jnp.int32, g.shape, 1)
    g = jnp.where((lane & 1) == 1, _log_sigmoid(g), g)
    n_rows = N_HEADS * GATE_STRIDE
    gt = g.T[0:n_rows, :]
    r_i = lax.broadcasted_iota(jnp.int32, (L, L), 0)
    c_i = lax.broadcasted_iota(jnp.int32, (L, L), 1)
    upper_b = (r_i <= c_i).astype(BF16)
    lower_b = (r_i >= c_i).astype(BF16)
    kind = lax.broadcasted_iota(jnp.int32, (n_rows, L), 0) & 3

    def tri_right(parts, tri):
        return _dot(parts[0], tri) + _dot(parts[1], tri) + _dot(parts[2], tri)

    pieces = []
    for c in range(tm // L):
        xg = gt[:, c * L:(c + 1) * L]
        parts = _split3(xg)
        pre = tri_right(parts, upper_b)
        suf = tri_right(parts, lower_b)
        pieces.append(jnp.where(kind == 1, pre, jnp.where(kind == 3, suf, xg)))
    gs = jnp.concatenate(pieces, axis=1)
    for hh in range(N_HEADS):
        grow_ref[hh] = gs[hh * GATE_STRIDE:(hh + 1) * GATE_STRIDE, :]
    gcol_ref[...] = jnp.concatenate([gs, jnp.zeros((GATE_LANES - n_rows, tm), F32)], axis=0).T

    tn = PROJ_COLS
    for jn, fn in enumerate(epilogues):
        acc = _dot(h_ref[...], w_ref[:, jn * tn:(jn + 1) * tn])
        proj_ref[:, jn * tn:(jn + 1) * tn] = fn(acc).astype(BF16)


def _inproj(x, mod, layer, mod_row, w_main, w_gate, b_gate, epilogues):
    rows, d = x.shape
    n = w_main.shape[2]
    tm = ROW_TILE
    assert n == PROJ_COLS * len(epilogues) and rows % tm == 0 and tm % MLSTM_CHUNK == 0
    return pl.pallas_call(
        functools.partial(_inproj_kernel, d=d, epilogues=epilogues),
        grid=(rows // tm,),
        in_specs=[
            pl.BlockSpec((tm, d), lambda i: (i, 0)),
            pl.BlockSpec((None, None, 1, 6 * d), lambda i: (layer, mod_row(i, tm), 0, 0)),
            _resident((None, d, n), lambda i: (layer, 0, 0)),
            _resident((None, d, GATE_LANES), lambda i: (layer, 0, 0)),
            _resident((None, 1, GATE_LANES), lambda i: (layer, 0, 0)),
        ],
        out_specs=[
            pl.BlockSpec((tm, n), lambda i: (i, 0)),
            pl.BlockSpec((tm, GATE_LANES), lambda i: (i, 0)),
            pl.BlockSpec((N_HEADS, GATE_STRIDE, tm), lambda i: (0, 0, i)),
        ],
        out_shape=[
            jax.ShapeDtypeStruct((rows, n), BF16),
            jax.ShapeDtypeStruct((rows, GATE_LANES), F32),
            jax.ShapeDtypeStruct((N_HEADS, GATE_STRIDE, rows), F32),
        ],
        scratch_shapes=[pltpu.VMEM((tm, d), BF16)],
        compiler_params=_params("parallel"),
        name="inproj",
    )(x, mod, w_main, w_gate, b_gate)


def _mlstm_kernel(*refs, seq, has_init, layer, alias_state):
    if has_init:
        (q_ref, k_ref, v_ref, og_ref, gcol_ref, grow_ref, gain_ref, c0_ref, n0_ref, m0_ref,
         out_ref, cc_ref) = refs
    elif alias_state:
        (q_ref, k_ref, v_ref, og_ref, gcol_ref, grow_ref, gain_ref, _,
         out_ref, cfin_ref, nfin_ref, mfin_ref) = refs
    else:
        (q_ref, k_ref, v_ref, og_ref, gcol_ref, grow_ref, gain_ref,
         out_ref, cfin_ref, nfin_ref, mfin_ref) = refs
    L = MLSTM_CHUNK
    nc = seq // L
    b_idx = pl.program_id(0)
    h_idx = pl.program_id(1)
    dqk = q_ref.shape[1]

    rows_i = lax.broadcasted_iota(jnp.int32, (L, L), 0)
    cols_i = lax.broadcasted_iota(jnp.int32, (L, L), 1)
    masks = (rows_i >= cols_i, rows_i <= cols_i)
    lane = lax.broadcasted_iota(jnp.int32, (L, GATE_LANES), 1)
    lane0 = h_idx * GATE_STRIDE

    def gate_terms(c):
        gc = gcol_ref[pl.ds(c * L, L), :]
        gr = grow_ref[:, pl.ds(c * L, L)]

        def pick(jj):
            return jnp.sum(jnp.where(lane == lane0 + jj, gc, 0.0), axis=1, keepdims=True)

        fwd = (pick(1), gr[1:2, :], pick(0), gr[0:1, :], gr[1:2, L - 1:L])
        bwd = (pick(3), gr[3:4, :], pick(2), gr[2:3, :], gr[3:4, 0:1])
        return fwd, bwd

    def state_update(c, terms, m_prev):
        b_col, _, i_col, _, b_tot = terms
        g_col = b_tot - b_col + i_col
        carried = b_tot + m_prev
        m_new = jnp.maximum(carried, jnp.max(g_col, axis=0, keepdims=True))
        kw = k_ref[pl.ds(c * L, L), :].astype(F32) * jnp.exp(g_col - m_new)
        return m_new, jnp.exp(carried - m_new), kw

    terms = [gate_terms(c) for c in range(nc)]

    m_pre = [[None] * nc for _ in range(2)]
    n_pre = [[None] * nc for _ in range(2)]
    for d in range(2):
        order = list(range(nc)) if d == 0 else list(range(nc - 1, -1, -1))
        if has_init:
            m = jnp.full((1, 1), m0_ref[b_idx, layer, d, h_idx], F32)
            n = n0_ref[d, pl.ds(h_idx, 1), :]
            ct = c0_ref[d].T
            for pos, c in enumerate(order):
                m_pre[d][c] = m
                n_pre[d][c] = n
                cc_ref[c, d * dqk:(d + 1) * dqk, :] = ct.astype(BF16)
                if pos == nc - 1:
                    break
                m, decay, kw = state_update(c, terms[c][d], m)
                ct = decay * ct + _dot_tn(kw.astype(BF16), v_ref[pl.ds(c * L, L), :])
                n = decay * n + jnp.sum(kw, axis=0, keepdims=True)
        else:
            m0 = jnp.zeros((1, 1), F32)
            m_pre[d][0] = m0
            m, _, kw = state_update(0, terms[0][d], m0)
            cfin_ref[d] = _dot_tn(v_ref[...], kw.astype(BF16))
            nfin_ref[pl.ds(d, 1), :] = jnp.sum(kw, axis=0, keepdims=True)
            mfin_ref[pl.ds(d, 1), :] = jnp.broadcast_to(m, (1, GATE_LANES))

    gain = gain_ref[...]
    for c in range(nc):
        q = q_ref[pl.ds(c * L, L), :]
        k = k_ref[pl.ds(c * L, L), :]
        v = v_ref[pl.ds(c * L, L), :]
        qk = _dot_nt(q, k)
        if has_init:
            qf = q.astype(F32)
            n_rows = jnp.concatenate(
                [n_pre[0][c], n_pre[1][c], jnp.zeros((GATE_LANES - 2, dqk), F32)], axis=0).astype(BF16)
            qn = _dot_nt(q, n_rows)
        p_sum = None
        q_scaled = []
        for d in range(2):
            b_col, b_row, _, i_row, _ = terms[c][d]
            em = jnp.where(masks[d], (i_row - b_row) * LOG2E, -jnp.inf)
            b2 = b_col * LOG2E
            a2 = b2 + m_pre[d][c] * LOG2E
            mt2 = jnp.maximum(a2, b2 + jnp.max(em, axis=1, keepdims=True))
            s = qk * jnp.exp2((b2 - mt2) + em)
            den = jnp.sum(s, axis=1, keepdims=True)
            if has_init:
                inter = jnp.exp2(a2 - mt2)
                den = den + inter * qn[:, d:d + 1]
            r = 1.0 / jnp.maximum(jnp.abs(den), jnp.exp2(-mt2))
            p = s * r
            p_sum = p if p_sum is None else p_sum + p
            if has_init:
                q_scaled.append((qf * (inter * r)).astype(BF16))
        h = _dot(p_sum.astype(BF16), v)
        if has_init:
            h = h + _dot(jnp.concatenate(q_scaled, axis=1), cc_ref[c])
        mu = jnp.mean(h, axis=-1, keepdims=True)
        hc = h - mu
        var = jnp.mean(hc * hc, axis=-1, keepdims=True)
        hn = hc * lax.rsqrt(var + LN_EPS) * gain
        out_ref[pl.ds(c * L, L), :] = hn.astype(BF16) * og_ref[pl.ds(c * L, L), :]


def _mlstm(proj, gcol, grow, gain, seq, layer, depth, init=None, state_c=None):
    rows = proj.shape[0]
    n_seq = rows // seq
    dv = gain.shape[3]
    dm = N_HEADS * dv
    dqk = dv // 2
    kb = N_HEADS
    vb = (2 * N_HEADS * dqk) // dv
    ob = vb + N_HEADS
    has_init = init is not None
    assert seq % MLSTM_CHUNK == 0 and (has_init or seq == MLSTM_CHUNK)
    in_specs = [
        pl.BlockSpec((seq, dqk), lambda b, h: (b, h)),
        pl.BlockSpec((seq, dqk), lambda b, h: (b, kb + h)),
        pl.BlockSpec((seq, dv), lambda b, h: (b, vb + h)),
        pl.BlockSpec((seq, dv), lambda b, h: (b, ob + h)),
        pl.BlockSpec((seq, GATE_LANES), lambda b, h: (b, 0)),
        pl.BlockSpec((None, GATE_STRIDE, seq), lambda b, h: (h, 0, b)),
        pl.BlockSpec((None, None, 1, dv), lambda b, h: (layer, h, 0, 0)),
    ]
    args = [proj, proj, proj, proj, gcol, grow, gain]
    out_specs = [pl.BlockSpec((seq, dv), lambda b, h: (b, h))]
    out_shape = [jax.ShapeDtypeStruct((rows, dm), BF16)]
    scratch = []
    aliases = {}
    if has_init:
        init_c, init_n, init_m = init
        in_specs += [
            pl.BlockSpec((None, None, 2, None, dv, dqk), lambda b, h: (b, layer, 0, h, 0, 0)),
            pl.BlockSpec((None, None, 2, N_HEADS, dqk), lambda b, h: (b, layer, 0, 0, 0)),
            pl.BlockSpec(memory_space=pltpu.MemorySpace.SMEM),
        ]
        args += [init_c, init_n, init_m]
        scratch = [pltpu.VMEM((seq // MLSTM_CHUNK, 2 * dqk, dv), BF16)]
    else:
        if state_c is not None:
            in_specs.append(pl.BlockSpec(memory_space=pl.ANY))
            args.append(state_c)
            aliases = {len(args) - 1: 1}
        out_specs += [
            pl.BlockSpec((None, None, 2, None, dv, dqk), lambda b, h: (b, layer, 0, h, 0, 0)),
            pl.BlockSpec((None, None, 2, dqk), lambda b, h: (b, h, 0, 0)),
            pl.BlockSpec((None, None, 2, GATE_LANES), lambda b, h: (b, h, 0, 0)),
        ]
        out_shape += [
            jax.ShapeDtypeStruct((n_seq, depth, 2, N_HEADS, dv, dqk), F32),
            jax.ShapeDtypeStruct((n_seq, N_HEADS, 2, dqk), F32),
            jax.ShapeDtypeStruct((n_seq, N_HEADS, 2, GATE_LANES), F32),
        ]
    return pl.pallas_call(
        functools.partial(_mlstm_kernel, seq=seq, has_init=has_init, layer=layer,
                          alias_state=state_c is not None),
        grid=(n_seq, N_HEADS),
        in_specs=in_specs,
        out_specs=out_specs,
        out_shape=out_shape,
        scratch_shapes=scratch,
        input_output_aliases=aliases,
        compiler_params=_params("parallel", "parallel"),
        name="mlstm_lat" if has_init else "mlstm_ctx",
    )(*args)


def _fourier_kernel(u_ref, cc_ref, sc_ref, ms_ref, o_ref, ab_ref, *, seq, dg):
    for g in range(u_ref.shape[1] // dg):
        u = u_ref[:, g * dg:(g + 1) * dg]
        ab_ref[0:seq, g * dg:(g + 1) * dg] = _dot(u, cc_ref[...]).astype(BF16)
        ab_ref[seq:2 * seq, g * dg:(g + 1) * dg] = _dot(u, sc_ref[...]).astype(BF16)
    o_ref[...] = _dot(ms_ref[...], ab_ref[...]).astype(BF16)


def _dft_tables(seq, dg, grid_w):
    ch = np.arange(dg)
    ang_c = 2.0 * np.pi * ((np.outer(ch, ch) % dg) / dg)
    sc_c = 1.0 / np.sqrt(dg)
    cc = np.cos(ang_c) * sc_c
    sc = np.sin(ang_c) * sc_c
    t = np.arange(seq)
    if grid_w is None:
        frac = (np.outer(t, t) % seq) / seq
    else:
        rows = seq // grid_w
        r, w = t // grid_w, t % grid_w
        frac = (np.outer(r, r) % rows) / rows + (np.outer(w, w) % grid_w) / grid_w
    ang_s = 2.0 * np.pi * frac
    sc_s = 1.0 / np.sqrt(seq)
    ms = np.concatenate([np.cos(ang_s) * sc_s, -np.sin(ang_s) * sc_s], axis=1)
    return (jnp.asarray(cc, F32).astype(BF16), jnp.asarray(sc, F32).astype(BF16),
            jnp.asarray(ms, F32).astype(BF16))


def _fourier(proj, tables, seq, col_block0, df):
    rows = proj.shape[0]
    cc, sc, ms = tables
    dg = cc.shape[0]
    assert col_block0 % (df // dg) == 0
    return pl.pallas_call(
        functools.partial(_fourier_kernel, seq=seq, dg=dg),
        grid=(rows // seq,),
        in_specs=[
            pl.BlockSpec((seq, df), lambda b: (b, col_block0 * dg // df)),
            _resident((dg, dg), lambda b: (0, 0)),
            _resident((dg, dg), lambda b: (0, 0)),
            _resident((seq, 2 * seq), lambda b: (0, 0)),
        ],
        out_specs=pl.BlockSpec((seq, df), lambda b: (b, 0)),
        out_shape=jax.ShapeDtypeStruct((rows, df), BF16),
        scratch_shapes=[pltpu.VMEM((2 * seq, df), BF16)],
        compiler_params=_params("parallel"),
        name="fourier",
    )(proj, cc, sc, ms)


def _merge_kernel(hg_ref, fy_ref, ga_ref, gb_ref, x_ref, mod_ref, wa_ref, wb_ref, wo_ref, lg_ref, lb_ref,
                  o_ref, *, d, alpha):
    ya = _dot(hg_ref[...], wa_ref[...])
    yb = _dot(fy_ref[...], wb_ref[...])
    merged = ga_ref[...].astype(F32) * ya + gb_ref[...].astype(F32) * yb
    out = _dot(merged.astype(BF16), wo_ref[...])
    g1 = mod_ref[...][:, 2 * d:3 * d]
    o_ref[...] = _layer_norm(alpha * x_ref[...] + g1 * out, lg_ref[...], lb_ref[...])


def _merge(hg, fy, proj, x, mod, layer, mod_row, wa, wb, wo, ln_g, ln_b, ga_block, alpha):
    rows, d = x.shape
    dm = hg.shape[1]
    df = fy.shape[1]
    tm = ROW_TILE
    wmap = lambda i: (layer, 0, 0)
    return pl.pallas_call(
        functools.partial(_merge_kernel, d=d, alpha=alpha),
        grid=(rows // tm,),
        in_specs=[
            pl.BlockSpec((tm, dm), lambda i: (i, 0)),
            pl.BlockSpec((tm, df), lambda i: (i, 0)),
            pl.BlockSpec((tm, d), lambda i: (i, ga_block)),
            pl.BlockSpec((tm, d), lambda i: (i, ga_block + 1)),
            pl.BlockSpec((tm, d), lambda i: (i, 0)),
            pl.BlockSpec((None, None, 1, 6 * d), lambda i: (layer, mod_row(i, tm), 0, 0)),
            _resident((None, dm, d), wmap),
            _resident((None, df, d), wmap),
            _resident((None, d, d), wmap),
            _resident((None, None, 1, d), lambda i: (layer, 0, 0, 0)),
            _resident((None, None, 1, d), lambda i: (layer, 0, 0, 0)),
        ],
        out_specs=pl.BlockSpec((tm, d), lambda i: (i, 0)),
        out_shape=jax.ShapeDtypeStruct((rows, d), F32),
        compiler_params=_params("parallel"),
        name="merge",
    )(hg, fy, proj, proj, x, mod, wa, wb, wo, ln_g, ln_b)


def _ffn_kernel(x_ref, mod_ref, w1_ref, w2_ref, lg_ref, lb_ref, o_ref, h_ref, g_ref, *, d, alpha):
    dff = w2_ref.shape[0]
    m = mod_ref[...]
    x = x_ref[...]
    h_ref[...] = (x * (1.0 + m[:, 4 * d:5 * d]) + m[:, 3 * d:4 * d]).astype(BF16)
    tk = FFN_COLS
    for kk in range(dff // tk):
        a = _dot(h_ref[...], w1_ref[:, kk * tk:(kk + 1) * tk])
        u = _dot(h_ref[...], w1_ref[:, dff + kk * tk:dff + (kk + 1) * tk])
        g_ref[:, kk * tk:(kk + 1) * tk] = (a * jax.nn.sigmoid(a) * u).astype(BF16)
    f = _dot(g_ref[...], w2_ref[...])
    g2 = m[:, 5 * d:6 * d]
    o_ref[...] = _layer_norm(alpha * x + g2 * f, lg_ref[...], lb_ref[...])


def _ffn(x, mod, layer, mod_row, w1, w2, ln_g, ln_b, alpha):
    rows, d = x.shape
    dff = w2.shape[1]
    tm = ROW_TILE
    assert dff % FFN_COLS == 0
    return pl.pallas_call(
        functools.partial(_ffn_kernel, d=d, alpha=alpha),
        grid=(rows // tm,),
        in_specs=[
            pl.BlockSpec((tm, d), lambda i: (i, 0)),
            pl.BlockSpec((None, None, 1, 6 * d), lambda i: (layer, mod_row(i, tm), 0, 0)),
            _resident((None, d, 2 * dff), lambda i: (layer, 0, 0)),
            _resident((None, dff, d), lambda i: (layer, 0, 0)),
            _resident((None, None, 1, d), lambda i: (layer, 1, 0, 0)),
            _resident((None, None, 1, d), lambda i: (layer, 1, 0, 0)),
        ],
        out_specs=pl.BlockSpec((tm, d), lambda i: (i, 0)),
        out_shape=jax.ShapeDtypeStruct((rows, d), F32),
        scratch_shapes=[pltpu.VMEM((tm, d), BF16), pltpu.VMEM((tm, dff), BF16)],
        compiler_params=_params("parallel"),
        name="ffn",
    )(x, mod, w1, w2, ln_g, ln_b)


def kernel(x_prompt, x_sample, c, state_C, state_n, state_m, c_ctx, w_mod, b_mod, w_in, b_gate, mh_gain,
           w_branch_a, w_branch_b, w_out, ln_gain, ln_bias, w_ffn_in, w_ffn_out):
    batch, seq_ctx, d = x_prompt.shape
    dec_batch, seq_lat, _ = x_sample.shape
    depth = w_in.shape[0]
    dm = w_branch_a.shape[1]
    df = w_branch_b.shape[1]
    dv = dm // N_HEADS
    dqk = dv // 2
    dg = df // N_FGROUPS
    n_gates = 4 * N_HEADS
    q_end = N_HEADS * dqk
    k_end = 2 * q_end
    v_end = k_end + dm
    o_end = v_end + dm
    g_end = o_end + n_gates
    alpha = float((2 * depth) ** 0.25)

    mod_rows = 16
    cvec = jnp.zeros((mod_rows, d), F32).at[0].set(c_ctx).at[1:1 + dec_batch].set(c)
    mod = _modulation(cvec, w_mod, b_mod).reshape(depth, mod_rows, 1, 6 * d)

    w_main = jnp.concatenate([w_in[:, :, :o_end], w_in[:, :, g_end:]], axis=2).astype(BF16)
    wg = w_in[:, :, o_end:g_end].reshape(depth, d, 2, 2, N_HEADS).transpose(0, 1, 4, 2, 3)
    wg = wg.reshape(depth, d, N_HEADS, 4)
    wg = jnp.pad(wg, ((0, 0), (0, 0), (0, 0), (0, GATE_STRIDE - 4))).reshape(depth, d, N_HEADS * GATE_STRIDE)
    wg = jnp.pad(wg, ((0, 0), (0, 0), (0, GATE_LANES - N_HEADS * GATE_STRIDE))).astype(BF16)
    bg = b_gate.reshape(depth, 2, 2, N_HEADS).transpose(0, 3, 1, 2).reshape(depth, N_HEADS, 4)
    bg = jnp.pad(bg, ((0, 0), (0, 0), (0, GATE_STRIDE - 4))).reshape(depth, 1, N_HEADS * GATE_STRIDE)
    bg = jnp.pad(bg, ((0, 0), (0, 0), (0, GATE_LANES - N_HEADS * GATE_STRIDE)))
    wa = w_branch_a.astype(BF16)
    wb = w_branch_b.astype(BF16)
    wo = w_out.astype(BF16)
    w1 = w_ffn_in.astype(BF16)
    w2 = w_ffn_out.astype(BF16)
    gain = mh_gain.reshape(depth, N_HEADS, 1, dv)
    ln_g = ln_gain.reshape(depth, 2, 1, d)
    ln_b = ln_bias.reshape(depth, 2, 1, d)

    def slab_fn(col):
        if q_end <= col < k_end:
            return lambda t: t * (dqk ** -0.5)
        if v_end <= col < o_end or col >= o_end + df:
            return jax.nn.sigmoid
        return lambda t: t
    n_main = w_main.shape[2]
    assert all(e % PROJ_COLS == 0 for e in (q_end, k_end, v_end, o_end, o_end + df, n_main))
    epilogues = tuple(slab_fn(j * PROJ_COLS) for j in range(n_main // PROJ_COLS))

    tables_ctx = _dft_tables(seq_ctx, dg, None)
    tables_lat = _dft_tables(seq_lat, dg, GRID_W)
    f_block0 = o_end // dg
    ga_block = (o_end + df) // d

    def ctx_row(i, tm):
        return 0

    def lat_row(i, tm):
        return 1 + (i * tm) // seq_lat

    def layer(x, l, seq, mod_row, tables, init, state_c):
        proj, gcol, grow = _inproj(x, mod, l, mod_row, w_main, wg, bg, epilogues)
        res = _mlstm(proj, gcol, grow, gain, seq, l, depth, init, state_c)
        fy = _fourier(proj, tables, seq, f_block0, df)
        x = _merge(res[0], fy, proj, x, mod, l, mod_row, wa, wb, wo, ln_g, ln_b, ga_block, alpha)
        x = _ffn(x, mod, l, mod_row, w1, w2, ln_g, ln_b, alpha)
        return x, res[1:]

    xp = x_prompt.reshape(batch * seq_ctx, d)
    xs = x_sample.reshape(dec_batch * seq_lat, d)
    new_c, new_n, new_m = None, [], []
    for l in range(depth):
        xp, (new_c, nfin, mfin) = layer(xp, l, seq_ctx, ctx_row, tables_ctx, None, new_c)
        new_n.append(nfin.transpose(0, 2, 1, 3))
        new_m.append(mfin[..., 0].transpose(0, 2, 1))
        xs, _ = layer(xs, l, seq_lat, lat_row, tables_lat, (state_C, state_n, state_m), None)
    return (xp.reshape(batch, seq_ctx, d), xs.reshape(dec_batch, seq_lat, d),
            new_c, jnp.stack(new_n, axis=1), jnp.stack(new_m, axis=1))
```

```python
import functools
import math

import numpy as np
import jax
import jax.numpy as jnp
from jax import lax
from jax.experimental import pallas as pl
from jax.experimental.pallas import tpu as pltpu

F32 = jnp.float32
BF16 = jnp.bfloat16

N_HEADS = 4
N_FGROUPS = 4
GRID_W = 64
LN_EPS = 1e-5
MLSTM_CHUNK = 256
GATE_LANES = 128
GATE_STRIDE = 8
ROW_TILE = 512
PROJ_COLS = 1024
FFN_COLS = 256
VMEM_LIMIT = 56 * 1024 * 1024
LOG2E = math.log2(math.e)

NT_DIMS = (((1,), (1,)), ((), ()))
TN_DIMS = (((0,), (0,)), ((), ()))


def _dot(a, b):
    return jnp.dot(a, b, preferred_element_type=F32)


def _dot_nt(a, b):
    return lax.dot_general(a, b, NT_DIMS, preferred_element_type=F32)


def _dot_tn(a, b):
    return lax.dot_general(a, b, TN_DIMS, preferred_element_type=F32)


def _split3(x):
    hi = x.astype(BF16)
    r = x - hi.astype(F32)
    mid = r.astype(BF16)
    lo = (r - mid.astype(F32)).astype(BF16)
    return hi, mid, lo


def _layer_norm(y, g, b):
    mu = jnp.mean(y, axis=-1, keepdims=True)
    yc = y - mu
    var = jnp.mean(yc * yc, axis=-1, keepdims=True)
    return yc * lax.rsqrt(var + LN_EPS) * g + b


def _log_sigmoid(x):
    return jnp.minimum(x, 0.0) - jnp.log(1.0 + jnp.exp(-jnp.abs(x)))


def _params(*sem):
    return pltpu.CompilerParams(dimension_semantics=sem, vmem_limit_bytes=VMEM_LIMIT)


def _resident(shape, index_map):
    return pl.BlockSpec(shape, index_map, pipeline_mode=pl.Buffered(1))


def _mod_kernel(c_ref, w_ref, b_ref, o_ref):
    c = c_ref[...]
    s = (c * jax.nn.sigmoid(c)).astype(BF16)
    o_ref[...] = _dot(s, w_ref[...].astype(BF16)) + b_ref[...]


def _modulation(cvec, w_mod, b_mod):
    depth, d, n6 = w_mod.shape
    rows = cvec.shape[0]
    tn = 1536
    return pl.pallas_call(
        _mod_kernel,
        grid=(depth, n6 // tn),
        in_specs=[
            pl.BlockSpec((rows, d), lambda l, j: (0, 0)),
            pl.BlockSpec((None, d, tn), lambda l, j: (l, 0, j)),
            pl.BlockSpec((None, 1, tn), lambda l, j: (l, 0, j)),
        ],
        out_specs=pl.BlockSpec((None, rows, tn), lambda l, j: (l, 0, j)),
        out_shape=jax.ShapeDtypeStruct((depth, rows, n6), F32),
        compiler_params=_params("parallel", "parallel"),
        name="modulation",
    )(cvec, w_mod, b_mod.reshape(depth, 1, n6))


def _inproj_kernel(x_ref, mod_ref, wh_ref, wt_ref, wg_ref, bg_ref, proj_ref, gcol_ref, grow_ref, h_ref,
                   *, d, epilogues):
    tm = x_ref.shape[0]
    L = MLSTM_CHUNK
    G = GATE_STRIDE
    m = mod_ref[...]
    hb = (x_ref[...] * (1.0 + m[:, d:2 * d]) + m[:, 0:d]).astype(BF16)
    h_ref[...] = hb

    g = _dot(hb, wg_ref[...]) + bg_ref[...]
    lane = lax.broadcasted_iota(jnp.int32, g.shape, 1)
    g = jnp.where(((lane >> 3) & 1) == 1, _log_sigmoid(g), g)
    n_rows = 4 * G
    gt = g.T[0:n_rows, :]
    r_i = lax.broadcasted_iota(jnp.int32, (L, L), 0)
    c_i = lax.broadcasted_iota(jnp.int32, (L, L), 1)
    upper_b = (r_i <= c_i).astype(BF16)
    lower_b = (r_i >= c_i).astype(BF16)

    def cumsum(x, tri):
        hi, mid, lo = _split3(x)
        return _dot(hi, tri) + _dot(mid, tri) + _dot(lo, tri)

    pieces = []
    for c in range(tm // L):
        xg = gt[:, c * L:(c + 1) * L]
        pre = cumsum(xg[G:2 * G], upper_b)
        suf = cumsum(xg[3 * G:4 * G], lower_b)
        pieces.append(jnp.concatenate([xg[0:G] - pre, pre, xg[2 * G:3 * G] - suf, suf], axis=0))
    gs = jnp.concatenate(pieces, axis=1)
    for qq in range(4):
        grow_ref[qq] = gs[qq * G:(qq + 1) * G, :]
    gcol_ref[...] = jnp.concatenate([gs, jnp.zeros((GATE_LANES - n_rows, tm), F32)], axis=0).T

    tn = PROJ_COLS
    n_head = wh_ref.shape[1] // tn
    for jn, fn in enumerate(epilogues):
        if jn < n_head:
            w = wh_ref[:, jn * tn:(jn + 1) * tn]
        else:
            w = wt_ref[:, (jn - n_head) * tn:(jn - n_head + 1) * tn]
        proj_ref[:, jn * tn:(jn + 1) * tn] = fn(_dot(h_ref[...], w)).astype(BF16)


def _inproj(x, mod, layer, mod_row, w_head, w_tail, w_gate, b_gate, epilogues):
    rows, d = x.shape
    n_h, n_t = w_head.shape[2], w_tail.shape[2]
    n = n_h + n_t
    tm = ROW_TILE
    assert n == PROJ_COLS * len(epilogues) and n_h % PROJ_COLS == 0 and rows % tm == 0 and tm % MLSTM_CHUNK == 0
    return pl.pallas_call(
        functools.partial(_inproj_kernel, d=d, epilogues=epilogues),
        grid=(rows // tm,),
        in_specs=[
            pl.BlockSpec((tm, d), lambda i: (i, 0)),
            pl.BlockSpec((None, None, 1, 6 * d), lambda i: (layer, mod_row(i, tm), 0, 0)),
            _resident((None, d, n_h), lambda i: (layer, 0, 0)),
            _resident((None, d, n_t), lambda i: (layer, 0, 0)),
            _resident((None, d, GATE_LANES), lambda i: (layer, 0, 0)),
            _resident((None, 1, GATE_LANES), lambda i: (layer, 0, 0)),
        ],
        out_specs=[
            pl.BlockSpec((tm, n), lambda i: (i, 0)),
            pl.BlockSpec((tm, GATE_LANES), lambda i: (i, 0)),
            pl.BlockSpec((4, GATE_STRIDE, tm), lambda i: (0, 0, i)),
        ],
        out_shape=[
            jax.ShapeDtypeStruct((rows, n), BF16),
            jax.ShapeDtypeStruct((rows, GATE_LANES), F32),
            jax.ShapeDtypeStruct((4, GATE_STRIDE, rows), F32),
        ],
        scratch_shapes=[pltpu.VMEM((tm, d), BF16)],
        compiler_params=_params("parallel"),
        name="inproj",
    )(x, mod, w_head, w_tail, w_gate, b_gate)


def _mlstm_kernel(*refs, heads, **static):
    for hh in range(heads):
        _mlstm_head(hh, *refs, heads=heads, **static)


def _mlstm_head(hh, *refs, seq, has_init, layer, alias_state, heads):
    if has_init:
        (q_ref, k_ref, v_ref, og_ref, gcol_ref, grow_ref, gain_ref, c0_ref, n0_ref, m0_ref,
         out_ref, cc_ref) = refs
    elif alias_state:
        (q_ref, k_ref, v_ref, og_ref, gcol_ref, grow_ref, gain_ref, _,
         out_ref, cfin_ref, nfin_ref, mfin_ref) = refs
    else:
        (q_ref, k_ref, v_ref, og_ref, gcol_ref, grow_ref, gain_ref,
         out_ref, cfin_ref, nfin_ref, mfin_ref) = refs
    L = MLSTM_CHUNK
    G = GATE_STRIDE
    nc = seq // L
    b_idx = pl.program_id(0)
    h_idx = pl.program_id(1) * heads + hh
    dqk = q_ref.shape[1] // heads
    dv = v_ref.shape[1] // heads
    qc = slice(hh * dqk, (hh + 1) * dqk)
    vc = slice(hh * dv, (hh + 1) * dv)

    s_i = lax.broadcasted_iota(jnp.int32, (L, L), 0)
    t_i = lax.broadcasted_iota(jnp.int32, (L, L), 1)
    masks = (s_i <= t_i, s_i >= t_i)
    eye_b = (s_i == t_i).astype(BF16)
    ones_b = jnp.ones((G, L), BF16)
    lane = lax.broadcasted_iota(jnp.int32, (L, GATE_LANES), 1)

    def gate_rows(c):
        sl = pl.ds(c * L, L)
        hs = pl.ds(h_idx, 1)
        b_f = grow_ref[1, hs, sl]
        b_b = grow_ref[3, hs, sl]
        return ((grow_ref[0, hs, sl], b_f, b_f[:, L - 1:L]), (grow_ref[2, hs, sl], b_b, b_b[:, 0:1]))

    def e_column(c, d):
        gc = gcol_ref[pl.ds(c * L, L), :]
        return jnp.sum(jnp.where(lane == 2 * d * G + h_idx, gc, 0.0), axis=1, keepdims=True)

    def to_columns(rows):
        rep = jnp.concatenate([jnp.broadcast_to(r, (GATE_LANES, L)) for r in rows], axis=0)
        return _dot_nt(eye_b, rep.astype(BF16))

    def wide(col_tile):
        return jnp.concatenate([col_tile] * (dqk // GATE_LANES), axis=1)

    rows = [gate_rows(c) for c in range(nc)]

    def local_update(c):
        e_max, w_rows = [], []
        for d in range(2):
            e_row = rows[c][d][0]
            mx = jnp.max(e_row, axis=1, keepdims=True)
            e_max.append(mx)
            w_rows.append(jnp.exp(e_row - mx))
        w_cols = to_columns(w_rows).astype(BF16)
        k = k_ref[pl.ds(c * L, L), qc]
        kws = [k * wide(w_cols[:, d * GATE_LANES:(d + 1) * GATE_LANES]) for d in range(2)]
        return e_max, kws

    m_pre = [[None] * nc for _ in range(2)]
    n_pre = [[None] * nc for _ in range(2)]
    if has_init:
        local = {c: local_update(c) for c in range(nc)}
        for d in range(2):
            order = list(range(nc)) if d == 0 else list(range(nc - 1, -1, -1))
            m = jnp.full((1, 1), m0_ref[b_idx, layer, d, h_idx], F32)
            n = n0_ref[d, pl.ds(h_idx, 1), :]
            cm = c0_ref[d]
            for pos, c in enumerate(order):
                m_pre[d][c] = m
                n_pre[d][c] = n
                cc_ref[c, :, d * dqk:(d + 1) * dqk] = cm.astype(BF16)
                if pos == nc - 1:
                    break
                e_max, kws = local[c]
                b_tot = rows[c][d][2]
                g_max = b_tot + e_max[d]
                carried = b_tot + m
                m = jnp.maximum(carried, g_max)
                decay = jnp.exp(carried - m)
                up = jnp.exp(g_max - m)
                v = v_ref[pl.ds(c * L, L), vc]
                cm = decay * cm + up * _dot_tn(v, kws[d])
                n = decay * n + up * _dot(ones_b, kws[d])[0:1, :]
    else:
        e_max, kws = local_update(0)
        for d in range(2):
            b_tot = rows[0][d][2]
            g_max = b_tot + e_max[d]
            m_pre[d][0] = jnp.zeros((1, 1), F32)
            m = jnp.maximum(b_tot, g_max)
            up = jnp.exp(g_max - m)
            cfin_ref[d, hh] = up * _dot_tn(v_ref[:, vc], kws[d])
            nfin_ref[hh, pl.ds(d, 1), :] = up * _dot(ones_b, kws[d])[0:1, :]
            mfin_ref[hh, pl.ds(d, 1), :] = jnp.broadcast_to(m, (1, GATE_LANES))

    gain = gain_ref[hh]
    for c in range(nc):
        q = q_ref[pl.ds(c * L, L), qc]
        k = k_ref[pl.ds(c * L, L), qc]
        v = v_ref[pl.ds(c * L, L), vc]
        qk_t = _dot_nt(k, q)
        if has_init:
            n_rows = jnp.concatenate([n_pre[0][c], n_pre[1][c], jnp.zeros((G - 2, dqk), F32)], axis=0)
            qn_t = _dot_nt(n_rows.astype(BF16), q)
        p_t = None
        scale_rows = []
        for d in range(2):
            _, b_row, _ = rows[c][d]
            em = jnp.where(masks[d], e_column(c, d) * LOG2E, -jnp.inf)
            b2 = b_row * LOG2E
            a2 = b2 + m_pre[d][c] * LOG2E
            mt2 = jnp.maximum(a2, b2 + jnp.max(em, axis=0, keepdims=True))
            s_t = qk_t * jnp.exp2((b2 - mt2) + em)
            den = jnp.sum(s_t, axis=0, keepdims=True)
            if has_init:
                inter = jnp.exp2(a2 - mt2)
                den = den + inter * qn_t[d:d + 1, :]
            r = 1.0 / jnp.maximum(jnp.abs(den), jnp.exp2(-mt2))
            p_t = s_t * r if p_t is None else p_t + s_t * r
            if has_init:
                scale_rows.append(inter * r)
        h = _dot_tn(p_t.astype(BF16), v)
        if has_init:
            sc = to_columns(scale_rows).astype(BF16)
            qs = jnp.concatenate([q * wide(sc[:, d * GATE_LANES:(d + 1) * GATE_LANES]) for d in range(2)], axis=1)
            h = h + _dot_nt(qs, cc_ref[c])
        mu = jnp.mean(h, axis=-1, keepdims=True)
        hc = h - mu
        var = jnp.mean(hc * hc, axis=-1, keepdims=True)
        hn = hc * lax.rsqrt(var + LN_EPS) * gain
        out_ref[pl.ds(c * L, L), vc] = hn.astype(BF16) * og_ref[pl.ds(c * L, L), vc]


def _mlstm(proj, gcol, grow, gain, seq, layer, depth, init=None, state_c=None):
    rows = proj.shape[0]
    n_seq = rows // seq
    dv = gain.shape[3]
    dm = N_HEADS * dv
    dqk = dv // 2
    has_init = init is not None
    hps = 1 if has_init else N_HEADS
    hb = N_HEADS // hps
    vb = (2 * N_HEADS * dqk) // (hps * dv)
    ob = vb + hb
    assert seq % MLSTM_CHUNK == 0 and (has_init or seq == MLSTM_CHUNK) and dqk % GATE_LANES == 0
    in_specs = [
        pl.BlockSpec((seq, hps * dqk), lambda b, h: (b, h)),
        pl.BlockSpec((seq, hps * dqk), lambda b, h: (b, hb + h)),
        pl.BlockSpec((seq, hps * dv), lambda b, h: (b, vb + h)),
        pl.BlockSpec((seq, hps * dv), lambda b, h: (b, ob + h)),
        pl.BlockSpec((seq, GATE_LANES), lambda b, h: (b, 0)),
        pl.BlockSpec((4, GATE_STRIDE, seq), lambda b, h: (0, 0, b)),
        pl.BlockSpec((None, hps, 1, dv), lambda b, h: (layer, h, 0, 0)),
    ]
    args = [proj, proj, proj, proj, gcol, grow, gain]
    out_specs = [pl.BlockSpec((seq, hps * dv), lambda b, h: (b, h))]
    out_shape = [jax.ShapeDtypeStruct((rows, dm), BF16)]
    scratch = []
    aliases = {}
    if has_init:
        init_c, init_n, init_m = init
        in_specs += [
            pl.BlockSpec((None, None, 2, None, dv, dqk), lambda b, h: (b, layer, 0, h, 0, 0)),
            pl.BlockSpec((None, None, 2, N_HEADS, dqk), lambda b, h: (b, layer, 0, 0, 0)),
            pl.BlockSpec(memory_space=pltpu.MemorySpace.SMEM),
        ]
        args += [init_c, init_n, init_m]
        scratch = [pltpu.VMEM((seq // MLSTM_CHUNK, dv, 2 * dqk), BF16)]
    else:
        if state_c is not None:
            in_specs.append(pl.BlockSpec(memory_space=pl.ANY))
            args.append(state_c)
            aliases = {len(args) - 1: 1}
        out_specs += [
            pl.BlockSpec((None, None, 2, hps, dv, dqk), lambda b, h: (b, layer, 0, h, 0, 0)),
            pl.BlockSpec((None, hps, 2, dqk), lambda b, h: (b, h, 0, 0)),
            pl.BlockSpec((None, hps, 2, GATE_LANES), lambda b, h: (b, h, 0, 0)),
        ]
        out_shape += [
            jax.ShapeDtypeStruct((n_seq, depth, 2, N_HEADS, dv, dqk), F32),
            jax.ShapeDtypeStruct((n_seq, N_HEADS, 2, dqk), F32),
            jax.ShapeDtypeStruct((n_seq, N_HEADS, 2, GATE_LANES), F32),
        ]
    return pl.pallas_call(
        functools.partial(_mlstm_kernel, seq=seq, has_init=has_init, layer=layer,
                          alias_state=state_c is not None, heads=hps),
        grid=(n_seq, N_HEADS // hps),
        in_specs=in_specs,
        out_specs=out_specs,
        out_shape=out_shape,
        scratch_shapes=scratch,
        input_output_aliases=aliases,
        compiler_params=_params("parallel", "parallel"),
        name="mlstm_lat" if has_init else "mlstm_ctx",
    )(*args)


def _fourier_kernel(u_ref, cc_ref, sc_ref, ms_ref, o_ref, ab_ref, *, seq, dg):
    for g in range(u_ref.shape[1] // dg):
        u = u_ref[:, g * dg:(g + 1) * dg]
        ab_ref[0:seq, g * dg:(g + 1) * dg] = _dot(u, cc_ref[...]).astype(BF16)
        ab_ref[seq:2 * seq, g * dg:(g + 1) * dg] = _dot(u, sc_ref[...]).astype(BF16)
    o_ref[...] = _dot(ms_ref[...], ab_ref[...]).astype(BF16)


def _dft_tables(seq, dg, grid_w):
    ch = np.arange(dg)
    ang_c = 2.0 * np.pi * ((np.outer(ch, ch) % dg) / dg)
    sc_c = 1.0 / np.sqrt(dg)
    cc = np.cos(ang_c) * sc_c
    sc = np.sin(ang_c) * sc_c
    t = np.arange(seq)
    if grid_w is None:
        frac = (np.outer(t, t) % seq) / seq
    else:
        rows = seq // grid_w
        r, w = t // grid_w, t % grid_w
        frac = (np.outer(r, r) % rows) / rows + (np.outer(w, w) % grid_w) / grid_w
    ang_s = 2.0 * np.pi * frac
    sc_s = 1.0 / np.sqrt(seq)
    ms = np.concatenate([np.cos(ang_s) * sc_s, -np.sin(ang_s) * sc_s], axis=1)
    return (jnp.asarray(cc, F32).astype(BF16), jnp.asarray(sc, F32).astype(BF16),
            jnp.asarray(ms, F32).astype(BF16))


def _fourier(proj, tables, seq, col_block0, df):
    rows = proj.shape[0]
    cc, sc, ms = tables
    dg = cc.shape[0]
    assert col_block0 % (df // dg) == 0
    return pl.pallas_call(
        functools.partial(_fourier_kernel, seq=seq, dg=dg),
        grid=(rows // seq,),
        in_specs=[
            pl.BlockSpec((seq, df), lambda b: (b, col_block0 * dg // df)),
            _resident((dg, dg), lambda b: (0, 0)),
            _resident((dg, dg), lambda b: (0, 0)),
            _resident((seq, 2 * seq), lambda b: (0, 0)),
        ],
        out_specs=pl.BlockSpec((seq, df), lambda b: (b, 0)),
        out_shape=jax.ShapeDtypeStruct((rows, df), BF16),
        scratch_shapes=[pltpu.VMEM((2 * seq, df), BF16)],
        compiler_params=_params("parallel"),
        name="fourier",
    )(proj, cc, sc, ms)


def _merge_kernel(hg_ref, fy_ref, ga_ref, gb_ref, x_ref, mod_ref, wa_ref, wb_ref, wo_ref, lg_ref, lb_ref,
                  o_ref, *, d, alpha):
    ya = _dot(hg_ref[...], wa_ref[...])
    yb = _dot(fy_ref[...], wb_ref[...])
    merged = ga_ref[...].astype(F32) * ya + gb_ref[...].astype(F32) * yb
    out = _dot(merged.astype(BF16), wo_ref[...])
    g1 = mod_ref[...][:, 2 * d:3 * d]
    o_ref[...] = _layer_norm(alpha * x_ref[...] + g1 * out, lg_ref[...], lb_ref[...])


def _merge(hg, fy, proj, x, mod, layer, mod_row, wa, wb, wo, ln_g, ln_b, ga_block, alpha):
    rows, d = x.shape
    dm = hg.shape[1]
    df = fy.shape[1]
    tm = ROW_TILE
    wmap = lambda i: (layer, 0, 0)
    return pl.pallas_call(
        functools.partial(_merge_kernel, d=d, alpha=alpha),
        grid=(rows // tm,),
        in_specs=[
            pl.BlockSpec((tm, dm), lambda i: (i, 0)),
            pl.BlockSpec((tm, df), lambda i: (i, 0)),
            pl.BlockSpec((tm, d), lambda i: (i, ga_block)),
            pl.BlockSpec((tm, d), lambda i: (i, ga_block + 1)),
            pl.BlockSpec((tm, d), lambda i: (i, 0)),
            pl.BlockSpec((None, None, 1, 6 * d), lambda i: (layer, mod_row(i, tm), 0, 0)),
            _resident((None, dm, d), wmap),
            _resident((None, df, d), wmap),
            _resident((None, d, d), wmap),
            _resident((None, None, 1, d), lambda i: (layer, 0, 0, 0)),
            _resident((None, None, 1, d), lambda i: (layer, 0, 0, 0)),
        ],
        out_specs=pl.BlockSpec((tm, d), lambda i: (i, 0)),
        out_shape=jax.ShapeDtypeStruct((rows, d), F32),
        compiler_params=_params("parallel"),
        name="merge",
    )(hg, fy, proj, proj, x, mod, wa, wb, wo, ln_g, ln_b)


def _ffn_kernel(x_ref, mod_ref, w1_ref, w2_ref, lg_ref, lb_ref, o_ref, h_ref, g_ref, *, d, alpha):
    dff = w2_ref.shape[0]
    m = mod_ref[...]
    x = x_ref[...]
    h_ref[...] = (x * (1.0 + m[:, 4 * d:5 * d]) + m[:, 3 * d:4 * d]).astype(BF16)
    tk = FFN_COLS
    for kk in range(dff // tk):
        a = _dot(h_ref[...], w1_ref[:, kk * tk:(kk + 1) * tk])
        u = _dot(h_ref[...], w1_ref[:, dff + kk * tk:dff + (kk + 1) * tk])
        g_ref[:, kk * tk:(kk + 1) * tk] = (a * jax.nn.sigmoid(a) * u).astype(BF16)
    f = _dot(g_ref[...], w2_ref[...])
    g2 = m[:, 5 * d:6 * d]
    o_ref[...] = _layer_norm(alpha * x + g2 * f, lg_ref[...], lb_ref[...])


def _ffn(x, mod, layer, mod_row, w1, w2, ln_g, ln_b, alpha):
    rows, d = x.shape
    dff = w2.shape[1]
    tm = ROW_TILE
    assert dff % FFN_COLS == 0
    return pl.pallas_call(
        functools.partial(_ffn_kernel, d=d, alpha=alpha),
        grid=(rows // tm,),
        in_specs=[
            pl.BlockSpec((tm, d), lambda i: (i, 0)),
            pl.BlockSpec((None, None, 1, 6 * d), lambda i: (layer, mod_row(i, tm), 0, 0)),
            _resident((None, d, 2 * dff), lambda i: (layer, 0, 0)),
            _resident((None, dff, d), lambda i: (layer, 0, 0)),
            _resident((None, None, 1, d), lambda i: (layer, 1, 0, 0)),
            _resident((None, None, 1, d), lambda i: (layer, 1, 0, 0)),
        ],
        out_specs=pl.BlockSpec((tm, d), lambda i: (i, 0)),
        out_shape=jax.ShapeDtypeStruct((rows, d), F32),
        scratch_shapes=[pltpu.VMEM((tm, d), BF16), pltpu.VMEM((tm, dff), BF16)],
        compiler_params=_params("parallel"),
        name="ffn",
    )(x, mod, w1, w2, ln_g, ln_b)


def kernel(x_prompt, x_sample, c, state_C, state_n, state_m, c_ctx, w_mod, b_mod, w_in, b_gate, mh_gain,
           w_branch_a, w_branch_b, w_out, ln_gain, ln_bias, w_ffn_in, w_ffn_out):
    batch, seq_ctx, d = x_prompt.shape
    dec_batch, seq_lat, _ = x_sample.shape
    depth = w_in.shape[0]
    dm = w_branch_a.shape[1]
    df = w_branch_b.shape[1]
    dv = dm // N_HEADS
    dqk = dv // 2
    dg = df // N_FGROUPS
    n_gates = 4 * N_HEADS
    q_end = N_HEADS * dqk
    k_end = 2 * q_end
    v_end = k_end + dm
    o_end = v_end + dm
    g_end = o_end + n_gates
    alpha = float((2 * depth) ** 0.25)

    mod_rows = 16
    cvec = jnp.zeros((mod_rows, d), F32).at[0].set(c_ctx).at[1:1 + dec_batch].set(c)
    mod = _modulation(cvec, w_mod, b_mod).reshape(depth, mod_rows, 1, 6 * d)

    w_head = w_in[:, :, :o_end].astype(BF16)
    w_tail = w_in[:, :, g_end:].astype(BF16)
    gate_pad = ((0, 0), (0, 0), (0, 0), (0, GATE_STRIDE - N_HEADS))
    lane_pad = ((0, 0), (0, 0), (0, GATE_LANES - 4 * GATE_STRIDE))
    wg = jnp.pad(w_in[:, :, o_end:g_end].reshape(depth, d, 4, N_HEADS), gate_pad)
    wg = jnp.pad(wg.reshape(depth, d, 4 * GATE_STRIDE), lane_pad).astype(BF16)
    bg = jnp.pad(b_gate.reshape(depth, 1, 4, N_HEADS), gate_pad)
    bg = jnp.pad(bg.reshape(depth, 1, 4 * GATE_STRIDE), lane_pad)
    wa = w_branch_a.astype(BF16)
    wb = w_branch_b.astype(BF16)
    wo = w_out.astype(BF16)
    w1 = w_ffn_in.astype(BF16)
    w2 = w_ffn_out.astype(BF16)
    gain = mh_gain.reshape(depth, N_HEADS, 1, dv)
    ln_g = ln_gain.reshape(depth, 2, 1, d)
    ln_b = ln_bias.reshape(depth, 2, 1, d)

    def slab_fn(col):
        if q_end <= col < k_end:
            return lambda t: t * (dqk ** -0.5)
        if v_end <= col < o_end or col >= o_end + df:
            return jax.nn.sigmoid
        return lambda t: t
    n_main = o_end + w_tail.shape[2]
    assert all(e % PROJ_COLS == 0 for e in (q_end, k_end, v_end, o_end, o_end + df, n_main))
    epilogues = tuple(slab_fn(j * PROJ_COLS) for j in range(n_main // PROJ_COLS))

    tables_ctx = _dft_tables(seq_ctx, dg, None)
    tables_lat = _dft_tables(seq_lat, dg, GRID_W)
    f_block0 = o_end // dg
    ga_block = (o_end + df) // d

    def ctx_row(i, tm):
        return 0

    def lat_row(i, tm):
        return 1 + (i * tm) // seq_lat

    def layer(x, l, seq, mod_row, tables, init, state_c):
        proj, gcol, grow = _inproj(x, mod, l, mod_row, w_head, w_tail, wg, bg, epilogues)
        res = _mlstm(proj, gcol, grow, gain, seq, l, depth, init, state_c)
        fy = _fourier(proj, tables, seq, f_block0, df)
        x = _merge(res[0], fy, proj, x, mod, l, mod_row, wa, wb, wo, ln_g, ln_b, ga_block, alpha)
        x = _ffn(x, mod, l, mod_row, w1, w2, ln_g, ln_b, alpha)
        return x, res[1:]

    xp = x_prompt.reshape(batch * seq_ctx, d)
    xs = x_sample.reshape(dec_batch * seq_lat, d)
    new_c, new_n, new_m = None, [], []
    for l in range(depth):
        xp, (new_c, nfin, mfin) = layer(xp, l, seq_ctx, ctx_row, tables_ctx, None, new_c)
        new_n.append(nfin.transpose(0, 2, 1, 3))
        new_m.append(mfin[..., 0].transpose(0, 2, 1))
        xs, _ = layer(xs, l, seq_lat, lat_row, tables_lat, (state_C, state_n, state_m), None)
    return (xp.reshape(batch, seq_ctx, d), xs.reshape(dec_batch, seq_lat, d),
            new_c, jnp.stack(new_n, axis=1), jnp.stack(new_m, axis=1))
```

```python
import functools
import math

import numpy as np
import jax
import jax.numpy as jnp
from jax import lax
from jax.experimental import pallas as pl
from jax.experimental.pallas import tpu as pltpu

F32 = jnp.float32
BF16 = jnp.bfloat16

N_HEADS = 4
N_FGROUPS = 4
GRID_W = 64
LN_EPS = 1e-5
MLSTM_CHUNK = 256
GATE_LANES = 128
GATE_STRIDE = 8
ROW_TILE = 512
PROJ_COLS = 1024
FFN_COLS = 256
LAT_HEADS_PER_STEP = 2
VMEM_LIMIT = 56 * 1024 * 1024
LOG2E = math.log2(math.e)

NT_DIMS = (((1,), (1,)), ((), ()))
TN_DIMS = (((0,), (0,)), ((), ()))


def _dot(a, b):
    return jnp.dot(a, b, preferred_element_type=F32)


def _dot_nt(a, b):
    return lax.dot_general(a, b, NT_DIMS, preferred_element_type=F32)


def _dot_tn(a, b):
    return lax.dot_general(a, b, TN_DIMS, preferred_element_type=F32)


def _split3(x):
    hi = x.astype(BF16)
    r = x - hi.astype(F32)
    mid = r.astype(BF16)
    lo = (r - mid.astype(F32)).astype(BF16)
    return hi, mid, lo


def _layer_norm(y, g, b):
    mu = jnp.mean(y, axis=-1, keepdims=True)
    yc = y - mu
    var = jnp.mean(yc * yc, axis=-1, keepdims=True)
    return yc * lax.rsqrt(var + LN_EPS) * g + b


def _log_sigmoid(x):
    return jnp.minimum(x, 0.0) - jnp.log(1.0 + jnp.exp(-jnp.abs(x)))


def _params(*sem):
    return pltpu.CompilerParams(dimension_semantics=sem, vmem_limit_bytes=VMEM_LIMIT)


def _resident(shape, index_map):
    return pl.BlockSpec(shape, index_map, pipeline_mode=pl.Buffered(1))


def _mod_kernel(c_ref, w_ref, b_ref, o_ref):
    c = c_ref[...]
    s = (c * jax.nn.sigmoid(c)).astype(BF16)
    o_ref[...] = _dot(s, w_ref[...].astype(BF16)) + b_ref[...]


def _modulation(cvec, w_mod, b_mod):
    depth, d, n6 = w_mod.shape
    rows = cvec.shape[0]
    tn = 1536
    return pl.pallas_call(
        _mod_kernel,
        grid=(depth, n6 // tn),
        in_specs=[
            pl.BlockSpec((rows, d), lambda l, j: (0, 0)),
            pl.BlockSpec((None, d, tn), lambda l, j: (l, 0, j)),
            pl.BlockSpec((None, 1, tn), lambda l, j: (l, 0, j)),
        ],
        out_specs=pl.BlockSpec((None, rows, tn), lambda l, j: (l, 0, j)),
        out_shape=jax.ShapeDtypeStruct((depth, rows, n6), F32),
        compiler_params=_params("parallel", "parallel"),
        name="modulation",
    )(cvec, w_mod, b_mod.reshape(depth, 1, n6))


def _cast_kernel(w_ref, o_ref):
    o_ref[...] = w_ref[...].astype(BF16)


def _shift_cast_kernel(a_ref, b_ref, o_ref, *, shift):
    cat = jnp.concatenate([a_ref[...], b_ref[...]], axis=1)
    o_ref[...] = cat[:, shift:shift + o_ref.shape[1]].astype(BF16)


def _split_projection_weights(w_in, o_end, g_end):
    depth, d, n_in = w_in.shape
    tn = PROJ_COLS
    shift = g_end - o_end
    n_tail = n_in - g_end
    assert o_end % tn == 0 and n_tail % tn == 0 and 0 < shift <= GATE_LANES
    head = pl.pallas_call(
        _cast_kernel,
        grid=(depth, o_end // tn),
        in_specs=[pl.BlockSpec((None, d, tn), lambda l, j: (l, 0, j))],
        out_specs=pl.BlockSpec((None, d, tn), lambda l, j: (l, 0, j)),
        out_shape=jax.ShapeDtypeStruct((depth, d, o_end), BF16),
        compiler_params=_params("parallel", "parallel"),
        name="w_head_cast",
    )(w_in)
    base = o_end // tn
    tiles = tn // GATE_LANES
    tail = pl.pallas_call(
        functools.partial(_shift_cast_kernel, shift=shift),
        grid=(depth, n_tail // tn),
        in_specs=[
            pl.BlockSpec((None, d, tn), lambda l, j: (l, 0, base + j)),
            pl.BlockSpec((None, d, GATE_LANES), lambda l, j: (l, 0, (base + j + 1) * tiles)),
        ],
        out_specs=pl.BlockSpec((None, d, tn), lambda l, j: (l, 0, j)),
        out_shape=jax.ShapeDtypeStruct((depth, d, n_tail), BF16),
        compiler_params=_params("parallel", "parallel"),
        name="w_tail_cast",
    )(w_in, w_in)
    return head, tail


def _inproj_kernel(x_ref, mod_ref, wh_ref, wt_ref, wg_ref, bg_ref, proj_ref, gcol_ref, grow_ref, h_ref,
                   *, d, epilogues):
    tm = x_ref.shape[0]
    L = MLSTM_CHUNK
    G = GATE_STRIDE
    m = mod_ref[...]
    hb = (x_ref[...] * (1.0 + m[:, d:2 * d]) + m[:, 0:d]).astype(BF16)
    h_ref[...] = hb

    g = _dot(hb, wg_ref[...]) + bg_ref[...]
    lane = lax.broadcasted_iota(jnp.int32, g.shape, 1)
    g = jnp.where(((lane >> 3) & 1) == 1, _log_sigmoid(g), g)
    n_rows = 4 * G
    gt = g.T[0:n_rows, :]
    r_i = lax.broadcasted_iota(jnp.int32, (L, L), 0)
    c_i = lax.broadcasted_iota(jnp.int32, (L, L), 1)
    upper_b = (r_i <= c_i).astype(BF16)
    lower_b = (r_i >= c_i).astype(BF16)

    def cumsum(x, tri):
        hi, mid, lo = _split3(x)
        return _dot(hi, tri) + _dot(mid, tri) + _dot(lo, tri)

    pieces = []
    for c in range(tm // L):
        xg = gt[:, c * L:(c + 1) * L]
        pre = cumsum(xg[G:2 * G], upper_b)
        suf = cumsum(xg[3 * G:4 * G], lower_b)
        pieces.append(jnp.concatenate([xg[0:G] - pre, pre, xg[2 * G:3 * G] - suf, suf], axis=0))
    gs = jnp.concatenate(pieces, axis=1)
    for qq in range(4):
        grow_ref[qq] = gs[qq * G:(qq + 1) * G, :]
    gcol_ref[...] = jnp.concatenate([gs, jnp.zeros((GATE_LANES - n_rows, tm), F32)], axis=0).T

    tn = PROJ_COLS
    n_head = wh_ref.shape[1] // tn
    for jn, fn in enumerate(epilogues):
        if jn < n_head:
            w = wh_ref[:, jn * tn:(jn + 1) * tn]
        else:
            w = wt_ref[:, (jn - n_head) * tn:(jn - n_head + 1) * tn]
        proj_ref[:, jn * tn:(jn + 1) * tn] = fn(_dot(h_ref[...], w)).astype(BF16)


def _inproj(x, mod, layer, mod_row, w_head, w_tail, w_gate, b_gate, epilogues):
    rows, d = x.shape
    n_h, n_t = w_head.shape[2], w_tail.shape[2]
    n = n_h + n_t
    tm = ROW_TILE
    assert n == PROJ_COLS * len(epilogues) and n_h % PROJ_COLS == 0 and rows % tm == 0 and tm % MLSTM_CHUNK == 0
    return pl.pallas_call(
        functools.partial(_inproj_kernel, d=d, epilogues=epilogues),
        grid=(rows // tm,),
        in_specs=[
            pl.BlockSpec((tm, d), lambda i: (i, 0)),
            pl.BlockSpec((None, None, 1, 6 * d), lambda i: (layer, mod_row(i, tm), 0, 0)),
            _resident((None, d, n_h), lambda i: (layer, 0, 0)),
            _resident((None, d, n_t), lambda i: (layer, 0, 0)),
            _resident((None, d, GATE_LANES), lambda i: (layer, 0, 0)),
            _resident((None, 1, GATE_LANES), lambda i: (layer, 0, 0)),
        ],
        out_specs=[
            pl.BlockSpec((tm, n), lambda i: (i, 0)),
            pl.BlockSpec((tm, GATE_LANES), lambda i: (i, 0)),
            pl.BlockSpec((4, GATE_STRIDE, tm), lambda i: (0, 0, i)),
        ],
        out_shape=[
            jax.ShapeDtypeStruct((rows, n), BF16),
            jax.ShapeDtypeStruct((rows, GATE_LANES), F32),
            jax.ShapeDtypeStruct((4, GATE_STRIDE, rows), F32),
        ],
        scratch_shapes=[pltpu.VMEM((tm, d), BF16)],
        compiler_params=_params("parallel"),
        name="inproj",
    )(x, mod, w_head, w_tail, w_gate, b_gate)


def _mlstm_kernel(*refs, heads, **static):
    for hh in range(heads):
        _mlstm_head(hh, *refs, heads=heads, **static)


def _mlstm_head(hh, *refs, seq, has_init, layer, alias_state, heads):
    if has_init:
        (q_ref, k_ref, v_ref, og_ref, gcol_ref, grow_ref, gain_ref, c0_ref, n0_ref, m0_ref,
         out_ref, cc_ref) = refs
    elif alias_state:
        (q_ref, k_ref, v_ref, og_ref, gcol_ref, grow_ref, gain_ref, _,
         out_ref, cfin_ref, nfin_ref, mfin_ref) = refs
    else:
        (q_ref, k_ref, v_ref, og_ref, gcol_ref, grow_ref, gain_ref,
         out_ref, cfin_ref, nfin_ref, mfin_ref) = refs
    L = MLSTM_CHUNK
    G = GATE_STRIDE
    nc = seq // L
    b_idx = pl.program_id(0)
    h_idx = pl.program_id(1) * heads + hh
    dqk = q_ref.shape[1] // heads
    dv = v_ref.shape[1] // heads
    qc = slice(hh * dqk, (hh + 1) * dqk)
    vc = slice(hh * dv, (hh + 1) * dv)

    s_i = lax.broadcasted_iota(jnp.int32, (L, L), 0)
    t_i = lax.broadcasted_iota(jnp.int32, (L, L), 1)
    masks = (s_i <= t_i, s_i >= t_i)
    eye_b = (s_i == t_i).astype(BF16)
    ones_b = jnp.ones((G, L), BF16)
    lane = lax.broadcasted_iota(jnp.int32, (L, GATE_LANES), 1)

    def gate_rows(c):
        sl = pl.ds(c * L, L)
        hs = pl.ds(h_idx, 1)
        b_f = grow_ref[1, hs, sl]
        b_b = grow_ref[3, hs, sl]
        return ((grow_ref[0, hs, sl], b_f, b_f[:, L - 1:L]), (grow_ref[2, hs, sl], b_b, b_b[:, 0:1]))

    def e_column(c, d):
        gc = gcol_ref[pl.ds(c * L, L), :]
        return jnp.sum(jnp.where(lane == 2 * d * G + h_idx, gc, 0.0), axis=1, keepdims=True)

    def to_columns(rows):
        rep = jnp.concatenate([jnp.broadcast_to(r, (GATE_LANES, L)) for r in rows], axis=0)
        return _dot_nt(eye_b, rep.astype(BF16))

    def wide(col_tile):
        return jnp.concatenate([col_tile] * (dqk // GATE_LANES), axis=1)

    rows = [gate_rows(c) for c in range(nc)]

    def local_update(c):
        e_max, w_rows = [], []
        for d in range(2):
            e_row = rows[c][d][0]
            mx = jnp.max(e_row, axis=1, keepdims=True)
            e_max.append(mx)
            w_rows.append(jnp.exp(e_row - mx))
        w_cols = to_columns(w_rows).astype(BF16)
        k = k_ref[pl.ds(c * L, L), qc]
        kws = [k * wide(w_cols[:, d * GATE_LANES:(d + 1) * GATE_LANES]) for d in range(2)]
        return e_max, kws

    m_pre = [[None] * nc for _ in range(2)]
    n_pre = [[None] * nc for _ in range(2)]
    if has_init:
        local = {c: local_update(c) for c in range(nc)}
        for d in range(2):
            order = list(range(nc)) if d == 0 else list(range(nc - 1, -1, -1))
            m = jnp.full((1, 1), m0_ref[b_idx, layer, d, h_idx], F32)
            n = n0_ref[d, pl.ds(h_idx, 1), :]
            cm = c0_ref[d, hh]
            for pos, c in enumerate(order):
                m_pre[d][c] = m
                n_pre[d][c] = n
                cc_ref[hh, c, :, d * dqk:(d + 1) * dqk] = cm.astype(BF16)
                if pos == nc - 1:
                    break
                e_max, kws = local[c]
                b_tot = rows[c][d][2]
                g_max = b_tot + e_max[d]
                carried = b_tot + m
                m = jnp.maximum(carried, g_max)
                decay = jnp.exp(carried - m)
                up = jnp.exp(g_max - m)
                v = v_ref[pl.ds(c * L, L), vc]
                cm = decay * cm + up * _dot_tn(v, kws[d])
                n = decay * n + up * _dot(ones_b, kws[d])[0:1, :]
    else:
        e_max, kws = local_update(0)
        for d in range(2):
            b_tot = rows[0][d][2]
            g_max = b_tot + e_max[d]
            m_pre[d][0] = jnp.zeros((1, 1), F32)
            m = jnp.maximum(b_tot, g_max)
            up = jnp.exp(g_max - m)
            c_new = up * _dot_tn(v_ref[:, vc], kws[d])
            if alias_state:
                cfin_ref[d, hh] = c_new
            else:
                for lyr in range(cfin_ref.shape[0]):
                    cfin_ref[lyr, d, hh] = c_new if lyr == layer else jnp.zeros_like(c_new)
            nfin_ref[hh, pl.ds(d, 1), :] = up * _dot(ones_b, kws[d])[0:1, :]
            mfin_ref[hh, pl.ds(d, 1), :] = jnp.broadcast_to(m, (1, GATE_LANES))

    gain = gain_ref[hh]
    for c in range(nc):
        q = q_ref[pl.ds(c * L, L), qc]
        k = k_ref[pl.ds(c * L, L), qc]
        v = v_ref[pl.ds(c * L, L), vc]
        qk_t = _dot_nt(k, q)
        if has_init:
            n_rows = jnp.concatenate([n_pre[0][c], n_pre[1][c], jnp.zeros((G - 2, dqk), F32)], axis=0)
            qn_t = _dot_nt(n_rows.astype(BF16), q)
        p_t = None
        scale_rows = []
        for d in range(2):
            _, b_row, _ = rows[c][d]
            em = jnp.where(masks[d], e_column(c, d) * LOG2E, -jnp.inf)
            b2 = b_row * LOG2E
            a2 = b2 + m_pre[d][c] * LOG2E
            mt2 = jnp.maximum(a2, b2 + jnp.max(em, axis=0, keepdims=True))
            s_t = qk_t * jnp.exp2((b2 - mt2) + em)
            den = jnp.sum(s_t, axis=0, keepdims=True)
            if has_init:
                inter = jnp.exp2(a2 - mt2)
                den = den + inter * qn_t[d:d + 1, :]
            r = 1.0 / jnp.maximum(jnp.abs(den), jnp.exp2(-mt2))
            p_t = s_t * r if p_t is None else p_t + s_t * r
            if has_init:
                scale_rows.append(inter * r)
        h = _dot_tn(p_t.astype(BF16), v)
        if has_init:
            sc = to_columns(scale_rows).astype(BF16)
            qs = jnp.concatenate([q * wide(sc[:, d * GATE_LANES:(d + 1) * GATE_LANES]) for d in range(2)], axis=1)
            h = h + _dot_nt(qs, cc_ref[hh, c])
        mu = jnp.mean(h, axis=-1, keepdims=True)
        hc = h - mu
        var = jnp.mean(hc * hc, axis=-1, keepdims=True)
        hn = hc * lax.rsqrt(var + LN_EPS) * gain
        out_ref[pl.ds(c * L, L), vc] = hn.astype(BF16) * og_ref[pl.ds(c * L, L), vc]


def _mlstm(proj, gcol, grow, gain, seq, layer, depth, init=None, state_c=None):
    rows = proj.shape[0]
    n_seq = rows // seq
    dv = gain.shape[3]
    dm = N_HEADS * dv
    dqk = dv // 2
    has_init = init is not None
    hps = LAT_HEADS_PER_STEP if has_init else N_HEADS
    hb = N_HEADS // hps
    vb = (2 * N_HEADS * dqk) // (hps * dv)
    ob = vb + hb
    assert seq % MLSTM_CHUNK == 0 and (has_init or seq == MLSTM_CHUNK) and dqk % GATE_LANES == 0
    in_specs = [
        pl.BlockSpec((seq, hps * dqk), lambda b, h: (b, h)),
        pl.BlockSpec((seq, hps * dqk), lambda b, h: (b, hb + h)),
        pl.BlockSpec((seq, hps * dv), lambda b, h: (b, vb + h)),
        pl.BlockSpec((seq, hps * dv), lambda b, h: (b, ob + h)),
        pl.BlockSpec((seq, GATE_LANES), lambda b, h: (b, 0)),
        pl.BlockSpec((4, GATE_STRIDE, seq), lambda b, h: (0, 0, b)),
        pl.BlockSpec((None, hps, 1, dv), lambda b, h: (layer, h, 0, 0)),
    ]
    args = [proj, proj, proj, proj, gcol, grow, gain]
    out_specs = [pl.BlockSpec((seq, hps * dv), lambda b, h: (b, h))]
    out_shape = [jax.ShapeDtypeStruct((rows, dm), BF16)]
    scratch = []
    aliases = {}
    if has_init:
        init_c, init_n, init_m = init
        in_specs += [
            pl.BlockSpec((None, None, 2, hps, dv, dqk), lambda b, h: (b, layer, 0, h, 0, 0)),
            pl.BlockSpec((None, None, 2, N_HEADS, dqk), lambda b, h: (b, layer, 0, 0, 0)),
            pl.BlockSpec(memory_space=pltpu.MemorySpace.SMEM),
        ]
        args += [init_c, init_n, init_m]
        scratch = [pltpu.VMEM((hps, seq // MLSTM_CHUNK, dv, 2 * dqk), BF16)]
    else:
        if state_c is not None:
            in_specs.append(pl.BlockSpec(memory_space=pl.ANY))
            args.append(state_c)
            aliases = {len(args) - 1: 1}
            state_spec = pl.BlockSpec((None, None, 2, hps, dv, dqk), lambda b, h: (b, layer, 0, h, 0, 0))
        else:
            state_spec = pl.BlockSpec((None, depth, 2, hps, dv, dqk), lambda b, h: (b, 0, 0, h, 0, 0))
        out_specs += [
            state_spec,
            pl.BlockSpec((None, hps, 2, dqk), lambda b, h: (b, h, 0, 0)),
            pl.BlockSpec((None, hps, 2, GATE_LANES), lambda b, h: (b, h, 0, 0)),
        ]
        out_shape += [
            jax.ShapeDtypeStruct((n_seq, depth, 2, N_HEADS, dv, dqk), F32),
            jax.ShapeDtypeStruct((n_seq, N_HEADS, 2, dqk), F32),
            jax.ShapeDtypeStruct((n_seq, N_HEADS, 2, GATE_LANES), F32),
        ]
    return pl.pallas_call(
        functools.partial(_mlstm_kernel, seq=seq, has_init=has_init, layer=layer,
                          alias_state=state_c is not None, heads=hps),
        grid=(n_seq, N_HEADS // hps),
        in_specs=in_specs,
        out_specs=out_specs,
        out_shape=out_shape,
        scratch_shapes=scratch,
        input_output_aliases=aliases,
        compiler_params=_params("parallel", "parallel"),
        name="mlstm_lat" if has_init else "mlstm_ctx",
    )(*args)


def _fourier_kernel(u_ref, cc_ref, sc_ref, ms_ref, o_ref, ab_ref, *, seq, dg):
    for g in range(u_ref.shape[1] // dg):
        u = u_ref[:, g * dg:(g + 1) * dg]
        ab_ref[0:seq, g * dg:(g + 1) * dg] = _dot(u, cc_ref[...]).astype(BF16)
        ab_ref[seq:2 * seq, g * dg:(g + 1) * dg] = _dot(u, sc_ref[...]).astype(BF16)
    o_ref[...] = _dot(ms_ref[...], ab_ref[...]).astype(BF16)


def _dft_tables(seq, dg, grid_w):
    ch = np.arange(dg)
    ang_c = 2.0 * np.pi * ((np.outer(ch, ch) % dg) / dg)
    sc_c = 1.0 / np.sqrt(dg)
    cc = np.cos(ang_c) * sc_c
    sc = np.sin(ang_c) * sc_c
    t = np.arange(seq)
    if grid_w is None:
        frac = (np.outer(t, t) % seq) / seq
    else:
        rows = seq // grid_w
        r, w = t // grid_w, t % grid_w
        frac = (np.outer(r, r) % rows) / rows + (np.outer(w, w) % grid_w) / grid_w
    ang_s = 2.0 * np.pi * frac
    sc_s = 1.0 / np.sqrt(seq)
    ms = np.concatenate([np.cos(ang_s) * sc_s, -np.sin(ang_s) * sc_s], axis=1)
    return (jnp.asarray(cc, F32).astype(BF16), jnp.asarray(sc, F32).astype(BF16),
            jnp.asarray(ms, F32).astype(BF16))


def _fourier(proj, tables, seq, col_block0, df):
    rows = proj.shape[0]
    cc, sc, ms = tables
    dg = cc.shape[0]
    assert col_block0 % (df // dg) == 0
    return pl.pallas_call(
        functools.partial(_fourier_kernel, seq=seq, dg=dg),
        grid=(rows // seq,),
        in_specs=[
            pl.BlockSpec((seq, df), lambda b: (b, col_block0 * dg // df)),
            _resident((dg, dg), lambda b: (0, 0)),
            _resident((dg, dg), lambda b: (0, 0)),
            _resident((seq, 2 * seq), lambda b: (0, 0)),
        ],
        out_specs=pl.BlockSpec((seq, df), lambda b: (b, 0)),
        out_shape=jax.ShapeDtypeStruct((rows, df), BF16),
        scratch_shapes=[pltpu.VMEM((2 * seq, df), BF16)],
        compiler_params=_params("parallel"),
        name="fourier",
    )(proj, cc, sc, ms)


def _merge_kernel(hg_ref, fy_ref, ga_ref, gb_ref, x_ref, mod_ref, wa_ref, wb_ref, wo_ref, lg_ref, lb_ref,
                  o_ref, *, d, alpha):
    ya = _dot(hg_ref[...], wa_ref[...])
    yb = _dot(fy_ref[...], wb_ref[...])
    merged = ga_ref[...].astype(F32) * ya + gb_ref[...].astype(F32) * yb
    out = _dot(merged.astype(BF16), wo_ref[...])
    g1 = mod_ref[...][:, 2 * d:3 * d]
    o_ref[...] = _layer_norm(alpha * x_ref[...] + g1 * out, lg_ref[...], lb_ref[...])


def _merge(hg, fy, proj, x, mod, layer, mod_row, wa, wb, wo, ln_g, ln_b, ga_block, alpha):
    rows, d = x.shape
    dm = hg.shape[1]
    df = fy.shape[1]
    tm = ROW_TILE
    wmap = lambda i: (layer, 0, 0)
    return pl.pallas_call(
        functools.partial(_merge_kernel, d=d, alpha=alpha),
        grid=(rows // tm,),
        in_specs=[
            pl.BlockSpec((tm, dm), lambda i: (i, 0)),
            pl.BlockSpec((tm, df), lambda i: (i, 0)),
            pl.BlockSpec((tm, d), lambda i: (i, ga_block)),
            pl.BlockSpec((tm, d), lambda i: (i, ga_block + 1)),
            pl.BlockSpec((tm, d), lambda i: (i, 0)),
            pl.BlockSpec((None, None, 1, 6 * d), lambda i: (layer, mod_row(i, tm), 0, 0)),
            _resident((None, dm, d), wmap),
            _resident((None, df, d), wmap),
            _resident((None, d, d), wmap),
            _resident((None, None, 1, d), lambda i: (layer, 0, 0, 0)),
            _resident((None, None, 1, d), lambda i: (layer, 0, 0, 0)),
        ],
        out_specs=pl.BlockSpec((tm, d), lambda i: (i, 0)),
        out_shape=jax.ShapeDtypeStruct((rows, d), F32),
        compiler_params=_params("parallel"),
        name="merge",
    )(hg, fy, proj, proj, x, mod, wa, wb, wo, ln_g, ln_b)


def _ffn_kernel(x_ref, mod_ref, w1_ref, w2_ref, lg_ref, lb_ref, o_ref, h_ref, g_ref, *, d, alpha):
    dff = w2_ref.shape[0]
    m = mod_ref[...]
    x = x_ref[...]
    h_ref[...] = (x * (1.0 + m[:, 4 * d:5 * d]) + m[:, 3 * d:4 * d]).astype(BF16)
    tk = FFN_COLS
    for kk in range(dff // tk):
        a = _dot(h_ref[...], w1_ref[:, kk * tk:(kk + 1) * tk])
        u = _dot(h_ref[...], w1_ref[:, dff + kk * tk:dff + (kk + 1) * tk])
        g_ref[:, kk * tk:(kk + 1) * tk] = (a * jax.nn.sigmoid(a) * u).astype(BF16)
    f = _dot(g_ref[...], w2_ref[...])
    g2 = m[:, 5 * d:6 * d]
    o_ref[...] = _layer_norm(alpha * x + g2 * f, lg_ref[...], lb_ref[...])


def _ffn(x, mod, layer, mod_row, w1, w2, ln_g, ln_b, alpha):
    rows, d = x.shape
    dff = w2.shape[1]
    tm = ROW_TILE
    assert dff % FFN_COLS == 0
    return pl.pallas_call(
        functools.partial(_ffn_kernel, d=d, alpha=alpha),
        grid=(rows // tm,),
        in_specs=[
            pl.BlockSpec((tm, d), lambda i: (i, 0)),
            pl.BlockSpec((None, None, 1, 6 * d), lambda i: (layer, mod_row(i, tm), 0, 0)),
            _resident((None, d, 2 * dff), lambda i: (layer, 0, 0)),
            _resident((None, dff, d), lambda i: (layer, 0, 0)),
            _resident((None, None, 1, d), lambda i: (layer, 1, 0, 0)),
            _resident((None, None, 1, d), lambda i: (layer, 1, 0, 0)),
        ],
        out_specs=pl.BlockSpec((tm, d), lambda i: (i, 0)),
        out_shape=jax.ShapeDtypeStruct((rows, d), F32),
        scratch_shapes=[pltpu.VMEM((tm, d), BF16), pltpu.VMEM((tm, dff), BF16)],
        compiler_params=_params("parallel"),
        name="ffn",
    )(x, mod, w1, w2, ln_g, ln_b)


def kernel(x_prompt, x_sample, c, state_C, state_n, state_m, c_ctx, w_mod, b_mod, w_in, b_gate, mh_gain,
           w_branch_a, w_branch_b, w_out, ln_gain, ln_bias, w_ffn_in, w_ffn_out):
    batch, seq_ctx, d = x_prompt.shape
    dec_batch, seq_lat, _ = x_sample.shape
    depth = w_in.shape[0]
    dm = w_branch_a.shape[1]
    df = w_branch_b.shape[1]
    dv = dm // N_HEADS
    dqk = dv // 2
    dg = df // N_FGROUPS
    n_gates = 4 * N_HEADS
    q_end = N_HEADS * dqk
    k_end = 2 * q_end
    v_end = k_end + dm
    o_end = v_end + dm
    g_end = o_end + n_gates
    alpha = float((2 * depth) ** 0.25)

    mod_rows = 16
    cvec = jnp.zeros((mod_rows, d), F32).at[0].set(c_ctx).at[1:1 + dec_batch].set(c)
    mod = _modulation(cvec, w_mod, b_mod).reshape(depth, mod_rows, 1, 6 * d)

    w_head, w_tail = _split_projection_weights(w_in, o_end, g_end)
    gate_pad = ((0, 0), (0, 0), (0, 0), (0, GATE_STRIDE - N_HEADS))
    lane_pad = ((0, 0), (0, 0), (0, GATE_LANES - 4 * GATE_STRIDE))
    wg = jnp.pad(w_in[:, :, o_end:g_end].reshape(depth, d, 4, N_HEADS), gate_pad)
    wg = jnp.pad(wg.reshape(depth, d, 4 * GATE_STRIDE), lane_pad).astype(BF16)
    bg = jnp.pad(b_gate.reshape(depth, 1, 4, N_HEADS), gate_pad)
    bg = jnp.pad(bg.reshape(depth, 1, 4 * GATE_STRIDE), lane_pad)
    wa = w_branch_a.astype(BF16)
    wb = w_branch_b.astype(BF16)
    wo = w_out.astype(BF16)
    w1 = w_ffn_in.astype(BF16)
    w2 = w_ffn_out.astype(BF16)
    gain = mh_gain.reshape(depth, N_HEADS, 1, dv)
    ln_g = ln_gain.reshape(depth, 2, 1, d)
    ln_b = ln_bias.reshape(depth, 2, 1, d)

    def slab_fn(col):
        if q_end <= col < k_end:
            return lambda t: t * (dqk ** -0.5)
        if v_end <= col < o_end or col >= o_end + df:
            return jax.nn.sigmoid
        return lambda t: t
    n_main = o_end + w_tail.shape[2]
    assert all(e % PROJ_COLS == 0 for e in (q_end, k_end, v_end, o_end, o_end + df, n_main))
    epilogues = tuple(slab_fn(j * PROJ_COLS) for j in range(n_main // PROJ_COLS))

    tables_ctx = _dft_tables(seq_ctx, dg, None)
    tables_lat = _dft_tables(seq_lat, dg, GRID_W)
    f_block0 = o_end // dg
    ga_block = (o_end + df) // d

    def ctx_row(i, tm):
        return 0

    def lat_row(i, tm):
        return 1 + (i * tm) // seq_lat

    def layer(x, l, seq, mod_row, tables, init, state_c):
        proj, gcol, grow = _inproj(x, mod, l, mod_row, w_head, w_tail, wg, bg, epilogues)
        res = _mlstm(proj, gcol, grow, gain, seq, l, depth, init, state_c)
        fy = _fourier(proj, tables, seq, f_block0, df)
        x = _merge(res[0], fy, proj, x, mod, l, mod_row, wa, wb, wo, ln_g, ln_b, ga_block, alpha)
        x = _ffn(x, mod, l, mod_row, w1, w2, ln_g, ln_b, alpha)
        return x, res[1:]

    xp = x_prompt.reshape(batch * seq_ctx, d)
    xs = x_sample.reshape(dec_batch * seq_lat, d)
    new_c, new_n, new_m = None, [], []
    for l in range(depth):
        xp, (new_c, nfin, mfin) = layer(xp, l, seq_ctx, ctx_row, tables_ctx, None, new_c)
        new_n.append(nfin.transpose(0, 2, 1, 3))
        new_m.append(mfin[..., 0].transpose(0, 2, 1))
        xs, _ = layer(xs, l, seq_lat, lat_row, tables_lat, (state_C, state_n, state_m), None)
    return (xp.reshape(batch, seq_ctx, d), xs.reshape(dec_batch, seq_lat, d),
            new_c, jnp.stack(new_n, axis=1), jnp.stack(new_m, axis=1))
```

```python
import functools
import math

import numpy as np
import jax
import jax.numpy as jnp
from jax import lax
from jax.experimental import pallas as pl
from jax.experimental.pallas import tpu as pltpu

F32 = jnp.float32
BF16 = jnp.bfloat16

N_HEADS = 4
N_FGROUPS = 4
GRID_W = 64
LN_EPS = 1e-5
MLSTM_CHUNK = 256
GATE_LANES = 128
GATE_STRIDE = 8
ROW_TILE = 512
PROJ_COLS = 1024
FFN_COLS = 256
LAT_HEADS_PER_STEP = 2
VMEM_LIMIT = 56 * 1024 * 1024
LOG2E = math.log2(math.e)

NT_DIMS = (((1,), (1,)), ((), ()))
TN_DIMS = (((0,), (0,)), ((), ()))


def _dot(a, b):
    return jnp.dot(a, b, preferred_element_type=F32)


def _dot_nt(a, b):
    return lax.dot_general(a, b, NT_DIMS, preferred_element_type=F32)


def _dot_tn(a, b):
    return lax.dot_general(a, b, TN_DIMS, preferred_element_type=F32)


def _split3(x):
    hi = x.astype(BF16)
    r = x - hi.astype(F32)
    mid = r.astype(BF16)
    lo = (r - mid.astype(F32)).astype(BF16)
    return hi, mid, lo


def _layer_norm(y, g, b):
    mu = jnp.mean(y, axis=-1, keepdims=True)
    yc = y - mu
    var = jnp.mean(yc * yc, axis=-1, keepdims=True)
    return yc * lax.rsqrt(var + LN_EPS) * g + b


def _log_sigmoid(x):
    return jnp.minimum(x, 0.0) - jnp.log(1.0 + jnp.exp(-jnp.abs(x)))


def _params(*sem):
    return pltpu.CompilerParams(dimension_semantics=sem, vmem_limit_bytes=VMEM_LIMIT)


def _resident(shape, index_map):
    return pl.BlockSpec(shape, index_map, pipeline_mode=pl.Buffered(1))


def _mod_kernel(c_ref, w_ref, b_ref, o_ref):
    c = c_ref[...]
    s = (c * jax.nn.sigmoid(c)).astype(BF16)
    o_ref[...] = _dot(s, w_ref[...].astype(BF16)) + b_ref[...]


def _modulation(cvec, w_mod, b_mod):
    depth, d, n6 = w_mod.shape
    rows = cvec.shape[0]
    tn = 1536
    return pl.pallas_call(
        _mod_kernel,
        grid=(depth, n6 // tn),
        in_specs=[
            pl.BlockSpec((rows, d), lambda l, j: (0, 0)),
            pl.BlockSpec((None, d, tn), lambda l, j: (l, 0, j)),
            pl.BlockSpec((None, 1, tn), lambda l, j: (l, 0, j)),
        ],
        out_specs=pl.BlockSpec((None, rows, tn), lambda l, j: (l, 0, j)),
        out_shape=jax.ShapeDtypeStruct((depth, rows, n6), F32),
        compiler_params=_params("parallel", "parallel"),
        name="modulation",
    )(cvec, w_mod, b_mod.reshape(depth, 1, n6))


def _cast_kernel(w_ref, o_ref):
    o_ref[...] = w_ref[...].astype(BF16)


def _shift_cast_kernel(a_ref, b_ref, o_ref, *, shift):
    o_ref[...] = jnp.concatenate([a_ref[shift:, :], b_ref[...]], axis=0).astype(BF16)


def _split_projection_weights(w_in_t, o_end, g_end):
    depth, n_in, d = w_in_t.shape
    tn = PROJ_COLS
    shift = g_end - o_end
    n_tail = n_in - g_end
    assert o_end % tn == 0 and n_tail % tn == 0 and shift % 8 == 0 and tn % shift == 0
    head = pl.pallas_call(
        _cast_kernel,
        grid=(depth, o_end // tn),
        in_specs=[pl.BlockSpec((None, tn, d), lambda l, j: (l, j, 0))],
        out_specs=pl.BlockSpec((None, tn, d), lambda l, j: (l, j, 0)),
        out_shape=jax.ShapeDtypeStruct((depth, o_end, d), BF16),
        compiler_params=_params("parallel", "parallel"),
        name="w_head_cast",
    )(w_in_t)
    base = o_end // tn
    per_slab = tn // shift
    tail = pl.pallas_call(
        functools.partial(_shift_cast_kernel, shift=shift),
        grid=(depth, n_tail // tn),
        in_specs=[
            pl.BlockSpec((None, tn, d), lambda l, j: (l, base + j, 0)),
            pl.BlockSpec((None, shift, d), lambda l, j: (l, (base + j + 1) * per_slab, 0)),
        ],
        out_specs=pl.BlockSpec((None, tn, d), lambda l, j: (l, j, 0)),
        out_shape=jax.ShapeDtypeStruct((depth, n_tail, d), BF16),
        compiler_params=_params("parallel", "parallel"),
        name="w_tail_cast",
    )(w_in_t, w_in_t)
    return head, tail


def _inproj_kernel(x_ref, mod_ref, wh_ref, wt_ref, wg_ref, bg_ref, proj_ref, gcol_ref, grow_ref, h_ref,
                   *, d, epilogues):
    tm = x_ref.shape[0]
    L = MLSTM_CHUNK
    G = GATE_STRIDE
    m = mod_ref[...]
    hb = (x_ref[...] * (1.0 + m[:, d:2 * d]) + m[:, 0:d]).astype(BF16)
    h_ref[...] = hb

    n_rows = 4 * G
    gt = _dot_nt(wg_ref[...], hb) + bg_ref[...]
    row = lax.broadcasted_iota(jnp.int32, gt.shape, 0)
    gt = jnp.where(((row >> 3) & 1) == 1, _log_sigmoid(gt), gt)
    r_i = lax.broadcasted_iota(jnp.int32, (L, L), 0)
    c_i = lax.broadcasted_iota(jnp.int32, (L, L), 1)
    upper_b = (r_i <= c_i).astype(BF16)
    lower_b = (r_i >= c_i).astype(BF16)

    def cumsum(x, tri):
        hi, mid, lo = _split3(x)
        return _dot(hi, tri) + _dot(mid, tri) + _dot(lo, tri)

    pieces = []
    for c in range(tm // L):
        xg = gt[:, c * L:(c + 1) * L]
        pre = cumsum(xg[G:2 * G], upper_b)
        suf = cumsum(xg[3 * G:4 * G], lower_b)
        pieces.append(jnp.concatenate([xg[0:G] - pre, pre, xg[2 * G:3 * G] - suf, suf], axis=0))
    gs = jnp.concatenate(pieces, axis=1)
    for qq in range(4):
        grow_ref[qq] = gs[qq * G:(qq + 1) * G, :]
    gcol_ref[...] = jnp.concatenate([gs, jnp.zeros((GATE_LANES - n_rows, tm), F32)], axis=0).T

    tn = PROJ_COLS
    n_head = wh_ref.shape[0] // tn
    for jn, fn in enumerate(epilogues):
        if jn < n_head:
            w = wh_ref[jn * tn:(jn + 1) * tn, :]
        else:
            w = wt_ref[(jn - n_head) * tn:(jn - n_head + 1) * tn, :]
        proj_ref[:, jn * tn:(jn + 1) * tn] = fn(_dot_nt(h_ref[...], w)).astype(BF16)


def _inproj(x, mod, layer, mod_row, w_head, w_tail, w_gate, b_gate, epilogues):
    rows, d = x.shape
    n_h, n_t = w_head.shape[1], w_tail.shape[1]
    n = n_h + n_t
    tm = ROW_TILE
    assert n == PROJ_COLS * len(epilogues) and n_h % PROJ_COLS == 0 and rows % tm == 0 and tm % MLSTM_CHUNK == 0
    return pl.pallas_call(
        functools.partial(_inproj_kernel, d=d, epilogues=epilogues),
        grid=(rows // tm,),
        in_specs=[
            pl.BlockSpec((tm, d), lambda i: (i, 0)),
            pl.BlockSpec((None, None, 1, 6 * d), lambda i: (layer, mod_row(i, tm), 0, 0)),
            _resident((None, n_h, d), lambda i: (layer, 0, 0)),
            _resident((None, n_t, d), lambda i: (layer, 0, 0)),
            _resident((None, 4 * GATE_STRIDE, d), lambda i: (layer, 0, 0)),
            _resident((None, 4 * GATE_STRIDE, 1), lambda i: (layer, 0, 0)),
        ],
        out_specs=[
            pl.BlockSpec((tm, n), lambda i: (i, 0)),
            pl.BlockSpec((tm, GATE_LANES), lambda i: (i, 0)),
            pl.BlockSpec((4, GATE_STRIDE, tm), lambda i: (0, 0, i)),
        ],
        out_shape=[
            jax.ShapeDtypeStruct((rows, n), BF16),
            jax.ShapeDtypeStruct((rows, GATE_LANES), F32),
            jax.ShapeDtypeStruct((4, GATE_STRIDE, rows), F32),
        ],
        scratch_shapes=[pltpu.VMEM((tm, d), BF16)],
        compiler_params=_params("parallel"),
        name="inproj",
    )(x, mod, w_head, w_tail, w_gate, b_gate)


def _mlstm_kernel(*refs, heads, **static):
    for hh in range(heads):
        _mlstm_head(hh, *refs, heads=heads, **static)


def _mlstm_head(hh, *refs, seq, has_init, layer, alias_state, heads):
    if has_init:
        (q_ref, k_ref, v_ref, og_ref, gcol_ref, grow_ref, gain_ref, c0_ref, n0_ref, m0_ref,
         out_ref, cc_ref) = refs
    elif alias_state:
        (q_ref, k_ref, v_ref, og_ref, gcol_ref, grow_ref, gain_ref, _,
         out_ref, cfin_ref, nfin_ref, mfin_ref) = refs
    else:
        (q_ref, k_ref, v_ref, og_ref, gcol_ref, grow_ref, gain_ref,
         out_ref, cfin_ref, nfin_ref, mfin_ref) = refs
    L = MLSTM_CHUNK
    G = GATE_STRIDE
    nc = seq // L
    b_idx = pl.program_id(0)
    h_idx = pl.program_id(1) * heads + hh
    dqk = q_ref.shape[1] // heads
    dv = v_ref.shape[1] // heads
    qc = slice(hh * dqk, (hh + 1) * dqk)
    vc = slice(hh * dv, (hh + 1) * dv)

    s_i = lax.broadcasted_iota(jnp.int32, (L, L), 0)
    t_i = lax.broadcasted_iota(jnp.int32, (L, L), 1)
    masks = (s_i <= t_i, s_i >= t_i)
    eye_b = (s_i == t_i).astype(BF16)
    ones_b = jnp.ones((G, L), BF16)
    lane = lax.broadcasted_iota(jnp.int32, (L, GATE_LANES), 1)

    def gate_rows(c):
        sl = pl.ds(c * L, L)
        hs = pl.ds(h_idx, 1)
        b_f = grow_ref[1, hs, sl]
        b_b = grow_ref[3, hs, sl]
        return ((grow_ref[0, hs, sl], b_f, b_f[:, L - 1:L]), (grow_ref[2, hs, sl], b_b, b_b[:, 0:1]))

    def e_column(c, d):
        gc = gcol_ref[pl.ds(c * L, L), :]
        return jnp.sum(jnp.where(lane == 2 * d * G + h_idx, gc, 0.0), axis=1, keepdims=True)

    def to_columns(rows):
        rep = jnp.concatenate([jnp.broadcast_to(r, (GATE_LANES, L)) for r in rows], axis=0)
        return _dot_nt(eye_b, rep.astype(BF16))

    def wide(col_tile):
        return jnp.concatenate([col_tile] * (dqk // GATE_LANES), axis=1)

    rows = [gate_rows(c) for c in range(nc)]

    def local_update(c):
        e_max, w_rows = [], []
        for d in range(2):
            e_row = rows[c][d][0]
            mx = jnp.max(e_row, axis=1, keepdims=True)
            e_max.append(mx)
            w_rows.append(jnp.exp(e_row - mx))
        w_cols = to_columns(w_rows).astype(BF16)
        k = k_ref[pl.ds(c * L, L), qc]
        kws = [k * wide(w_cols[:, d * GATE_LANES:(d + 1) * GATE_LANES]) for d in range(2)]
        return e_max, kws

    m_pre = [[None] * nc for _ in range(2)]
    n_pre = [[None] * nc for _ in range(2)]
    if has_init:
        local = {c: local_update(c) for c in range(nc)}
        for d in range(2):
            order = list(range(nc)) if d == 0 else list(range(nc - 1, -1, -1))
            m = jnp.full((1, 1), m0_ref[b_idx, layer, d, h_idx], F32)
            n = n0_ref[d, pl.ds(h_idx, 1), :]
            cm = c0_ref[d, hh]
            for pos, c in enumerate(order):
                m_pre[d][c] = m
                n_pre[d][c] = n
                cc_ref[hh, c, :, d * dqk:(d + 1) * dqk] = cm.astype(BF16)
                if pos == nc - 1:
                    break
                e_max, kws = local[c]
                b_tot = rows[c][d][2]
                g_max = b_tot + e_max[d]
                carried = b_tot + m
                m = jnp.maximum(carried, g_max)
                decay = jnp.exp(carried - m)
                up = jnp.exp(g_max - m)
                v = v_ref[pl.ds(c * L, L), vc]
                cm = decay * cm + up * _dot_tn(v, kws[d])
                n = decay * n + up * _dot(ones_b, kws[d])[0:1, :]
    else:
        e_max, kws = local_update(0)
        for d in range(2):
            b_tot = rows[0][d][2]
            g_max = b_tot + e_max[d]
            m_pre[d][0] = jnp.zeros((1, 1), F32)
            m = jnp.maximum(b_tot, g_max)
            up = jnp.exp(g_max - m)
            c_new = up * _dot_tn(v_ref[:, vc], kws[d])
            if alias_state:
                cfin_ref[d, hh] = c_new
            else:
                for lyr in range(cfin_ref.shape[0]):
                    cfin_ref[lyr, d, hh] = c_new if lyr == layer else jnp.zeros_like(c_new)
            nfin_ref[hh, pl.ds(d, 1), :] = up * _dot(ones_b, kws[d])[0:1, :]
            mfin_ref[hh, pl.ds(d, 1), :] = jnp.broadcast_to(m, (1, GATE_LANES))

    gain = gain_ref[hh]
    for c in range(nc):
        q = q_ref[pl.ds(c * L, L), qc]
        k = k_ref[pl.ds(c * L, L), qc]
        v = v_ref[pl.ds(c * L, L), vc]
        qk_t = _dot_nt(k, q)
        if has_init:
            n_rows = jnp.concatenate([n_pre[0][c], n_pre[1][c], jnp.zeros((G - 2, dqk), F32)], axis=0)
            qn_t = _dot_nt(n_rows.astype(BF16), q)
        p_t = None
        scale_rows = []
        for d in range(2):
            _, b_row, _ = rows[c][d]
            em = jnp.where(masks[d], e_column(c, d) * LOG2E, -jnp.inf)
            b2 = b_row * LOG2E
            a2 = b2 + m_pre[d][c] * LOG2E
            mt2 = jnp.maximum(a2, b2 + jnp.max(em, axis=0, keepdims=True))
            s_t = qk_t * jnp.exp2((b2 - mt2) + em)
            den = jnp.sum(s_t, axis=0, keepdims=True)
            if has_init:
                inter = jnp.exp2(a2 - mt2)
                den = den + inter * qn_t[d:d + 1, :]
            r = 1.0 / jnp.maximum(jnp.abs(den), jnp.exp2(-mt2))
            p_t = s_t * r if p_t is None else p_t + s_t * r
            if has_init:
                scale_rows.append(inter * r)
        h = _dot_tn(p_t.astype(BF16), v)
        if has_init:
            sc = to_columns(scale_rows).astype(BF16)
            qs = jnp.concatenate([q * wide(sc[:, d * GATE_LANES:(d + 1) * GATE_LANES]) for d in range(2)], axis=1)
            h = h + _dot_nt(qs, cc_ref[hh, c])
        mu = jnp.mean(h, axis=-1, keepdims=True)
        hc = h - mu
        var = jnp.mean(hc * hc, axis=-1, keepdims=True)
        hn = hc * lax.rsqrt(var + LN_EPS) * gain
        out_ref[pl.ds(c * L, L), vc] = hn.astype(BF16) * og_ref[pl.ds(c * L, L), vc]


def _mlstm(proj, gcol, grow, gain, seq, layer, depth, init=None, state_c=None):
    rows = proj.shape[0]
    n_seq = rows // seq
    dv = gain.shape[3]
    dm = N_HEADS * dv
    dqk = dv // 2
    has_init = init is not None
    hps = LAT_HEADS_PER_STEP if has_init else N_HEADS
    hb = N_HEADS // hps
    vb = (2 * N_HEADS * dqk) // (hps * dv)
    ob = vb + hb
    assert seq % MLSTM_CHUNK == 0 and (has_init or seq == MLSTM_CHUNK) and dqk % GATE_LANES == 0
    in_specs = [
        pl.BlockSpec((seq, hps * dqk), lambda b, h: (b, h)),
        pl.BlockSpec((seq, hps * dqk), lambda b, h: (b, hb + h)),
        pl.BlockSpec((seq, hps * dv), lambda b, h: (b, vb + h)),
        pl.BlockSpec((seq, hps * dv), lambda b, h: (b, ob + h)),
        pl.BlockSpec((seq, GATE_LANES), lambda b, h: (b, 0)),
        pl.BlockSpec((4, GATE_STRIDE, seq), lambda b, h: (0, 0, b)),
        pl.BlockSpec((None, hps, 1, dv), lambda b, h: (layer, h, 0, 0)),
    ]
    args = [proj, proj, proj, proj, gcol, grow, gain]
    out_specs = [pl.BlockSpec((seq, hps * dv), lambda b, h: (b, h))]
    out_shape = [jax.ShapeDtypeStruct((rows, dm), BF16)]
    scratch = []
    aliases = {}
    if has_init:
        init_c, init_n, init_m = init
        in_specs += [
            pl.BlockSpec((None, None, 2, hps, dv, dqk), lambda b, h: (b, layer, 0, h, 0, 0)),
            pl.BlockSpec((None, None, 2, N_HEADS, dqk), lambda b, h: (b, layer, 0, 0, 0)),
            pl.BlockSpec(memory_space=pltpu.MemorySpace.SMEM),
        ]
        args += [init_c, init_n, init_m]
        scratch = [pltpu.VMEM((hps, seq // MLSTM_CHUNK, dv, 2 * dqk), BF16)]
    else:
        if state_c is not None:
            in_specs.append(pl.BlockSpec(memory_space=pl.ANY))
            args.append(state_c)
            aliases = {len(args) - 1: 1}
            state_spec = pl.BlockSpec((None, None, 2, hps, dv, dqk), lambda b, h: (b, layer, 0, h, 0, 0))
        else:
            state_spec = pl.BlockSpec((None, depth, 2, hps, dv, dqk), lambda b, h: (b, 0, 0, h, 0, 0))
        out_specs += [
            state_spec,
            pl.BlockSpec((None, hps, 2, dqk), lambda b, h: (b, h, 0, 0)),
            pl.BlockSpec((None, hps, 2, GATE_LANES), lambda b, h: (b, h, 0, 0)),
        ]
        out_shape += [
            jax.ShapeDtypeStruct((n_seq, depth, 2, N_HEADS, dv, dqk), F32),
            jax.ShapeDtypeStruct((n_seq, N_HEADS, 2, dqk), F32),
            jax.ShapeDtypeStruct((n_seq, N_HEADS, 2, GATE_LANES), F32),
        ]
    return pl.pallas_call(
        functools.partial(_mlstm_kernel, seq=seq, has_init=has_init, layer=layer,
                          alias_state=state_c is not None, heads=hps),
        grid=(n_seq, N_HEADS // hps),
        in_specs=in_specs,
        out_specs=out_specs,
        out_shape=out_shape,
        scratch_shapes=scratch,
        input_output_aliases=aliases,
        compiler_params=_params("parallel", "parallel"),
        name="mlstm_lat" if has_init else "mlstm_ctx",
    )(*args)


def _fourier_kernel(u_ref, cc_ref, sc_ref, ms_ref, o_ref, ab_ref, *, seq, dg):
    for g in range(u_ref.shape[1] // dg):
        u = u_ref[:, g * dg:(g + 1) * dg]
        ab_ref[0:seq, g * dg:(g + 1) * dg] = _dot(u, cc_ref[...]).astype(BF16)
        ab_ref[seq:2 * seq, g * dg:(g + 1) * dg] = _dot(u, sc_ref[...]).astype(BF16)
    o_ref[...] = _dot(ms_ref[...], ab_ref[...]).astype(BF16)


def _dft_tables(seq, dg, grid_w):
    ch = np.arange(dg)
    ang_c = 2.0 * np.pi * ((np.outer(ch, ch) % dg) / dg)
    sc_c = 1.0 / np.sqrt(dg)
    cc = np.cos(ang_c) * sc_c
    sc = np.sin(ang_c) * sc_c
    t = np.arange(seq)
    if grid_w is None:
        frac = (np.outer(t, t) % seq) / seq
    else:
        rows = seq // grid_w
        r, w = t // grid_w, t % grid_w
        frac = (np.outer(r, r) % rows) / rows + (np.outer(w, w) % grid_w) / grid_w
    ang_s = 2.0 * np.pi * frac
    sc_s = 1.0 / np.sqrt(seq)
    ms = np.concatenate([np.cos(ang_s) * sc_s, -np.sin(ang_s) * sc_s], axis=1)
    return (jnp.asarray(cc, F32).astype(BF16), jnp.asarray(sc, F32).astype(BF16),
            jnp.asarray(ms, F32).astype(BF16))


def _fourier(proj, tables, seq, col_block0, df):
    rows = proj.shape[0]
    cc, sc, ms = tables
    dg = cc.shape[0]
    assert col_block0 % (df // dg) == 0
    return pl.pallas_call(
        functools.partial(_fourier_kernel, seq=seq, dg=dg),
        grid=(rows // seq,),
        in_specs=[
            pl.BlockSpec((seq, df), lambda b: (b, col_block0 * dg // df)),
            _resident((dg, dg), lambda b: (0, 0)),
            _resident((dg, dg), lambda b: (0, 0)),
            _resident((seq, 2 * seq), lambda b: (0, 0)),
        ],
        out_specs=pl.BlockSpec((seq, df), lambda b: (b, 0)),
        out_shape=jax.ShapeDtypeStruct((rows, df), BF16),
        scratch_shapes=[pltpu.VMEM((2 * seq, df), BF16)],
        compiler_params=_params("parallel"),
        name="fourier",
    )(proj, cc, sc, ms)


def _merge_kernel(hg_ref, fy_ref, ga_ref, gb_ref, x_ref, mod_ref, wa_ref, wb_ref, wo_ref, lg_ref, lb_ref,
                  o_ref, *, d, alpha):
    ya = _dot(hg_ref[...], wa_ref[...])
    yb = _dot(fy_ref[...], wb_ref[...])
    merged = ga_ref[...].astype(F32) * ya + gb_ref[...].astype(F32) * yb
    out = _dot(merged.astype(BF16), wo_ref[...])
    g1 = mod_ref[...][:, 2 * d:3 * d]
    o_ref[...] = _layer_norm(alpha * x_ref[...] + g1 * out, lg_ref[...], lb_ref[...])


def _merge(hg, fy, proj, x, mod, layer, mod_row, wa, wb, wo, ln_g, ln_b, ga_block, alpha):
    rows, d = x.shape
    dm = hg.shape[1]
    df = fy.shape[1]
    tm = ROW_TILE
    wmap = lambda i: (layer, 0, 0)
    return pl.pallas_call(
        functools.partial(_merge_kernel, d=d, alpha=alpha),
        grid=(rows // tm,),
        in_specs=[
            pl.BlockSpec((tm, dm), lambda i: (i, 0)),
            pl.BlockSpec((tm, df), lambda i: (i, 0)),
            pl.BlockSpec((tm, d), lambda i: (i, ga_block)),
            pl.BlockSpec((tm, d), lambda i: (i, ga_block + 1)),
            pl.BlockSpec((tm, d), lambda i: (i, 0)),
            pl.BlockSpec((None, None, 1, 6 * d), lambda i: (layer, mod_row(i, tm), 0, 0)),
            _resident((None, dm, d), wmap),
            _resident((None, df, d), wmap),
            _resident((None, d, d), wmap),
            _resident((None, None, 1, d), lambda i: (layer, 0, 0, 0)),
            _resident((None, None, 1, d), lambda i: (layer, 0, 0, 0)),
        ],
        out_specs=pl.BlockSpec((tm, d), lambda i: (i, 0)),
        out_shape=jax.ShapeDtypeStruct((rows, d), F32),
        compiler_params=_params("parallel"),
        name="merge",
    )(hg, fy, proj, proj, x, mod, wa, wb, wo, ln_g, ln_b)


def _ffn_kernel(x_ref, mod_ref, w1_ref, w2_ref, lg_ref, lb_ref, o_ref, h_ref, g_ref, *, d, alpha):
    dff = w2_ref.shape[0]
    m = mod_ref[...]
    x = x_ref[...]
    h_ref[...] = (x * (1.0 + m[:, 4 * d:5 * d]) + m[:, 3 * d:4 * d]).astype(BF16)
    tk = FFN_COLS
    for kk in range(dff // tk):
        a = _dot(h_ref[...], w1_ref[:, kk * tk:(kk + 1) * tk])
        u = _dot(h_ref[...], w1_ref[:, dff + kk * tk:dff + (kk + 1) * tk])
        g_ref[:, kk * tk:(kk + 1) * tk] = (a * jax.nn.sigmoid(a) * u).astype(BF16)
    f = _dot(g_ref[...], w2_ref[...])
    g2 = m[:, 5 * d:6 * d]
    o_ref[...] = _layer_norm(alpha * x + g2 * f, lg_ref[...], lb_ref[...])


def _ffn(x, mod, layer, mod_row, w1, w2, ln_g, ln_b, alpha):
    rows, d = x.shape
    dff = w2.shape[1]
    tm = ROW_TILE
    assert dff % FFN_COLS == 0
    return pl.pallas_call(
        functools.partial(_ffn_kernel, d=d, alpha=alpha),
        grid=(rows // tm,),
        in_specs=[
            pl.BlockSpec((tm, d), lambda i: (i, 0)),
            pl.BlockSpec((None, None, 1, 6 * d), lambda i: (layer, mod_row(i, tm), 0, 0)),
            _resident((None, d, 2 * dff), lambda i: (layer, 0, 0)),
            _resident((None, dff, d), lambda i: (layer, 0, 0)),
            _resident((None, None, 1, d), lambda i: (layer, 1, 0, 0)),
            _resident((None, None, 1, d), lambda i: (layer, 1, 0, 0)),
        ],
        out_specs=pl.BlockSpec((tm, d), lambda i: (i, 0)),
        out_shape=jax.ShapeDtypeStruct((rows, d), F32),
        scratch_shapes=[pltpu.VMEM((tm, d), BF16), pltpu.VMEM((tm, dff), BF16)],
        compiler_params=_params("parallel"),
        name="ffn",
    )(x, mod, w1, w2, ln_g, ln_b)


def kernel(x_prompt, x_sample, c, state_C, state_n, state_m, c_ctx, w_mod, b_mod, w_in, b_gate, mh_gain,
           w_branch_a, w_branch_b, w_out, ln_gain, ln_bias, w_ffn_in, w_ffn_out):
    batch, seq_ctx, d = x_prompt.shape
    dec_batch, seq_lat, _ = x_sample.shape
    depth = w_in.shape[0]
    dm = w_branch_a.shape[1]
    df = w_branch_b.shape[1]
    dv = dm // N_HEADS
    dqk = dv // 2
    dg = df // N_FGROUPS
    n_gates = 4 * N_HEADS
    q_end = N_HEADS * dqk
    k_end = 2 * q_end
    v_end = k_end + dm
    o_end = v_end + dm
    g_end = o_end + n_gates
    alpha = float((2 * depth) ** 0.25)

    mod_rows = 16
    cvec = jnp.zeros((mod_rows, d), F32).at[0].set(c_ctx).at[1:1 + dec_batch].set(c)
    mod = _modulation(cvec, w_mod, b_mod).reshape(depth, mod_rows, 1, 6 * d)

    w_in_t = jnp.swapaxes(w_in, 1, 2)
    w_head, w_tail = _split_projection_weights(w_in_t, o_end, g_end)
    gate_pad = ((0, 0), (0, 0), (0, GATE_STRIDE - N_HEADS), (0, 0))
    wg = jnp.pad(w_in_t[:, o_end:g_end, :].reshape(depth, 4, N_HEADS, d), gate_pad)
    wg = wg.reshape(depth, 4 * GATE_STRIDE, d).astype(BF16)
    bg = jnp.pad(b_gate.reshape(depth, 4, N_HEADS, 1), gate_pad).reshape(depth, 4 * GATE_STRIDE, 1)
    wa = w_branch_a.astype(BF16)
    wb = w_branch_b.astype(BF16)
    wo = w_out.astype(BF16)
    w1 = w_ffn_in.astype(BF16)
    w2 = w_ffn_out.astype(BF16)
    gain = mh_gain.reshape(depth, N_HEADS, 1, dv)
    ln_g = ln_gain.reshape(depth, 2, 1, d)
    ln_b = ln_bias.reshape(depth, 2, 1, d)

    def slab_fn(col):
        if q_end <= col < k_end:
            return lambda t: t * (dqk ** -0.5)
        if v_end <= col < o_end or col >= o_end + df:
            return jax.nn.sigmoid
        return lambda t: t
    n_main = o_end + w_tail.shape[1]
    assert all(e % PROJ_COLS == 0 for e in (q_end, k_end, v_end, o_end, o_end + df, n_main))
    epilogues = tuple(slab_fn(j * PROJ_COLS) for j in range(n_main // PROJ_COLS))

    tables_ctx = _dft_tables(seq_ctx, dg, None)
    tables_lat = _dft_tables(seq_lat, dg, GRID_W)
    f_block0 = o_end // dg
    ga_block = (o_end + df) // d

    def ctx_row(i, tm):
        return 0

    def lat_row(i, tm):
        return 1 + (i * tm) // seq_lat

    def layer(x, l, seq, mod_row, tables, init, state_c):
        proj, gcol, grow = _inproj(x, mod, l, mod_row, w_head, w_tail, wg, bg, epilogues)
        res = _mlstm(proj, gcol, grow, gain, seq, l, depth, init, state_c)
        fy = _fourier(proj, tables, seq, f_block0, df)
        x = _merge(res[0], fy, proj, x, mod, l, mod_row, wa, wb, wo, ln_g, ln_b, ga_block, alpha)
        x = _ffn(x, mod, l, mod_row, w1, w2, ln_g, ln_b, alpha)
        return x, res[1:]

    xp = x_prompt.reshape(batch * seq_ctx, d)
    xs = x_sample.reshape(dec_batch * seq_lat, d)
    new_c, new_n, new_m = None, [], []
    for l in range(depth):
        xp, (new_c, nfin, mfin) = layer(xp, l, seq_ctx, ctx_row, tables_ctx, None, new_c)
        new_n.append(nfin.transpose(0, 2, 1, 3))
        new_m.append(mfin[..., 0].transpose(0, 2, 1))
        xs, _ = layer(xs, l, seq_lat, lat_row, tables_lat, (state_C, state_n, state_m), None)
    return (xp.reshape(batch, seq_ctx, d), xs.reshape(dec_batch, seq_lat, d),
            new_c, jnp.stack(new_n, axis=1), jnp.stack(new_m, axis=1))
```

```python
import functools
import math

import numpy as np
import jax
import jax.numpy as jnp
from jax import lax
from jax.experimental import pallas as pl
from jax.experimental.pallas import tpu as pltpu

F32 = jnp.float32
BF16 = jnp.bfloat16

N_HEADS = 4
N_FGROUPS = 4
GRID_W = 64
LN_EPS = 1e-5
MLSTM_CHUNK = 256
GATE_LANES = 128
GATE_STRIDE = 8
ROW_TILE = 512
PROJ_COLS = 1024
FFN_COLS = 256
LAT_HEADS_PER_STEP = 2
VMEM_LIMIT = 56 * 1024 * 1024
LOG2E = math.log2(math.e)

NT_DIMS = (((1,), (1,)), ((), ()))
TN_DIMS = (((0,), (0,)), ((), ()))


def _dot(a, b):
    return jnp.dot(a, b, preferred_element_type=F32)


def _dot_nt(a, b):
    return lax.dot_general(a, b, NT_DIMS, preferred_element_type=F32)


def _dot_tn(a, b):
    return lax.dot_general(a, b, TN_DIMS, preferred_element_type=F32)


def _split3(x):
    hi = x.astype(BF16)
    r = x - hi.astype(F32)
    mid = r.astype(BF16)
    lo = (r - mid.astype(F32)).astype(BF16)
    return hi, mid, lo


def _layer_norm(y, g, b):
    mu = jnp.mean(y, axis=-1, keepdims=True)
    yc = y - mu
    var = jnp.mean(yc * yc, axis=-1, keepdims=True)
    return yc * lax.rsqrt(var + LN_EPS) * g + b


def _log_sigmoid(x):
    return jnp.minimum(x, 0.0) - jnp.log(1.0 + jnp.exp(-jnp.abs(x)))


def _params(*sem):
    return pltpu.CompilerParams(dimension_semantics=sem, vmem_limit_bytes=VMEM_LIMIT)


def _resident(shape, index_map):
    return pl.BlockSpec(shape, index_map, pipeline_mode=pl.Buffered(1))


def _mod_kernel(c_ref, w_ref, b_ref, o_ref):
    c = c_ref[...]
    s = (c * jax.nn.sigmoid(c)).astype(BF16)
    o_ref[...] = _dot(s, w_ref[...].astype(BF16)) + b_ref[...]


def _modulation(cvec, w_mod, b_mod):
    depth, d, n6 = w_mod.shape
    rows = cvec.shape[0]
    tn = 1536
    return pl.pallas_call(
        _mod_kernel,
        grid=(depth, n6 // tn),
        in_specs=[
            pl.BlockSpec((rows, d), lambda l, j: (0, 0)),
            pl.BlockSpec((None, d, tn), lambda l, j: (l, 0, j)),
            pl.BlockSpec((None, 1, tn), lambda l, j: (l, 0, j)),
        ],
        out_specs=pl.BlockSpec((None, rows, tn), lambda l, j: (l, 0, j)),
        out_shape=jax.ShapeDtypeStruct((depth, rows, n6), F32),
        compiler_params=_params("parallel", "parallel"),
        name="modulation",
    )(cvec, w_mod, b_mod.reshape(depth, 1, n6))


def _cast_kernel(w_ref, o_ref):
    o_ref[...] = w_ref[...].T.astype(BF16)


def _shift_cast_kernel(a_ref, b_ref, o_ref, *, shift):
    o_ref[...] = jnp.concatenate([a_ref[shift:, :], b_ref[...]], axis=0).T.astype(BF16)


def _split_projection_weights(w_in_t, o_end, g_end):
    depth, n_in, d = w_in_t.shape
    tn = PROJ_COLS
    shift = g_end - o_end
    n_tail = n_in - g_end
    assert o_end % tn == 0 and n_tail % tn == 0 and shift % 8 == 0 and tn % shift == 0
    head = pl.pallas_call(
        _cast_kernel,
        grid=(depth, o_end // tn),
        in_specs=[pl.BlockSpec((None, tn, d), lambda l, j: (l, j, 0))],
        out_specs=pl.BlockSpec((None, d, tn), lambda l, j: (l, 0, j)),
        out_shape=jax.ShapeDtypeStruct((depth, d, o_end), BF16),
        compiler_params=_params("parallel", "parallel"),
        name="w_head_cast",
    )(w_in_t)
    base = o_end // tn
    per_slab = tn // shift
    tail = pl.pallas_call(
        functools.partial(_shift_cast_kernel, shift=shift),
        grid=(depth, n_tail // tn),
        in_specs=[
            pl.BlockSpec((None, tn, d), lambda l, j: (l, base + j, 0)),
            pl.BlockSpec((None, shift, d), lambda l, j: (l, (base + j + 1) * per_slab, 0)),
        ],
        out_specs=pl.BlockSpec((None, d, tn), lambda l, j: (l, 0, j)),
        out_shape=jax.ShapeDtypeStruct((depth, d, n_tail), BF16),
        compiler_params=_params("parallel", "parallel"),
        name="w_tail_cast",
    )(w_in_t, w_in_t)
    return head, tail


def _inproj_kernel(x_ref, mod_ref, wh_ref, wt_ref, wg_ref, bg_ref, proj_ref, gcol_ref, grow_ref, h_ref,
                   *, d, epilogues):
    tm = x_ref.shape[0]
    L = MLSTM_CHUNK
    G = GATE_STRIDE
    m = mod_ref[...]
    hb = (x_ref[...] * (1.0 + m[:, d:2 * d]) + m[:, 0:d]).astype(BF16)
    h_ref[...] = hb

    n_rows = 4 * G
    g = _dot(hb, wg_ref[...]) + bg_ref[...]
    lane = lax.broadcasted_iota(jnp.int32, g.shape, 1)
    g = jnp.where(((lane >> 3) & 1) == 1, _log_sigmoid(g), g)
    gt = g.T[0:n_rows, :]
    r_i = lax.broadcasted_iota(jnp.int32, (L, L), 0)
    c_i = lax.broadcasted_iota(jnp.int32, (L, L), 1)
    upper_b = (r_i <= c_i).astype(BF16)
    lower_b = (r_i >= c_i).astype(BF16)

    def cumsum(x, tri):
        hi, mid, lo = _split3(x)
        return _dot(hi, tri) + _dot(mid, tri) + _dot(lo, tri)

    pieces = []
    for c in range(tm // L):
        xg = gt[:, c * L:(c + 1) * L]
        pre = cumsum(xg[G:2 * G], upper_b)
        suf = cumsum(xg[3 * G:4 * G], lower_b)
        pieces.append(jnp.concatenate([xg[0:G] - pre, pre, xg[2 * G:3 * G] - suf, suf], axis=0))
    gs = jnp.concatenate(pieces, axis=1)
    for qq in range(4):
        grow_ref[qq] = gs[qq * G:(qq + 1) * G, :]
    gcol_ref[...] = jnp.concatenate([gs, jnp.zeros((GATE_LANES - n_rows, tm), F32)], axis=0).T

    tn = PROJ_COLS
    n_head = wh_ref.shape[1] // tn
    for jn, fn in enumerate(epilogues):
        if jn < n_head:
            w = wh_ref[:, jn * tn:(jn + 1) * tn]
        else:
            w = wt_ref[:, (jn - n_head) * tn:(jn - n_head + 1) * tn]
        proj_ref[:, jn * tn:(jn + 1) * tn] = fn(_dot(h_ref[...], w)).astype(BF16)


def _inproj(x, mod, layer, mod_row, w_head, w_tail, w_gate, b_gate, epilogues):
    rows, d = x.shape
    n_h, n_t = w_head.shape[2], w_tail.shape[2]
    n = n_h + n_t
    tm = ROW_TILE
    assert n == PROJ_COLS * len(epilogues) and n_h % PROJ_COLS == 0 and rows % tm == 0 and tm % MLSTM_CHUNK == 0
    return pl.pallas_call(
        functools.partial(_inproj_kernel, d=d, epilogues=epilogues),
        grid=(rows // tm,),
        in_specs=[
            pl.BlockSpec((tm, d), lambda i: (i, 0)),
            pl.BlockSpec((None, None, 1, 6 * d), lambda i: (layer, mod_row(i, tm), 0, 0)),
            _resident((None, d, n_h), lambda i: (layer, 0, 0)),
            _resident((None, d, n_t), lambda i: (layer, 0, 0)),
            _resident((None, d, GATE_LANES), lambda i: (layer, 0, 0)),
            _resident((None, 1, GATE_LANES), lambda i: (layer, 0, 0)),
        ],
        out_specs=[
            pl.BlockSpec((tm, n), lambda i: (i, 0)),
            pl.BlockSpec((tm, GATE_LANES), lambda i: (i, 0)),
            pl.BlockSpec((4, GATE_STRIDE, tm), lambda i: (0, 0, i)),
        ],
        out_shape=[
            jax.ShapeDtypeStruct((rows, n), BF16),
            jax.ShapeDtypeStruct((rows, GATE_LANES), F32),
            jax.ShapeDtypeStruct((4, GATE_STRIDE, rows), F32),
        ],
        scratch_shapes=[pltpu.VMEM((tm, d), BF16)],
        compiler_params=_params("parallel"),
        name="inproj",
    )(x, mod, w_head, w_tail, w_gate, b_gate)


def _mlstm_kernel(*refs, heads, **static):
    for hh in range(heads):
        _mlstm_head(hh, *refs, heads=heads, **static)


def _mlstm_head(hh, *refs, seq, has_init, layer, alias_state, heads):
    if has_init:
        (q_ref, k_ref, v_ref, og_ref, gcol_ref, grow_ref, gain_ref, c0_ref, n0_ref, m0_ref,
         out_ref, cc_ref) = refs
    elif alias_state:
        (q_ref, k_ref, v_ref, og_ref, gcol_ref, grow_ref, gain_ref, _,
         out_ref, cfin_ref, nfin_ref, mfin_ref) = refs
    else:
        (q_ref, k_ref, v_ref, og_ref, gcol_ref, grow_ref, gain_ref,
         out_ref, cfin_ref, nfin_ref, mfin_ref) = refs
    L = MLSTM_CHUNK
    G = GATE_STRIDE
    nc = seq // L
    b_idx = pl.program_id(0)
    h_idx = pl.program_id(1) * heads + hh
    dqk = q_ref.shape[1] // heads
    dv = v_ref.shape[1] // heads
    qc = slice(hh * dqk, (hh + 1) * dqk)
    vc = slice(hh * dv, (hh + 1) * dv)

    s_i = lax.broadcasted_iota(jnp.int32, (L, L), 0)
    t_i = lax.broadcasted_iota(jnp.int32, (L, L), 1)
    masks = (s_i <= t_i, s_i >= t_i)
    eye_b = (s_i == t_i).astype(BF16)
    ones_b = jnp.ones((G, L), BF16)
    lane = lax.broadcasted_iota(jnp.int32, (L, GATE_LANES), 1)

    def gate_rows(c):
        sl = pl.ds(c * L, L)
        hs = pl.ds(h_idx, 1)
        b_f = grow_ref[1, hs, sl]
        b_b = grow_ref[3, hs, sl]
        return ((grow_ref[0, hs, sl], b_f, b_f[:, L - 1:L]), (grow_ref[2, hs, sl], b_b, b_b[:, 0:1]))

    def e_column(c, d):
        gc = gcol_ref[pl.ds(c * L, L), :]
        return jnp.sum(jnp.where(lane == 2 * d * G + h_idx, gc, 0.0), axis=1, keepdims=True)

    def to_columns(rows):
        rep = jnp.concatenate([jnp.broadcast_to(r, (GATE_LANES, L)) for r in rows], axis=0)
        return _dot_nt(eye_b, rep.astype(BF16))

    def wide(col_tile):
        return jnp.concatenate([col_tile] * (dqk // GATE_LANES), axis=1)

    rows = [gate_rows(c) for c in range(nc)]

    def local_update(c):
        e_max, w_rows = [], []
        for d in range(2):
            e_row = rows[c][d][0]
            mx = jnp.max(e_row, axis=1, keepdims=True)
            e_max.append(mx)
            w_rows.append(jnp.exp(e_row - mx))
        w_cols = to_columns(w_rows).astype(BF16)
        k = k_ref[pl.ds(c * L, L), qc]
        kws = [k * wide(w_cols[:, d * GATE_LANES:(d + 1) * GATE_LANES]) for d in range(2)]
        return e_max, kws

    m_pre = [[None] * nc for _ in range(2)]
    n_pre = [[None] * nc for _ in range(2)]
    if has_init:
        local = {c: local_update(c) for c in range(nc)}
        for d in range(2):
            order = list(range(nc)) if d == 0 else list(range(nc - 1, -1, -1))
            m = jnp.full((1, 1), m0_ref[b_idx, layer, d, h_idx], F32)
            n = n0_ref[d, pl.ds(h_idx, 1), :]
            cm = c0_ref[d, hh]
            for pos, c in enumerate(order):
                m_pre[d][c] = m
                n_pre[d][c] = n
                cc_ref[hh, c, :, d * dqk:(d + 1) * dqk] = cm.astype(BF16)
                if pos == nc - 1:
                    break
                e_max, kws = local[c]
                b_tot = rows[c][d][2]
                g_max = b_tot + e_max[d]
                carried = b_tot + m
                m = jnp.maximum(carried, g_max)
                decay = jnp.exp(carried - m)
                up = jnp.exp(g_max - m)
                v = v_ref[pl.ds(c * L, L), vc]
                cm = decay * cm + up * _dot_tn(v, kws[d])
                n = decay * n + up * _dot(ones_b, kws[d])[0:1, :]
    else:
        e_max, kws = local_update(0)
        for d in range(2):
            b_tot = rows[0][d][2]
            g_max = b_tot + e_max[d]
            m_pre[d][0] = jnp.zeros((1, 1), F32)
            m = jnp.maximum(b_tot, g_max)
            up = jnp.exp(g_max - m)
            c_new = up * _dot_tn(v_ref[:, vc], kws[d])
            if alias_state:
                cfin_ref[d, hh] = c_new
            else:
                for lyr in range(cfin_ref.shape[0]):
                    cfin_ref[lyr, d, hh] = c_new if lyr == layer else jnp.zeros_like(c_new)
            nfin_ref[hh, pl.ds(d, 1), :] = up * _dot(ones_b, kws[d])[0:1, :]
            mfin_ref[hh, pl.ds(d, 1), :] = jnp.broadcast_to(m, (1, GATE_LANES))

    gain = gain_ref[hh]
    for c in range(nc):
        q = q_ref[pl.ds(c * L, L), qc]
        k = k_ref[pl.ds(c * L, L), qc]
        v = v_ref[pl.ds(c * L, L), vc]
        qk_t = _dot_nt(k, q)
        if has_init:
            n_rows = jnp.concatenate([n_pre[0][c], n_pre[1][c], jnp.zeros((G - 2, dqk), F32)], axis=0)
            qn_t = _dot_nt(n_rows.astype(BF16), q)
        p_t = None
        scale_rows = []
        for d in range(2):
            _, b_row, _ = rows[c][d]
            em = jnp.where(masks[d], e_column(c, d) * LOG2E, -jnp.inf)
            b2 = b_row * LOG2E
            a2 = b2 + m_pre[d][c] * LOG2E
            mt2 = jnp.maximum(a2, b2 + jnp.max(em, axis=0, keepdims=True))
            s_t = qk_t * jnp.exp2((b2 - mt2) + em)
            den = jnp.sum(s_t, axis=0, keepdims=True)
            if has_init:
                inter = jnp.exp2(a2 - mt2)
                den = den + inter * qn_t[d:d + 1, :]
            r = 1.0 / jnp.maximum(jnp.abs(den), jnp.exp2(-mt2))
            p_t = s_t * r if p_t is None else p_t + s_t * r
            if has_init:
                scale_rows.append(inter * r)
        h = _dot_tn(p_t.astype(BF16), v)
        if has_init:
            sc = to_columns(scale_rows).astype(BF16)
            qs = jnp.concatenate([q * wide(sc[:, d * GATE_LANES:(d + 1) * GATE_LANES]) for d in range(2)], axis=1)
            h = h + _dot_nt(qs, cc_ref[hh, c])
        mu = jnp.mean(h, axis=-1, keepdims=True)
        hc = h - mu
        var = jnp.mean(hc * hc, axis=-1, keepdims=True)
        hn = hc * lax.rsqrt(var + LN_EPS) * gain
        out_ref[pl.ds(c * L, L), vc] = hn.astype(BF16) * og_ref[pl.ds(c * L, L), vc]


def _mlstm(proj, gcol, grow, gain, seq, layer, depth, init=None, state_c=None):
    rows = proj.shape[0]
    n_seq = rows // seq
    dv = gain.shape[3]
    dm = N_HEADS * dv
    dqk = dv // 2
    has_init = init is not None
    hps = LAT_HEADS_PER_STEP if has_init else N_HEADS
    hb = N_HEADS // hps
    vb = (2 * N_HEADS * dqk) // (hps * dv)
    ob = vb + hb
    assert seq % MLSTM_CHUNK == 0 and (has_init or seq == MLSTM_CHUNK) and dqk % GATE_LANES == 0
    in_specs = [
        pl.BlockSpec((seq, hps * dqk), lambda b, h: (b, h)),
        pl.BlockSpec((seq, hps * dqk), lambda b, h: (b, hb + h)),
        pl.BlockSpec((seq, hps * dv), lambda b, h: (b, vb + h)),
        pl.BlockSpec((seq, hps * dv), lambda b, h: (b, ob + h)),
        pl.BlockSpec((seq, GATE_LANES), lambda b, h: (b, 0)),
        pl.BlockSpec((4, GATE_STRIDE, seq), lambda b, h: (0, 0, b)),
        pl.BlockSpec((None, hps, 1, dv), lambda b, h: (layer, h, 0, 0)),
    ]
    args = [proj, proj, proj, proj, gcol, grow, gain]
    out_specs = [pl.BlockSpec((seq, hps * dv), lambda b, h: (b, h))]
    out_shape = [jax.ShapeDtypeStruct((rows, dm), BF16)]
    scratch = []
    aliases = {}
    if has_init:
        init_c, init_n, init_m = init
        in_specs += [
            pl.BlockSpec((None, None, 2, hps, dv, dqk), lambda b, h: (b, layer, 0, h, 0, 0)),
            pl.BlockSpec((None, None, 2, N_HEADS, dqk), lambda b, h: (b, layer, 0, 0, 0)),
            pl.BlockSpec(memory_space=pltpu.MemorySpace.SMEM),
        ]
        args += [init_c, init_n, init_m]
        scratch = [pltpu.VMEM((hps, seq // MLSTM_CHUNK, dv, 2 * dqk), BF16)]
    else:
        if state_c is not None:
            in_specs.append(pl.BlockSpec(memory_space=pl.ANY))
            args.append(state_c)
            aliases = {len(args) - 1: 1}
            state_spec = pl.BlockSpec((None, None, 2, hps, dv, dqk), lambda b, h: (b, layer, 0, h, 0, 0))
        else:
            state_spec = pl.BlockSpec((None, depth, 2, hps, dv, dqk), lambda b, h: (b, 0, 0, h, 0, 0))
        out_specs += [
            state_spec,
            pl.BlockSpec((None, hps, 2, dqk), lambda b, h: (b, h, 0, 0)),
            pl.BlockSpec((None, hps, 2, GATE_LANES), lambda b, h: (b, h, 0, 0)),
        ]
        out_shape += [
            jax.ShapeDtypeStruct((n_seq, depth, 2, N_HEADS, dv, dqk), F32),
            jax.ShapeDtypeStruct((n_seq, N_HEADS, 2, dqk), F32),
            jax.ShapeDtypeStruct((n_seq, N_HEADS, 2, GATE_LANES), F32),
        ]
    return pl.pallas_call(
        functools.partial(_mlstm_kernel, seq=seq, has_init=has_init, layer=layer,
                          alias_state=state_c is not None, heads=hps),
        grid=(n_seq, N_HEADS // hps),
        in_specs=in_specs,
        out_specs=out_specs,
        out_shape=out_shape,
        scratch_shapes=scratch,
        input_output_aliases=aliases,
        compiler_params=_params("parallel", "parallel"),
        name="mlstm_lat" if has_init else "mlstm_ctx",
    )(*args)


def _fourier_kernel(u_ref, cc_ref, sc_ref, ms_ref, o_ref, ab_ref, *, seq, dg):
    for g in range(u_ref.shape[1] // dg):
        u = u_ref[:, g * dg:(g + 1) * dg]
        ab_ref[0:seq, g * dg:(g + 1) * dg] = _dot(u, cc_ref[...]).astype(BF16)
        ab_ref[seq:2 * seq, g * dg:(g + 1) * dg] = _dot(u, sc_ref[...]).astype(BF16)
    o_ref[...] = _dot(ms_ref[...], ab_ref[...]).astype(BF16)


def _dft_tables(seq, dg, grid_w):
    ch = np.arange(dg)
    ang_c = 2.0 * np.pi * ((np.outer(ch, ch) % dg) / dg)
    sc_c = 1.0 / np.sqrt(dg)
    cc = np.cos(ang_c) * sc_c
    sc = np.sin(ang_c) * sc_c
    t = np.arange(seq)
    if grid_w is None:
        frac = (np.outer(t, t) % seq) / seq
    else:
        rows = seq // grid_w
        r, w = t // grid_w, t % grid_w
        frac = (np.outer(r, r) % rows) / rows + (np.outer(w, w) % grid_w) / grid_w
    ang_s = 2.0 * np.pi * frac
    sc_s = 1.0 / np.sqrt(seq)
    ms = np.concatenate([np.cos(ang_s) * sc_s, -np.sin(ang_s) * sc_s], axis=1)
    return (jnp.asarray(cc, F32).astype(BF16), jnp.asarray(sc, F32).astype(BF16),
            jnp.asarray(ms, F32).astype(BF16))


def _fourier(proj, tables, seq, col_block0, df):
    rows = proj.shape[0]
    cc, sc, ms = tables
    dg = cc.shape[0]
    assert col_block0 % (df // dg) == 0
    return pl.pallas_call(
        functools.partial(_fourier_kernel, seq=seq, dg=dg),
        grid=(rows // seq,),
        in_specs=[
            pl.BlockSpec((seq, df), lambda b: (b, col_block0 * dg // df)),
            _resident((dg, dg), lambda b: (0, 0)),
            _resident((dg, dg), lambda b: (0, 0)),
            _resident((seq, 2 * seq), lambda b: (0, 0)),
        ],
        out_specs=pl.BlockSpec((seq, df), lambda b: (b, 0)),
        out_shape=jax.ShapeDtypeStruct((rows, df), BF16),
        scratch_shapes=[pltpu.VMEM((2 * seq, df), BF16)],
        compiler_params=_params("parallel"),
        name="fourier",
    )(proj, cc, sc, ms)


def _merge_kernel(hg_ref, fy_ref, ga_ref, gb_ref, x_ref, mod_ref, wa_ref, wb_ref, wo_ref, lg_ref, lb_ref,
                  o_ref, *, d, alpha):
    ya = _dot(hg_ref[...], wa_ref[...])
    yb = _dot(fy_ref[...], wb_ref[...])
    merged = ga_ref[...].astype(F32) * ya + gb_ref[...].astype(F32) * yb
    out = _dot(merged.astype(BF16), wo_ref[...])
    g1 = mod_ref[...][:, 2 * d:3 * d]
    o_ref[...] = _layer_norm(alpha * x_ref[...] + g1 * out, lg_ref[...], lb_ref[...])


def _merge(hg, fy, proj, x, mod, layer, mod_row, wa, wb, wo, ln_g, ln_b, ga_block, alpha):
    rows, d = x.shape
    dm = hg.shape[1]
    df = fy.shape[1]
    tm = ROW_TILE
    wmap = lambda i: (layer, 0, 0)
    return pl.pallas_call(
        functools.partial(_merge_kernel, d=d, alpha=alpha),
        grid=(rows // tm,),
        in_specs=[
            pl.BlockSpec((tm, dm), lambda i: (i, 0)),
            pl.BlockSpec((tm, df), lambda i: (i, 0)),
            pl.BlockSpec((tm, d), lambda i: (i, ga_block)),
            pl.BlockSpec((tm, d), lambda i: (i, ga_block + 1)),
            pl.BlockSpec((tm, d), lambda i: (i, 0)),
            pl.BlockSpec((None, None, 1, 6 * d), lambda i: (layer, mod_row(i, tm), 0, 0)),
            _resident((None, dm, d), wmap),
            _resident((None, df, d), wmap),
            _resident((None, d, d), wmap),
            _resident((None, None, 1, d), lambda i: (layer, 0, 0, 0)),
            _resident((None, None, 1, d), lambda i: (layer, 0, 0, 0)),
        ],
        out_specs=pl.BlockSpec((tm, d), lambda i: (i, 0)),
        out_shape=jax.ShapeDtypeStruct((rows, d), F32),
        compiler_params=_params("parallel"),
        name="merge",
    )(hg, fy, proj, proj, x, mod, wa, wb, wo, ln_g, ln_b)


def _ffn_kernel(x_ref, mod_ref, w1_ref, w2_ref, lg_ref, lb_ref, o_ref, h_ref, g_ref, *, d, alpha):
    dff = w2_ref.shape[0]
    m = mod_ref[...]
    x = x_ref[...]
    h_ref[...] = (x * (1.0 + m[:, 4 * d:5 * d]) + m[:, 3 * d:4 * d]).astype(BF16)
    tk = FFN_COLS
    for kk in range(dff // tk):
        a = _dot(h_ref[...], w1_ref[:, kk * tk:(kk + 1) * tk])
        u = _dot(h_ref[...], w1_ref[:, dff + kk * tk:dff + (kk + 1) * tk])
        g_ref[:, kk * tk:(kk + 1) * tk] = (a * jax.nn.sigmoid(a) * u).astype(BF16)
    f = _dot(g_ref[...], w2_ref[...])
    g2 = m[:, 5 * d:6 * d]
    o_ref[...] = _layer_norm(alpha * x + g2 * f, lg_ref[...], lb_ref[...])


def _ffn(x, mod, layer, mod_row, w1, w2, ln_g, ln_b, alpha):
    rows, d = x.shape
    dff = w2.shape[1]
    tm = ROW_TILE
    assert dff % FFN_COLS == 0
    return pl.pallas_call(
        functools.partial(_ffn_kernel, d=d, alpha=alpha),
        grid=(rows // tm,),
        in_specs=[
            pl.BlockSpec((tm, d), lambda i: (i, 0)),
            pl.BlockSpec((None, None, 1, 6 * d), lambda i: (layer, mod_row(i, tm), 0, 0)),
            _resident((None, d, 2 * dff), lambda i: (layer, 0, 0)),
            _resident((None, dff, d), lambda i: (layer, 0, 0)),
            _resident((None, None, 1, d), lambda i: (layer, 1, 0, 0)),
            _resident((None, None, 1, d), lambda i: (layer, 1, 0, 0)),
        ],
        out_specs=pl.BlockSpec((tm, d), lambda i: (i, 0)),
        out_shape=jax.ShapeDtypeStruct((rows, d), F32),
        scratch_shapes=[pltpu.VMEM((tm, d), BF16), pltpu.VMEM((tm, dff), BF16)],
        compiler_params=_params("parallel"),
        name="ffn",
    )(x, mod, w1, w2, ln_g, ln_b)


def kernel(x_prompt, x_sample, c, state_C, state_n, state_m, c_ctx, w_mod, b_mod, w_in, b_gate, mh_gain,
           w_branch_a, w_branch_b, w_out, ln_gain, ln_bias, w_ffn_in, w_ffn_out):
    batch, seq_ctx, d = x_prompt.shape
    dec_batch, seq_lat, _ = x_sample.shape
    depth = w_in.shape[0]
    dm = w_branch_a.shape[1]
    df = w_branch_b.shape[1]
    dv = dm // N_HEADS
    dqk = dv // 2
    dg = df // N_FGROUPS
    n_gates = 4 * N_HEADS
    q_end = N_HEADS * dqk
    k_end = 2 * q_end
    v_end = k_end + dm
    o_end = v_end + dm
    g_end = o_end + n_gates
    alpha = float((2 * depth) ** 0.25)

    mod_rows = 16
    cvec = jnp.zeros((mod_rows, d), F32).at[0].set(c_ctx).at[1:1 + dec_batch].set(c)
    mod = _modulation(cvec, w_mod, b_mod).reshape(depth, mod_rows, 1, 6 * d)

    w_in_t = jnp.swapaxes(w_in, 1, 2)
    w_head, w_tail = _split_projection_weights(w_in_t, o_end, g_end)
    gate_pad = ((0, 0), (0, 0), (0, GATE_STRIDE - N_HEADS), (0, 0))
    lane_pad = ((0, 0), (0, 0), (0, GATE_LANES - 4 * GATE_STRIDE))
    wg = jnp.pad(w_in_t[:, o_end:g_end, :].reshape(depth, 4, N_HEADS, d), gate_pad)
    wg = jnp.pad(jnp.swapaxes(wg.reshape(depth, 4 * GATE_STRIDE, d), 1, 2), lane_pad).astype(BF16)
    bg = jnp.pad(b_gate.reshape(depth, 4, N_HEADS, 1), gate_pad).reshape(depth, 1, 4 * GATE_STRIDE)
    bg = jnp.pad(bg, lane_pad)
    wa = w_branch_a.astype(BF16)
    wb = w_branch_b.astype(BF16)
    wo = w_out.astype(BF16)
    w1 = w_ffn_in.astype(BF16)
    w2 = w_ffn_out.astype(BF16)
    gain = mh_gain.reshape(depth, N_HEADS, 1, dv)
    ln_g = ln_gain.reshape(depth, 2, 1, d)
    ln_b = ln_bias.reshape(depth, 2, 1, d)

    def slab_fn(col):
        if q_end <= col < k_end:
            return lambda t: t * (dqk ** -0.5)
        if v_end <= col < o_end or col >= o_end + df:
            return jax.nn.sigmoid
        return lambda t: t
    n_main = o_end + w_tail.shape[2]
    assert all(e % PROJ_COLS == 0 for e in (q_end, k_end, v_end, o_end, o_end + df, n_main))
    epilogues = tuple(slab_fn(j * PROJ_COLS) for j in range(n_main // PROJ_COLS))

    tables_ctx = _dft_tables(seq_ctx, dg, None)
    tables_lat = _dft_tables(seq_lat, dg, GRID_W)
    f_block0 = o_end // dg
    ga_block = (o_end + df) // d

    def ctx_row(i, tm):
        return 0

    def lat_row(i, tm):
        return 1 + (i * tm) // seq_lat

    def layer(x, l, seq, mod_row, tables, init, state_c):
        proj, gcol, grow = _inproj(x, mod, l, mod_row, w_head, w_tail, wg, bg, epilogues)
        res = _mlstm(proj, gcol, grow, gain, seq, l, depth, init, state_c)
        fy = _fourier(proj, tables, seq, f_block0, df)
        x = _merge(res[0], fy, proj, x, mod, l, mod_row, wa, wb, wo, ln_g, ln_b, ga_block, alpha)
        x = _ffn(x, mod, l, mod_row, w1, w2, ln_g, ln_b, alpha)
        return x, res[1:]

    xp = x_prompt.reshape(batch * seq_ctx, d)
    xs = x_sample.reshape(dec_batch * seq_lat, d)
    new_c, new_n, new_m = None, [], []
    for l in range(depth):
        xp, (new_c, nfin, mfin) = layer(xp, l, seq_ctx, ctx_row, tables_ctx, None, new_c)
        new_n.append(nfin.transpose(0, 2, 1, 3))
        new_m.append(mfin[..., 0].transpose(0, 2, 1))
        xs, _ = layer(xs, l, seq_lat, lat_row, tables_lat, (state_C, state_n, state_m), None)
    return (xp.reshape(batch, seq_ctx, d), xs.reshape(dec_batch, seq_lat, d),
            new_c, jnp.stack(new_n, axis=1), jnp.stack(new_m, axis=1))
```

```python
import functools
import math

import numpy as np
import jax
import jax.numpy as jnp
from jax import lax
from jax.experimental import pallas as pl
from jax.experimental.pallas import tpu as pltpu

F32 = jnp.float32
BF16 = jnp.bfloat16

N_HEADS = 4
N_FGROUPS = 4
GRID_W = 64
LN_EPS = 1e-5
MLSTM_CHUNK = 256
GATE_LANES = 128
GATE_STRIDE = 8
ROW_TILE = 512
PROJ_COLS = 1024
FFN_COLS = 256
LAT_HEADS_PER_STEP = 2
VMEM_LIMIT = 56 * 1024 * 1024
LOG2E = math.log2(math.e)

NT_DIMS = (((1,), (1,)), ((), ()))
TN_DIMS = (((0,), (0,)), ((), ()))


def _dot(a, b):
    return jnp.dot(a, b, preferred_element_type=F32)


def _dot_nt(a, b):
    return lax.dot_general(a, b, NT_DIMS, preferred_element_type=F32)


def _dot_tn(a, b):
    return lax.dot_general(a, b, TN_DIMS, preferred_element_type=F32)


def _split3(x):
    hi = x.astype(BF16)
    r = x - hi.astype(F32)
    mid = r.astype(BF16)
    lo = (r - mid.astype(F32)).astype(BF16)
    return hi, mid, lo


def _layer_norm(y, g, b):
    mu = jnp.mean(y, axis=-1, keepdims=True)
    yc = y - mu
    var = jnp.mean(yc * yc, axis=-1, keepdims=True)
    return yc * lax.rsqrt(var + LN_EPS) * g + b


def _log_sigmoid(x):
    return jnp.minimum(x, 0.0) - jnp.log(1.0 + jnp.exp(-jnp.abs(x)))


def _params(*sem):
    return pltpu.CompilerParams(dimension_semantics=sem, vmem_limit_bytes=VMEM_LIMIT)


def _resident(shape, index_map):
    return pl.BlockSpec(shape, index_map, pipeline_mode=pl.Buffered(1))


def _mod_kernel(c_ref, w_ref, b_ref, o_ref):
    c = c_ref[...]
    s = (c * jax.nn.sigmoid(c)).astype(BF16)
    o_ref[...] = _dot(s, w_ref[...].astype(BF16)) + b_ref[...]


def _modulation(cvec, w_mod, b_mod):
    depth, d, n6 = w_mod.shape
    rows = cvec.shape[0]
    tn = 1536
    return pl.pallas_call(
        _mod_kernel,
        grid=(depth, n6 // tn),
        in_specs=[
            pl.BlockSpec((rows, d), lambda l, j: (0, 0)),
            pl.BlockSpec((None, d, tn), lambda l, j: (l, 0, j)),
            pl.BlockSpec((None, 1, tn), lambda l, j: (l, 0, j)),
        ],
        out_specs=pl.BlockSpec((None, rows, tn), lambda l, j: (l, 0, j)),
        out_shape=jax.ShapeDtypeStruct((depth, rows, n6), F32),
        compiler_params=_params("parallel", "parallel"),
        name="modulation",
    )(cvec, w_mod, b_mod.reshape(depth, 1, n6))


def _cast_kernel(w_ref, o_ref):
    o_ref[...] = w_ref[...].astype(BF16)


def _transpose_cast_kernel(w_ref, o_ref):
    o_ref[...] = w_ref[...].T.astype(BF16)


def _shift_cast_kernel(a_ref, b_ref, o_ref, *, shift):
    o_ref[...] = jnp.concatenate([a_ref[shift:, :], b_ref[...]], axis=0).T.astype(BF16)


def _split_projection_weights(w_in_t, k_end, o_end, g_end):
    depth, n_in, d = w_in_t.shape
    tn = PROJ_COLS
    shift = g_end - o_end
    n_tail = n_in - g_end
    assert k_end % tn == 0 and o_end % tn == 0 and n_tail % tn == 0 and shift % 8 == 0 and tn % shift == 0
    w_qk = pl.pallas_call(
        _transpose_cast_kernel,
        grid=(depth, k_end // tn),
        in_specs=[pl.BlockSpec((None, tn, d), lambda l, j: (l, j, 0))],
        out_specs=pl.BlockSpec((None, d, tn), lambda l, j: (l, 0, j)),
        out_shape=jax.ShapeDtypeStruct((depth, d, k_end), BF16),
        compiler_params=_params("parallel", "parallel"),
        name="w_qk_cast",
    )(w_in_t)
    kb = k_end // tn
    w_vo = pl.pallas_call(
        _cast_kernel,
        grid=(depth, (o_end - k_end) // tn),
        in_specs=[pl.BlockSpec((None, tn, d), lambda l, j: (l, kb + j, 0))],
        out_specs=pl.BlockSpec((None, tn, d), lambda l, j: (l, j, 0)),
        out_shape=jax.ShapeDtypeStruct((depth, o_end - k_end, d), BF16),
        compiler_params=_params("parallel", "parallel"),
        name="w_vo_cast",
    )(w_in_t)
    base = o_end // tn
    per_slab = tn // shift
    tail = pl.pallas_call(
        functools.partial(_shift_cast_kernel, shift=shift),
        grid=(depth, n_tail // tn),
        in_specs=[
            pl.BlockSpec((None, tn, d), lambda l, j: (l, base + j, 0)),
            pl.BlockSpec((None, shift, d), lambda l, j: (l, (base + j + 1) * per_slab, 0)),
        ],
        out_specs=pl.BlockSpec((None, d, tn), lambda l, j: (l, 0, j)),
        out_shape=jax.ShapeDtypeStruct((depth, d, n_tail), BF16),
        compiler_params=_params("parallel", "parallel"),
        name="w_tail_cast",
    )(w_in_t, w_in_t)
    return w_qk, w_vo, tail


def _inproj_kernel(x_ref, mod_ref, wqk_ref, wvo_ref, wt_ref, wg_ref, bg_ref, proj_ref, proj_t_ref, gcol_ref,
                   grow_ref, h_ref, *, d, epilogues):
    tm = x_ref.shape[0]
    L = MLSTM_CHUNK
    G = GATE_STRIDE
    m = mod_ref[...]
    hb = (x_ref[...] * (1.0 + m[:, d:2 * d]) + m[:, 0:d]).astype(BF16)
    h_ref[...] = hb

    n_rows = 4 * G
    g = _dot(hb, wg_ref[...]) + bg_ref[...]
    lane = lax.broadcasted_iota(jnp.int32, g.shape, 1)
    g = jnp.where(((lane >> 3) & 1) == 1, _log_sigmoid(g), g)
    gt = g.T[0:n_rows, :]
    r_i = lax.broadcasted_iota(jnp.int32, (L, L), 0)
    c_i = lax.broadcasted_iota(jnp.int32, (L, L), 1)
    upper_b = (r_i <= c_i).astype(BF16)
    lower_b = (r_i >= c_i).astype(BF16)

    def cumsum(x, tri):
        hi, mid, lo = _split3(x)
        return _dot(hi, tri) + _dot(mid, tri) + _dot(lo, tri)

    pieces = []
    for c in range(tm // L):
        xg = gt[:, c * L:(c + 1) * L]
        pre = cumsum(xg[G:2 * G], upper_b)
        suf = cumsum(xg[3 * G:4 * G], lower_b)
        pieces.append(jnp.concatenate([xg[0:G] - pre, pre, xg[2 * G:3 * G] - suf, suf], axis=0))
    gs = jnp.concatenate(pieces, axis=1)
    for qq in range(4):
        grow_ref[qq] = gs[qq * G:(qq + 1) * G, :]
    gcol_ref[...] = jnp.concatenate([gs, jnp.zeros((GATE_LANES - n_rows, tm), F32)], axis=0).T

    tn = PROJ_COLS
    n_qk = wqk_ref.shape[1] // tn
    n_vo = wvo_ref.shape[0] // tn
    for jn, fn in enumerate(epilogues):
        if jn < n_qk:
            proj_ref[:, jn * tn:(jn + 1) * tn] = fn(_dot(h_ref[...], wqk_ref[:, jn * tn:(jn + 1) * tn])).astype(BF16)
        elif jn < n_qk + n_vo:
            jv = jn - n_qk
            proj_t_ref[jv * tn:(jv + 1) * tn, :] = fn(_dot_nt(wvo_ref[jv * tn:(jv + 1) * tn, :], h_ref[...])).astype(BF16)
        else:
            jt = jn - n_qk - n_vo
            jo = jn - n_vo
            proj_ref[:, jo * tn:(jo + 1) * tn] = fn(_dot(h_ref[...], wt_ref[:, jt * tn:(jt + 1) * tn])).astype(BF16)


def _inproj(x, mod, layer, mod_row, w_qk, w_vo, w_tail, w_gate, b_gate, epilogues):
    rows, d = x.shape
    n_qk, n_vo, n_t = w_qk.shape[2], w_vo.shape[1], w_tail.shape[2]
    n = n_qk + n_t
    tm = ROW_TILE
    assert n + n_vo == PROJ_COLS * len(epilogues) and rows % tm == 0 and tm % MLSTM_CHUNK == 0
    return pl.pallas_call(
        functools.partial(_inproj_kernel, d=d, epilogues=epilogues),
        grid=(rows // tm,),
        in_specs=[
            pl.BlockSpec((tm, d), lambda i: (i, 0)),
            pl.BlockSpec((None, None, 1, 6 * d), lambda i: (layer, mod_row(i, tm), 0, 0)),
            _resident((None, d, n_qk), lambda i: (layer, 0, 0)),
            _resident((None, n_vo, d), lambda i: (layer, 0, 0)),
            _resident((None, d, n_t), lambda i: (layer, 0, 0)),
            _resident((None, d, GATE_LANES), lambda i: (layer, 0, 0)),
            _resident((None, 1, GATE_LANES), lambda i: (layer, 0, 0)),
        ],
        out_specs=[
            pl.BlockSpec((tm, n), lambda i: (i, 0)),
            pl.BlockSpec((n_vo, tm), lambda i: (0, i)),
            pl.BlockSpec((tm, GATE_LANES), lambda i: (i, 0)),
            pl.BlockSpec((4, GATE_STRIDE, tm), lambda i: (0, 0, i)),
        ],
        out_shape=[
            jax.ShapeDtypeStruct((rows, n), BF16),
            jax.ShapeDtypeStruct((n_vo, rows), BF16),
            jax.ShapeDtypeStruct((rows, GATE_LANES), F32),
            jax.ShapeDtypeStruct((4, GATE_STRIDE, rows), F32),
        ],
        scratch_shapes=[pltpu.VMEM((tm, d), BF16)],
        compiler_params=_params("parallel"),
        name="inproj",
    )(x, mod, w_qk, w_vo, w_tail, w_gate, b_gate)


def _mlstm_kernel(*refs, heads, **static):
    for hh in range(heads):
        _mlstm_head(hh, *refs, heads=heads, **static)


def _mlstm_head(hh, *refs, seq, has_init, layer, alias_state, heads):
    if has_init:
        (q_ref, k_ref, vt_ref, ogt_ref, gcol_ref, grow_ref, gain_ref, c0_ref, n0_ref, m0_ref,
         out_ref, cc_ref) = refs
    elif alias_state:
        (q_ref, k_ref, vt_ref, ogt_ref, gcol_ref, grow_ref, gain_ref, _,
         out_ref, cfin_ref, nfin_ref, mfin_ref) = refs
    else:
        (q_ref, k_ref, vt_ref, ogt_ref, gcol_ref, grow_ref, gain_ref,
         out_ref, cfin_ref, nfin_ref, mfin_ref) = refs
    L = MLSTM_CHUNK
    G = GATE_STRIDE
    nc = seq // L
    b_idx = pl.program_id(0)
    h_idx = pl.program_id(1) * heads + hh
    dqk = q_ref.shape[1] // heads
    dv = vt_ref.shape[0] // heads
    qc = slice(hh * dqk, (hh + 1) * dqk)
    vr = slice(hh * dv, (hh + 1) * dv)

    s_i = lax.broadcasted_iota(jnp.int32, (L, L), 0)
    t_i = lax.broadcasted_iota(jnp.int32, (L, L), 1)
    masks = (s_i <= t_i, s_i >= t_i)
    lane = lax.broadcasted_iota(jnp.int32, (L, GATE_LANES), 1)

    def gate_rows(c):
        sl = pl.ds(c * L, L)
        hs = pl.ds(h_idx, 1)
        b_f = grow_ref[1, hs, sl]
        b_b = grow_ref[3, hs, sl]
        return ((grow_ref[0, hs, sl], b_f, b_f[:, L - 1:L]), (grow_ref[2, hs, sl], b_b, b_b[:, 0:1]))

    def e_column(c, d):
        gc = gcol_ref[pl.ds(c * L, L), :]
        return jnp.sum(jnp.where(lane == 2 * d * G + h_idx, gc, 0.0), axis=1, keepdims=True)

    rows = [gate_rows(c) for c in range(nc)]

    def local_update(c, d):
        e_row = rows[c][d][0]
        e_max = jnp.max(e_row, axis=1, keepdims=True)
        w_row = jnp.exp(e_row - e_max).astype(BF16)
        k = k_ref[pl.ds(c * L, L), qc]
        d_c = _dot(vt_ref[vr, pl.ds(c * L, L)] * w_row, k)
        d_n = _dot(jnp.broadcast_to(w_row, (G, L)), k)[0:1, :]
        return e_max, d_c, d_n

    m_pre = [[None] * nc for _ in range(2)]
    n_pre = [[None] * nc for _ in range(2)]
    if has_init:
        for d in range(2):
            order = list(range(nc)) if d == 0 else list(range(nc - 1, -1, -1))
            local = {c: local_update(c, d) for c in order[:-1]}
            m = jnp.full((1, 1), m0_ref[b_idx, layer, d, h_idx], F32)
            n = n0_ref[d, pl.ds(h_idx, 1), :]
            cm = c0_ref[d, hh]
            for pos, c in enumerate(order):
                m_pre[d][c] = m
                n_pre[d][c] = n
                cc_ref[hh, c, :, d * dqk:(d + 1) * dqk] = cm.astype(BF16)
                if pos == nc - 1:
                    break
                e_max, d_c, d_n = local[c]
                b_tot = rows[c][d][2]
                g_max = b_tot + e_max
                carried = b_tot + m
                m = jnp.maximum(carried, g_max)
                decay = jnp.exp(carried - m)
                up = jnp.exp(g_max - m)
                cm = decay * cm + up * d_c
                n = decay * n + up * d_n
    else:
        for d in range(2):
            e_max, d_c, d_n = local_update(0, d)
            b_tot = rows[0][d][2]
            g_max = b_tot + e_max
            m_pre[d][0] = jnp.zeros((1, 1), F32)
            m = jnp.maximum(b_tot, g_max)
            up = jnp.exp(g_max - m)
            c_new = up * d_c
            if alias_state:
                cfin_ref[d, hh] = c_new
            else:
                for lyr in range(cfin_ref.shape[0]):
                    cfin_ref[lyr, d, hh] = c_new if lyr == layer else jnp.zeros_like(c_new)
            nfin_ref[hh, pl.ds(d, 1), :] = up * d_n
            mfin_ref[hh, pl.ds(d, 1), :] = jnp.broadcast_to(m, (1, GATE_LANES))

    gain = jnp.concatenate([gain_ref[hh]] * (L // GATE_LANES), axis=1)
    for c in range(nc):
        q = q_ref[pl.ds(c * L, L), qc]
        k = k_ref[pl.ds(c * L, L), qc]
        v_t = vt_ref[vr, pl.ds(c * L, L)]
        qk_t = _dot_nt(k, q)
        if has_init:
            n_rows = jnp.concatenate([n_pre[0][c], n_pre[1][c], jnp.zeros((G - 2, dqk), F32)], axis=0)
            qn_t = _dot_nt(n_rows.astype(BF16), q)
        p_t = None
        scale_rows = []
        for d in range(2):
            _, b_row, _ = rows[c][d]
            em = jnp.where(masks[d], e_column(c, d) * LOG2E, -jnp.inf)
            b2 = b_row * LOG2E
            a2 = b2 + m_pre[d][c] * LOG2E
            mt2 = jnp.maximum(a2, b2 + jnp.max(em, axis=0, keepdims=True))
            s_t = qk_t * jnp.exp2((b2 - mt2) + em)
            den = jnp.sum(s_t, axis=0, keepdims=True)
            if has_init:
                inter = jnp.exp2(a2 - mt2)
                den = den + inter * qn_t[d:d + 1, :]
            r = 1.0 / jnp.maximum(jnp.abs(den), jnp.exp2(-mt2))
            p_t = s_t * r if p_t is None else p_t + s_t * r
            if has_init:
                scale_rows.append((inter * r).astype(BF16))
        h_t = _dot(v_t, p_t.astype(BF16))
        if has_init:
            q_t = q.T
            qs_t = jnp.concatenate([q_t * scale_rows[0], q_t * scale_rows[1]], axis=0)
            h_t = h_t + _dot(cc_ref[hh, c], qs_t)
        mu = jnp.mean(h_t, axis=0, keepdims=True)
        hc = h_t - mu
        var = jnp.mean(hc * hc, axis=0, keepdims=True)
        hn = hc * lax.rsqrt(var + LN_EPS) * gain
        out_ref[vr, pl.ds(c * L, L)] = hn.astype(BF16) * ogt_ref[vr, pl.ds(c * L, L)]


def _mlstm(proj, proj_t, gcol, grow, gain, seq, layer, depth, init=None, state_c=None):
    rows = proj.shape[0]
    n_seq = rows // seq
    dv = gain.shape[2]
    dm = N_HEADS * dv
    dqk = dv // 2
    has_init = init is not None
    hps = LAT_HEADS_PER_STEP if has_init else N_HEADS
    hb = N_HEADS // hps
    assert seq % MLSTM_CHUNK == 0 and (has_init or seq == MLSTM_CHUNK) and dqk % GATE_LANES == 0
    in_specs = [
        pl.BlockSpec((seq, hps * dqk), lambda b, h: (b, h)),
        pl.BlockSpec((seq, hps * dqk), lambda b, h: (b, hb + h)),
        pl.BlockSpec((hps * dv, seq), lambda b, h: (h, b)),
        pl.BlockSpec((hps * dv, seq), lambda b, h: (hb + h, b)),
        pl.BlockSpec((seq, GATE_LANES), lambda b, h: (b, 0)),
        pl.BlockSpec((4, GATE_STRIDE, seq), lambda b, h: (0, 0, b)),
        pl.BlockSpec((None, hps, dv, GATE_LANES), lambda b, h: (layer, h, 0, 0)),
    ]
    args = [proj, proj, proj_t, proj_t, gcol, grow, gain]
    out_specs = [pl.BlockSpec((hps * dv, seq), lambda b, h: (h, b))]
    out_shape = [jax.ShapeDtypeStruct((dm, rows), BF16)]
    scratch = []
    aliases = {}
    if has_init:
        init_c, init_n, init_m = init
        in_specs += [
            pl.BlockSpec((None, None, 2, hps, dv, dqk), lambda b, h: (b, layer, 0, h, 0, 0)),
            pl.BlockSpec((None, None, 2, N_HEADS, dqk), lambda b, h: (b, layer, 0, 0, 0)),
            pl.BlockSpec(memory_space=pltpu.MemorySpace.SMEM),
        ]
        args += [init_c, init_n, init_m]
        scratch = [pltpu.VMEM((hps, seq // MLSTM_CHUNK, dv, 2 * dqk), BF16)]
    else:
        if state_c is not None:
            in_specs.append(pl.BlockSpec(memory_space=pl.ANY))
            args.append(state_c)
            aliases = {len(args) - 1: 1}
            state_spec = pl.BlockSpec((None, None, 2, hps, dv, dqk), lambda b, h: (b, layer, 0, h, 0, 0))
        else:
            state_spec = pl.BlockSpec((None, depth, 2, hps, dv, dqk), lambda b, h: (b, 0, 0, h, 0, 0))
        out_specs += [
            state_spec,
            pl.BlockSpec((None, hps, 2, dqk), lambda b, h: (b, h, 0, 0)),
            pl.BlockSpec((None, hps, 2, GATE_LANES), lambda b, h: (b, h, 0, 0)),
        ]
        out_shape += [
            jax.ShapeDtypeStruct((n_seq, depth, 2, N_HEADS, dv, dqk), F32),
            jax.ShapeDtypeStruct((n_seq, N_HEADS, 2, dqk), F32),
            jax.ShapeDtypeStruct((n_seq, N_HEADS, 2, GATE_LANES), F32),
        ]
    return pl.pallas_call(
        functools.partial(_mlstm_kernel, seq=seq, has_init=has_init, layer=layer,
                          alias_state=state_c is not None, heads=hps),
        grid=(n_seq, N_HEADS // hps),
        in_specs=in_specs,
        out_specs=out_specs,
        out_shape=out_shape,
        scratch_shapes=scratch,
        input_output_aliases=aliases,
        compiler_params=_params("parallel", "parallel"),
        name="mlstm_lat" if has_init else "mlstm_ctx",
    )(*args)


def _fourier_kernel(u_ref, cc_ref, sc_ref, ms_ref, o_ref, ab_ref, *, seq, dg):
    for g in range(u_ref.shape[1] // dg):
        u = u_ref[:, g * dg:(g + 1) * dg]
        ab_ref[0:seq, g * dg:(g + 1) * dg] = _dot(u, cc_ref[...]).astype(BF16)
        ab_ref[seq:2 * seq, g * dg:(g + 1) * dg] = _dot(u, sc_ref[...]).astype(BF16)
    o_ref[...] = _dot(ms_ref[...], ab_ref[...]).astype(BF16)


def _dft_tables(seq, dg, grid_w):
    ch = np.arange(dg)
    ang_c = 2.0 * np.pi * ((np.outer(ch, ch) % dg) / dg)
    sc_c = 1.0 / np.sqrt(dg)
    cc = np.cos(ang_c) * sc_c
    sc = np.sin(ang_c) * sc_c
    t = np.arange(seq)
    if grid_w is None:
        frac = (np.outer(t, t) % seq) / seq
    else:
        rows = seq // grid_w
        r, w = t // grid_w, t % grid_w
        frac = (np.outer(r, r) % rows) / rows + (np.outer(w, w) % grid_w) / grid_w
    ang_s = 2.0 * np.pi * frac
    sc_s = 1.0 / np.sqrt(seq)
    ms = np.concatenate([np.cos(ang_s) * sc_s, -np.sin(ang_s) * sc_s], axis=1)
    return (jnp.asarray(cc, F32).astype(BF16), jnp.asarray(sc, F32).astype(BF16),
            jnp.asarray(ms, F32).astype(BF16))


def _fourier(proj, tables, seq, col_block0, df):
    rows = proj.shape[0]
    cc, sc, ms = tables
    dg = cc.shape[0]
    assert col_block0 % (df // dg) == 0
    return pl.pallas_call(
        functools.partial(_fourier_kernel, seq=seq, dg=dg),
        grid=(rows // seq,),
        in_specs=[
            pl.BlockSpec((seq, df), lambda b: (b, col_block0 * dg // df)),
            _resident((dg, dg), lambda b: (0, 0)),
            _resident((dg, dg), lambda b: (0, 0)),
            _resident((seq, 2 * seq), lambda b: (0, 0)),
        ],
        out_specs=pl.BlockSpec((seq, df), lambda b: (b, 0)),
        out_shape=jax.ShapeDtypeStruct((rows, df), BF16),
        scratch_shapes=[pltpu.VMEM((2 * seq, df), BF16)],
        compiler_params=_params("parallel"),
        name="fourier",
    )(proj, cc, sc, ms)


def _merge_kernel(hgt_ref, fy_ref, ga_ref, gb_ref, x_ref, mod_ref, wa_ref, wb_ref, wo_ref, lg_ref, lb_ref,
                  o_ref, *, d, alpha):
    ya = _dot_tn(hgt_ref[...], wa_ref[...])
    yb = _dot(fy_ref[...], wb_ref[...])
    merged = ga_ref[...].astype(F32) * ya + gb_ref[...].astype(F32) * yb
    out = _dot(merged.astype(BF16), wo_ref[...])
    g1 = mod_ref[...][:, 2 * d:3 * d]
    o_ref[...] = _layer_norm(alpha * x_ref[...] + g1 * out, lg_ref[...], lb_ref[...])


def _merge(hg_t, fy, proj, x, mod, layer, mod_row, wa, wb, wo, ln_g, ln_b, ga_block, alpha):
    rows, d = x.shape
    dm = hg_t.shape[0]
    df = fy.shape[1]
    tm = ROW_TILE
    wmap = lambda i: (layer, 0, 0)
    return pl.pallas_call(
        functools.partial(_merge_kernel, d=d, alpha=alpha),
        grid=(rows // tm,),
        in_specs=[
            pl.BlockSpec((dm, tm), lambda i: (0, i)),
            pl.BlockSpec((tm, df), lambda i: (i, 0)),
            pl.BlockSpec((tm, d), lambda i: (i, ga_block)),
            pl.BlockSpec((tm, d), lambda i: (i, ga_block + 1)),
            pl.BlockSpec((tm, d), lambda i: (i, 0)),
            pl.BlockSpec((None, None, 1, 6 * d), lambda i: (layer, mod_row(i, tm), 0, 0)),
            _resident((None, dm, d), wmap),
            _resident((None, df, d), wmap),
            _resident((None, d, d), wmap),
            _resident((None, None, 1, d), lambda i: (layer, 0, 0, 0)),
            _resident((None, None, 1, d), lambda i: (layer, 0, 0, 0)),
        ],
        out_specs=pl.BlockSpec((tm, d), lambda i: (i, 0)),
        out_shape=jax.ShapeDtypeStruct((rows, d), F32),
        compiler_params=_params("parallel"),
        name="merge",
    )(hg_t, fy, proj, proj, x, mod, wa, wb, wo, ln_g, ln_b)


def _ffn_kernel(x_ref, mod_ref, w1_ref, w2_ref, lg_ref, lb_ref, o_ref, h_ref, g_ref, *, d, alpha):
    dff = w2_ref.shape[0]
    m = mod_ref[...]
    x = x_ref[...]
    h_ref[...] = (x * (1.0 + m[:, 4 * d:5 * d]) + m[:, 3 * d:4 * d]).astype(BF16)
    tk = FFN_COLS
    for kk in range(dff // tk):
        a = _dot(h_ref[...], w1_ref[:, kk * tk:(kk + 1) * tk])
        u = _dot(h_ref[...], w1_ref[:, dff + kk * tk:dff + (kk + 1) * tk])
        g_ref[:, kk * tk:(kk + 1) * tk] = (a * jax.nn.sigmoid(a) * u).astype(BF16)
    f = _dot(g_ref[...], w2_ref[...])
    g2 = m[:, 5 * d:6 * d]
    o_ref[...] = _layer_norm(alpha * x + g2 * f, lg_ref[...], lb_ref[...])


def _ffn(x, mod, layer, mod_row, w1, w2, ln_g, ln_b, alpha):
    rows, d = x.shape
    dff = w2.shape[1]
    tm = ROW_TILE
    assert dff % FFN_COLS == 0
    return pl.pallas_call(
        functools.partial(_ffn_kernel, d=d, alpha=alpha),
        grid=(rows // tm,),
        in_specs=[
            pl.BlockSpec((tm, d), lambda i: (i, 0)),
            pl.BlockSpec((None, None, 1, 6 * d), lambda i: (layer, mod_row(i, tm), 0, 0)),
            _resident((None, d, 2 * dff), lambda i: (layer, 0, 0)),
            _resident((None, dff, d), lambda i: (layer, 0, 0)),
            _resident((None, None, 1, d), lambda i: (layer, 1, 0, 0)),
            _resident((None, None, 1, d), lambda i: (layer, 1, 0, 0)),
        ],
        out_specs=pl.BlockSpec((tm, d), lambda i: (i, 0)),
        out_shape=jax.ShapeDtypeStruct((rows, d), F32),
        scratch_shapes=[pltpu.VMEM((tm, d), BF16), pltpu.VMEM((tm, dff), BF16)],
        compiler_params=_params("parallel"),
        name="ffn",
    )(x, mod, w1, w2, ln_g, ln_b)


def kernel(x_prompt, x_sample, c, state_C, state_n, state_m, c_ctx, w_mod, b_mod, w_in, b_gate, mh_gain,
           w_branch_a, w_branch_b, w_out, ln_gain, ln_bias, w_ffn_in, w_ffn_out):
    batch, seq_ctx, d = x_prompt.shape
    dec_batch, seq_lat, _ = x_sample.shape
    depth = w_in.shape[0]
    dm = w_branch_a.shape[1]
    df = w_branch_b.shape[1]
    dv = dm // N_HEADS
    dqk = dv // 2
    dg = df // N_FGROUPS
    n_gates = 4 * N_HEADS
    q_end = N_HEADS * dqk
    k_end = 2 * q_end
    v_end = k_end + dm
    o_end = v_end + dm
    g_end = o_end + n_gates
    alpha = float((2 * depth) ** 0.25)

    mod_rows = 16
    cvec = jnp.zeros((mod_rows, d), F32).at[0].set(c_ctx).at[1:1 + dec_batch].set(c)
    mod = _modulation(cvec, w_mod, b_mod).reshape(depth, mod_rows, 1, 6 * d)

    w_in_t = jnp.swapaxes(w_in, 1, 2)
    w_qk, w_vo, w_tail = _split_projection_weights(w_in_t, k_end, o_end, g_end)
    gate_pad = ((0, 0), (0, 0), (0, GATE_STRIDE - N_HEADS), (0, 0))
    lane_pad = ((0, 0), (0, 0), (0, GATE_LANES - 4 * GATE_STRIDE))
    wg = jnp.pad(w_in_t[:, o_end:g_end, :].reshape(depth, 4, N_HEADS, d), gate_pad)
    wg = jnp.pad(jnp.swapaxes(wg.reshape(depth, 4 * GATE_STRIDE, d), 1, 2), lane_pad).astype(BF16)
    bg = jnp.pad(b_gate.reshape(depth, 4, N_HEADS, 1), gate_pad).reshape(depth, 1, 4 * GATE_STRIDE)
    bg = jnp.pad(bg, lane_pad)
    wa = w_branch_a.astype(BF16)
    wb = w_branch_b.astype(BF16)
    wo = w_out.astype(BF16)
    w1 = w_ffn_in.astype(BF16)
    w2 = w_ffn_out.astype(BF16)
    gain = jnp.broadcast_to(mh_gain.reshape(depth, N_HEADS, dv, 1), (depth, N_HEADS, dv, GATE_LANES))
    ln_g = ln_gain.reshape(depth, 2, 1, d)
    ln_b = ln_bias.reshape(depth, 2, 1, d)

    def slab_fn(col):
        if q_end <= col < k_end:
            return lambda t: t * (dqk ** -0.5)
        if v_end <= col < o_end or col >= o_end + df:
            return jax.nn.sigmoid
        return lambda t: t
    n_main = o_end + w_tail.shape[2]
    n_vo = o_end - k_end
    assert all(e % PROJ_COLS == 0 for e in (q_end, k_end, v_end, o_end, o_end + df, n_main))
    epilogues = tuple(slab_fn(j * PROJ_COLS) for j in range(n_main // PROJ_COLS))

    tables_ctx = _dft_tables(seq_ctx, dg, None)
    tables_lat = _dft_tables(seq_lat, dg, GRID_W)
    f_block0 = (o_end - n_vo) // dg
    ga_block = (o_end - n_vo + df) // d

    def ctx_row(i, tm):
        return 0

    def lat_row(i, tm):
        return 1 + (i * tm) // seq_lat

    def layer(x, l, seq, mod_row, tables, init, state_c):
        proj, proj_t, gcol, grow = _inproj(x, mod, l, mod_row, w_qk, w_vo, w_tail, wg, bg, epilogues)
        res = _mlstm(proj, proj_t, gcol, grow, gain, seq, l, depth, init, state_c)
        fy = _fourier(proj, tables, seq, f_block0, df)
        x = _merge(res[0], fy, proj, x, mod, l, mod_row, wa, wb, wo, ln_g, ln_b, ga_block, alpha)
        x = _ffn(x, mod, l, mod_row, w1, w2, ln_g, ln_b, alpha)
        return x, res[1:]

    xp = x_prompt.reshape(batch * seq_ctx, d)
    xs = x_sample.reshape(dec_batch * seq_lat, d)
    new_c, new_n, new_m = None, [], []
    for l in range(depth):
        xp, (new_c, nfin, mfin) = layer(xp, l, seq_ctx, ctx_row, tables_ctx, None, new_c)
        new_n.append(nfin.transpose(0, 2, 1, 3))
        new_m.append(mfin[..., 0].transpose(0, 2, 1))
        xs, _ = layer(xs, l, seq_lat, lat_row, tables_lat, (state_C, state_n, state_m), None)
    return (xp.reshape(batch, seq_ctx, d), xs.reshape(dec_batch, seq_lat, d),
            new_c, jnp.stack(new_n, axis=1), jnp.stack(new_m, axis=1))
```

```python
import functools
import math

import numpy as np
import jax
import jax.numpy as jnp
from jax import lax
from jax.experimental import pallas as pl
from jax.experimental.pallas import tpu as pltpu

F32 = jnp.float32
BF16 = jnp.bfloat16

N_HEADS = 4
N_FGROUPS = 4
GRID_W = 64
LN_EPS = 1e-5
MLSTM_CHUNK = 256
GATE_LANES = 128
GATE_STRIDE = 8
ROW_TILE = 512
FFN_ROW_TILE = 1024
PROJ_COLS = 1024
FFN_COLS = 256
LAT_HEADS_PER_STEP = 2
MERGE_PARTS = 2
VMEM_LIMIT = 56 * 1024 * 1024
LOG2E = math.log2(math.e)

NT_DIMS = (((1,), (1,)), ((), ()))
TN_DIMS = (((0,), (0,)), ((), ()))


def _dot(a, b):
    return jnp.dot(a, b, preferred_element_type=F32)


def _dot_nt(a, b):
    return lax.dot_general(a, b, NT_DIMS, preferred_element_type=F32)


def _dot_tn(a, b):
    return lax.dot_general(a, b, TN_DIMS, preferred_element_type=F32)


def _split3(x):
    hi = x.astype(BF16)
    r = x - hi.astype(F32)
    mid = r.astype(BF16)
    lo = (r - mid.astype(F32)).astype(BF16)
    return hi, mid, lo


def _layer_norm(y, g, b):
    mu = jnp.mean(y, axis=-1, keepdims=True)
    yc = y - mu
    var = jnp.mean(yc * yc, axis=-1, keepdims=True)
    return yc * lax.rsqrt(var + LN_EPS) * g + b


def _log_sigmoid(x):
    return jnp.minimum(x, 0.0) - jnp.log(1.0 + jnp.exp(-jnp.abs(x)))


def _params(*sem):
    return pltpu.CompilerParams(dimension_semantics=sem, vmem_limit_bytes=VMEM_LIMIT)


def _resident(shape, index_map):
    return pl.BlockSpec(shape, index_map, pipeline_mode=pl.Buffered(1))


def _mod_kernel(c_ref, w_ref, b_ref, o_ref):
    c = c_ref[...]
    s = (c * jax.nn.sigmoid(c)).astype(BF16)
    o_ref[...] = _dot(s, w_ref[...].astype(BF16)) + b_ref[...]


def _modulation(cvec, w_mod, b_mod):
    depth, d, n6 = w_mod.shape
    rows = cvec.shape[0]
    tn = 1536
    return pl.pallas_call(
        _mod_kernel,
        grid=(depth, n6 // tn),
        in_specs=[
            pl.BlockSpec((rows, d), lambda l, j: (0, 0)),
            pl.BlockSpec((None, d, tn), lambda l, j: (l, 0, j)),
            pl.BlockSpec((None, 1, tn), lambda l, j: (l, 0, j)),
        ],
        out_specs=pl.BlockSpec((None, rows, tn), lambda l, j: (l, 0, j)),
        out_shape=jax.ShapeDtypeStruct((depth, rows, n6), F32),
        compiler_params=_params("parallel", "parallel"),
        name="modulation",
    )(cvec, w_mod, b_mod.reshape(depth, 1, n6))


def _cast_kernel(w_ref, o_ref):
    o_ref[...] = w_ref[...].astype(BF16)


def _transpose_cast_kernel(w_ref, o_ref):
    o_ref[...] = w_ref[...].T.astype(BF16)


def _shift_cast_kernel(a_ref, b_ref, o_ref, *, shift):
    o_ref[...] = jnp.concatenate([a_ref[shift:, :], b_ref[...]], axis=0).T.astype(BF16)


def _split_projection_weights(w_in_t, k_end, o_end, g_end):
    depth, n_in, d = w_in_t.shape
    tn = PROJ_COLS
    shift = g_end - o_end
    n_tail = n_in - g_end
    assert k_end % tn == 0 and o_end % tn == 0 and n_tail % tn == 0 and shift % 8 == 0 and tn % shift == 0
    w_qk = pl.pallas_call(
        _transpose_cast_kernel,
        grid=(depth, k_end // tn),
        in_specs=[pl.BlockSpec((None, tn, d), lambda l, j: (l, j, 0))],
        out_specs=pl.BlockSpec((None, d, tn), lambda l, j: (l, 0, j)),
        out_shape=jax.ShapeDtypeStruct((depth, d, k_end), BF16),
        compiler_params=_params("parallel", "parallel"),
        name="w_qk_cast",
    )(w_in_t)
    kb = k_end // tn
    w_vo = pl.pallas_call(
        _cast_kernel,
        grid=(depth, (o_end - k_end) // tn),
        in_specs=[pl.BlockSpec((None, tn, d), lambda l, j: (l, kb + j, 0))],
        out_specs=pl.BlockSpec((None, tn, d), lambda l, j: (l, j, 0)),
        out_shape=jax.ShapeDtypeStruct((depth, o_end - k_end, d), BF16),
        compiler_params=_params("parallel", "parallel"),
        name="w_vo_cast",
    )(w_in_t)
    base = o_end // tn
    per_slab = tn // shift
    tail = pl.pallas_call(
        functools.partial(_shift_cast_kernel, shift=shift),
        grid=(depth, n_tail // tn),
        in_specs=[
            pl.BlockSpec((None, tn, d), lambda l, j: (l, base + j, 0)),
            pl.BlockSpec((None, shift, d), lambda l, j: (l, (base + j + 1) * per_slab, 0)),
        ],
        out_specs=pl.BlockSpec((None, d, tn), lambda l, j: (l, 0, j)),
        out_shape=jax.ShapeDtypeStruct((depth, d, n_tail), BF16),
        compiler_params=_params("parallel", "parallel"),
        name="w_tail_cast",
    )(w_in_t, w_in_t)
    return w_qk, w_vo, tail


def _inproj_kernel(x_ref, mod_ref, wqk_ref, wvo_ref, wt_ref, wg_ref, bg_ref, proj_ref, proj_t_ref, gcol_ref,
                   grow_ref, h_ref, *, d, epilogues):
    tm = x_ref.shape[0]
    L = MLSTM_CHUNK
    G = GATE_STRIDE
    m = mod_ref[...]
    hb = (x_ref[...] * (1.0 + m[:, d:2 * d]) + m[:, 0:d]).astype(BF16)
    h_ref[...] = hb

    n_rows = 4 * G
    g = _dot(hb, wg_ref[...]) + bg_ref[...]
    lane = lax.broadcasted_iota(jnp.int32, g.shape, 1)
    g = jnp.where(((lane >> 3) & 1) == 1, _log_sigmoid(g), g)
    gt = g.T[0:n_rows, :]
    r_i = lax.broadcasted_iota(jnp.int32, (L, L), 0)
    c_i = lax.broadcasted_iota(jnp.int32, (L, L), 1)
    upper_b = (r_i <= c_i).astype(BF16)
    lower_b = (r_i >= c_i).astype(BF16)

    def cumsum(x, tri):
        hi, mid, lo = _split3(x)
        return _dot(hi, tri) + _dot(mid, tri) + _dot(lo, tri)

    pieces = []
    for c in range(tm // L):
        xg = gt[:, c * L:(c + 1) * L]
        pre = cumsum(xg[G:2 * G], upper_b)
        suf = cumsum(xg[3 * G:4 * G], lower_b)
        pieces.append(jnp.concatenate([xg[0:G] - pre, pre, xg[2 * G:3 * G] - suf, suf], axis=0))
    gs = jnp.concatenate(pieces, axis=1)
    for qq in range(4):
        grow_ref[qq] = gs[qq * G:(qq + 1) * G, :]
    gcol_ref[...] = jnp.concatenate([gs, jnp.zeros((GATE_LANES - n_rows, tm), F32)], axis=0).T

    tn = PROJ_COLS
    n_qk = wqk_ref.shape[1] // tn
    n_vo = wvo_ref.shape[0] // tn
    for jn, fn in enumerate(epilogues):
        if jn < n_qk:
            proj_ref[:, jn * tn:(jn + 1) * tn] = fn(_dot(h_ref[...], wqk_ref[:, jn * tn:(jn + 1) * tn])).astype(BF16)
        elif jn < n_qk + n_vo:
            jv = jn - n_qk
            proj_t_ref[jv * tn:(jv + 1) * tn, :] = fn(_dot_nt(wvo_ref[jv * tn:(jv + 1) * tn, :], h_ref[...])).astype(BF16)
        else:
            jt = jn - n_qk - n_vo
            jo = jn - n_vo
            proj_ref[:, jo * tn:(jo + 1) * tn] = fn(_dot(h_ref[...], wt_ref[:, jt * tn:(jt + 1) * tn])).astype(BF16)


def _inproj(x, mod, layer, mod_row, w_qk, w_vo, w_tail, w_gate, b_gate, epilogues):
    rows, d = x.shape
    n_qk, n_vo, n_t = w_qk.shape[2], w_vo.shape[1], w_tail.shape[2]
    n = n_qk + n_t
    tm = ROW_TILE
    assert n + n_vo == PROJ_COLS * len(epilogues) and rows % tm == 0 and tm % MLSTM_CHUNK == 0
    return pl.pallas_call(
        functools.partial(_inproj_kernel, d=d, epilogues=epilogues),
        grid=(rows // tm,),
        in_specs=[
            pl.BlockSpec((tm, d), lambda i: (i, 0)),
            pl.BlockSpec((None, None, 1, 6 * d), lambda i: (layer, mod_row(i, tm), 0, 0)),
            _resident((None, d, n_qk), lambda i: (layer, 0, 0)),
            _resident((None, n_vo, d), lambda i: (layer, 0, 0)),
            _resident((None, d, n_t), lambda i: (layer, 0, 0)),
            _resident((None, d, GATE_LANES), lambda i: (layer, 0, 0)),
            _resident((None, 1, GATE_LANES), lambda i: (layer, 0, 0)),
        ],
        out_specs=[
            pl.BlockSpec((tm, n), lambda i: (i, 0)),
            pl.BlockSpec((n_vo, tm), lambda i: (0, i)),
            pl.BlockSpec((tm, GATE_LANES), lambda i: (i, 0)),
            pl.BlockSpec((4, GATE_STRIDE, tm), lambda i: (0, 0, i)),
        ],
        out_shape=[
            jax.ShapeDtypeStruct((rows, n), BF16),
            jax.ShapeDtypeStruct((n_vo, rows), BF16),
            jax.ShapeDtypeStruct((rows, GATE_LANES), F32),
            jax.ShapeDtypeStruct((4, GATE_STRIDE, rows), F32),
        ],
        scratch_shapes=[pltpu.VMEM((tm, d), BF16)],
        compiler_params=_params("parallel"),
        name="inproj",
    )(x, mod, w_qk, w_vo, w_tail, w_gate, b_gate)


def _mlstm_kernel(*refs, heads, n_in, n_out, seq, dg, **static):
    u_ref, fcc_ref, fsc_ref, fms_ref = refs[:4]
    fy_ref = refs[n_in]
    ab_ref = refs[n_in + n_out]
    mixer_refs = refs[4:n_in] + refs[n_in + 1:n_in + n_out] + refs[n_in + n_out + 1:]
    for hh in range(heads):
        _mlstm_head(hh, *mixer_refs, heads=heads, seq=seq, **static)
    for g in range(u_ref.shape[1] // dg):
        u = u_ref[:, g * dg:(g + 1) * dg]
        ab_ref[0:seq, g * dg:(g + 1) * dg] = _dot(u, fcc_ref[...]).astype(BF16)
        ab_ref[seq:2 * seq, g * dg:(g + 1) * dg] = _dot(u, fsc_ref[...]).astype(BF16)
    fy_ref[...] = _dot(fms_ref[...], ab_ref[...]).astype(BF16)


def _mlstm_head(hh, *refs, seq, has_init, layer, alias_state, heads):
    if has_init:
        (q_ref, k_ref, vt_ref, ogt_ref, gcol_ref, grow_ref, gain_ref, c0_ref, n0_ref, m0_ref,
         out_ref, cc_ref) = refs
    elif alias_state:
        (q_ref, k_ref, vt_ref, ogt_ref, gcol_ref, grow_ref, gain_ref, _,
         out_ref, cfin_ref, nfin_ref, mfin_ref) = refs
    else:
        (q_ref, k_ref, vt_ref, ogt_ref, gcol_ref, grow_ref, gain_ref,
         out_ref, cfin_ref, nfin_ref, mfin_ref) = refs
    L = MLSTM_CHUNK
    G = GATE_STRIDE
    nc = seq // L
    b_idx = pl.program_id(0)
    h_idx = pl.program_id(1) * heads + hh
    dqk = q_ref.shape[1] // heads
    dv = vt_ref.shape[0] // heads
    qc = slice(hh * dqk, (hh + 1) * dqk)
    vr = slice(hh * dv, (hh + 1) * dv)

    s_i = lax.broadcasted_iota(jnp.int32, (L, L), 0)
    t_i = lax.broadcasted_iota(jnp.int32, (L, L), 1)
    masks = (s_i <= t_i, s_i >= t_i)
    lane = lax.broadcasted_iota(jnp.int32, (L, GATE_LANES), 1)

    def gate_rows(c):
        sl = pl.ds(c * L, L)
        hs = pl.ds(h_idx, 1)
        b_f = grow_ref[1, hs, sl]
        b_b = grow_ref[3, hs, sl]
        return ((grow_ref[0, hs, sl], b_f, b_f[:, L - 1:L]), (grow_ref[2, hs, sl], b_b, b_b[:, 0:1]))

    def e_column(c, d):
        gc = gcol_ref[pl.ds(c * L, L), :]
        return jnp.sum(jnp.where(lane == 2 * d * G + h_idx, gc, 0.0), axis=1, keepdims=True)

    rows = [gate_rows(c) for c in range(nc)]

    def stabiliser_step(c, d, m_prev):
        e_row, _, b_tot = rows[c][d]
        carried = b_tot + m_prev
        m_new = jnp.maximum(carried, b_tot + jnp.max(e_row, axis=1, keepdims=True))
        return m_new, jnp.exp(carried - m_new)

    def state_contribution(c, d, m_new):
        e_row, _, b_tot = rows[c][d]
        w_row = jnp.exp((e_row + b_tot) - m_new).astype(BF16)
        k = k_ref[pl.ds(c * L, L), qc]
        d_c = _dot(vt_ref[vr, pl.ds(c * L, L)] * w_row, k)
        d_n = _dot(jnp.broadcast_to(w_row, (G, L)), k)[0:1, :]
        return d_c, d_n

    m_pre = [[None] * nc for _ in range(2)]
    n_pre = [[None] * nc for _ in range(2)]
    if has_init:
        for d in range(2):
            order = list(range(nc)) if d == 0 else list(range(nc - 1, -1, -1))
            m = jnp.full((1, 1), m0_ref[b_idx, layer, d, h_idx], F32)
            decays = {}
            for c in order:
                m_pre[d][c] = m
                if c != order[-1]:
                    m, decays[c] = stabiliser_step(c, d, m)
            local = {c: state_contribution(c, d, m_pre[d][nxt]) for c, nxt in zip(order[:-1], order[1:])}
            n = n0_ref[d, pl.ds(h_idx, 1), :]
            cm = c0_ref[d, hh]
            for pos, c in enumerate(order):
                n_pre[d][c] = n
                cc_ref[hh, c, :, d * dqk:(d + 1) * dqk] = cm.astype(BF16)
                if pos == nc - 1:
                    break
                d_c, d_n = local[c]
                cm = decays[c] * cm + d_c
                n = decays[c] * n + d_n
    else:
        for d in range(2):
            m_pre[d][0] = jnp.zeros((1, 1), F32)
            m, _ = stabiliser_step(0, d, m_pre[d][0])
            c_new, n_new = state_contribution(0, d, m)
            if alias_state:
                cfin_ref[d, hh] = c_new
            else:
                for lyr in range(cfin_ref.shape[0]):
                    cfin_ref[lyr, d, hh] = c_new if lyr == layer else jnp.zeros_like(c_new)
            nfin_ref[hh, pl.ds(d, 1), :] = n_new
            mfin_ref[hh, pl.ds(d, 1), :] = jnp.broadcast_to(m, (1, GATE_LANES))

    gain = jnp.concatenate([gain_ref[hh]] * (L // GATE_LANES), axis=1)
    for c in range(nc):
        q = q_ref[pl.ds(c * L, L), qc]
        k = k_ref[pl.ds(c * L, L), qc]
        v_t = vt_ref[vr, pl.ds(c * L, L)]
        qk_t = _dot_nt(k, q)
        if has_init:
            n_rows = jnp.concatenate([n_pre[0][c], n_pre[1][c], jnp.zeros((G - 2, dqk), F32)], axis=0)
            qn_t = _dot_nt(n_rows.astype(BF16), q)
        p_t = None
        scale_rows = []
        for d in range(2):
            _, b_row, _ = rows[c][d]
            em = jnp.where(masks[d], e_column(c, d) * LOG2E, -jnp.inf)
            b2 = b_row * LOG2E
            a2 = b2 + m_pre[d][c] * LOG2E
            mt2 = jnp.maximum(a2, b2 + jnp.max(em, axis=0, keepdims=True))
            s_t = qk_t * jnp.exp2((b2 - mt2) + em)
            den = jnp.sum(s_t, axis=0, keepdims=True)
            if has_init:
                inter = jnp.exp2(a2 - mt2)
                den = den + inter * qn_t[d:d + 1, :]
            r = 1.0 / jnp.maximum(jnp.abs(den), jnp.exp2(-mt2))
            p_t = s_t * r if p_t is None else p_t + s_t * r
            if has_init:
                scale_rows.append((inter * r).astype(BF16))
        h_t = _dot(v_t, p_t.astype(BF16))
        if has_init:
            q_t = q.T
            qs_t = jnp.concatenate([q_t * scale_rows[0], q_t * scale_rows[1]], axis=0)
            h_t = h_t + _dot(cc_ref[hh, c], qs_t)
        mu = jnp.mean(h_t, axis=0, keepdims=True)
        hc = h_t - mu
        var = jnp.mean(hc * hc, axis=0, keepdims=True)
        hn = hc * lax.rsqrt(var + LN_EPS) * gain
        out_ref[vr, pl.ds(c * L, L)] = hn.astype(BF16) * ogt_ref[vr, pl.ds(c * L, L)]


def _mixers(proj, proj_t, gcol, grow, gain, tables, f_block0, df, seq, layer, depth, init=None, state_c=None):
    rows = proj.shape[0]
    fcc, fsc, fms = tables
    dg = fcc.shape[0]
    n_seq = rows // seq
    dv = gain.shape[2]
    dm = N_HEADS * dv
    dqk = dv // 2
    has_init = init is not None
    hps = LAT_HEADS_PER_STEP if has_init else N_HEADS
    hb = N_HEADS // hps
    assert seq % MLSTM_CHUNK == 0 and (has_init or seq == MLSTM_CHUNK) and dqk % GATE_LANES == 0
    assert df // dg == N_HEADS and f_block0 % hps == 0
    in_specs = [
        pl.BlockSpec((seq, hps * dg), lambda b, h: (b, f_block0 // hps + h)),
        _resident((dg, dg), lambda b, h: (0, 0)),
        _resident((dg, dg), lambda b, h: (0, 0)),
        _resident((seq, 2 * seq), lambda b, h: (0, 0)),
        pl.BlockSpec((seq, hps * dqk), lambda b, h: (b, h)),
        pl.BlockSpec((seq, hps * dqk), lambda b, h: (b, hb + h)),
        pl.BlockSpec((hps * dv, seq), lambda b, h: (h, b)),
        pl.BlockSpec((hps * dv, seq), lambda b, h: (hb + h, b)),
        pl.BlockSpec((seq, GATE_LANES), lambda b, h: (b, 0)),
        pl.BlockSpec((4, GATE_STRIDE, seq), lambda b, h: (0, 0, b)),
        pl.BlockSpec((None, hps, dv, GATE_LANES), lambda b, h: (layer, h, 0, 0)),
    ]
    args = [proj, fcc, fsc, fms, proj, proj, proj_t, proj_t, gcol, grow, gain]
    out_specs = [pl.BlockSpec((seq, hps * dg), lambda b, h: (b, h)),
                 pl.BlockSpec((hps * dv, seq), lambda b, h: (h, b))]
    out_shape = [jax.ShapeDtypeStruct((rows, df), BF16), jax.ShapeDtypeStruct((dm, rows), BF16)]
    scratch = [pltpu.VMEM((2 * seq, hps * dg), BF16)]
    aliases = {}
    if has_init:
        init_c, init_n, init_m = init
        in_specs += [
            pl.BlockSpec((None, None, 2, hps, dv, dqk), lambda b, h: (b, layer, 0, h, 0, 0)),
            pl.BlockSpec((None, None, 2, N_HEADS, dqk), lambda b, h: (b, layer, 0, 0, 0)),
            pl.BlockSpec(memory_space=pltpu.MemorySpace.SMEM),
        ]
        args += [init_c, init_n, init_m]
        scratch += [pltpu.VMEM((hps, seq // MLSTM_CHUNK, dv, 2 * dqk), BF16)]
    else:
        if state_c is not None:
            in_specs.append(pl.BlockSpec(memory_space=pl.ANY))
            args.append(state_c)
            aliases = {len(args) - 1: 2}
            state_spec = pl.BlockSpec((None, None, 2, hps, dv, dqk), lambda b, h: (b, layer, 0, h, 0, 0))
        else:
            state_spec = pl.BlockSpec((None, depth, 2, hps, dv, dqk), lambda b, h: (b, 0, 0, h, 0, 0))
        out_specs += [
            state_spec,
            pl.BlockSpec((None, hps, 2, dqk), lambda b, h: (b, h, 0, 0)),
            pl.BlockSpec((None, hps, 2, GATE_LANES), lambda b, h: (b, h, 0, 0)),
        ]
        out_shape += [
            jax.ShapeDtypeStruct((n_seq, depth, 2, N_HEADS, dv, dqk), F32),
            jax.ShapeDtypeStruct((n_seq, N_HEADS, 2, dqk), F32),
            jax.ShapeDtypeStruct((n_seq, N_HEADS, 2, GATE_LANES), F32),
        ]
    return pl.pallas_call(
        functools.partial(_mlstm_kernel, seq=seq, has_init=has_init, layer=layer, alias_state=state_c is not None,
                          heads=hps, n_in=len(args), n_out=len(out_specs), dg=dg),
        grid=(n_seq, N_HEADS // hps),
        in_specs=in_specs,
        out_specs=out_specs,
        out_shape=out_shape,
        scratch_shapes=scratch,
        input_output_aliases=aliases,
        compiler_params=_params("parallel", "parallel"),
        name="mixers_lat" if has_init else "mixers_ctx",
    )(*args)


def _dft_tables(seq, dg, grid_w):
    ch = np.arange(dg)
    ang_c = 2.0 * np.pi * ((np.outer(ch, ch) % dg) / dg)
    sc_c = 1.0 / np.sqrt(dg)
    cc = np.cos(ang_c) * sc_c
    sc = np.sin(ang_c) * sc_c
    t = np.arange(seq)
    if grid_w is None:
        frac = (np.outer(t, t) % seq) / seq
    else:
        rows = seq // grid_w
        r, w = t // grid_w, t % grid_w
        frac = (np.outer(r, r) % rows) / rows + (np.outer(w, w) % grid_w) / grid_w
    ang_s = 2.0 * np.pi * frac
    sc_s = 1.0 / np.sqrt(seq)
    ms = np.concatenate([np.cos(ang_s) * sc_s, -np.sin(ang_s) * sc_s], axis=1)
    return (jnp.asarray(cc, F32).astype(BF16), jnp.asarray(sc, F32).astype(BF16),
            jnp.asarray(ms, F32).astype(BF16))


def _merge_kernel(hgt_ref, fy_ref, ga_ref, gb_ref, x_ref, mod_ref, wa_ref, wb_ref, wo_ref, lg_ref, lb_ref,
                  o_ref, *, d, alpha):
    g1 = mod_ref[...][:, 2 * d:3 * d]
    tm = x_ref.shape[0]
    for part in range(MERGE_PARTS):
        r = slice(part * tm // MERGE_PARTS, (part + 1) * tm // MERGE_PARTS)
        ya = _dot_tn(hgt_ref[:, r], wa_ref[...])
        yb = _dot(fy_ref[r, :], wb_ref[...])
        merged = ga_ref[r, :].astype(F32) * ya + gb_ref[r, :].astype(F32) * yb
        out = _dot(merged.astype(BF16), wo_ref[...])
        o_ref[r, :] = _layer_norm(alpha * x_ref[r, :] + g1 * out, lg_ref[...], lb_ref[...])


def _merge(hg_t, fy, proj, x, mod, layer, mod_row, wa, wb, wo, ln_g, ln_b, ga_block, alpha):
    rows, d = x.shape
    dm = hg_t.shape[0]
    df = fy.shape[1]
    tm = ROW_TILE
    wmap = lambda i: (layer, 0, 0)
    return pl.pallas_call(
        functools.partial(_merge_kernel, d=d, alpha=alpha),
        grid=(rows // tm,),
        in_specs=[
            pl.BlockSpec((dm, tm), lambda i: (0, i)),
            pl.BlockSpec((tm, df), lambda i: (i, 0)),
            pl.BlockSpec((tm, d), lambda i: (i, ga_block)),
            pl.BlockSpec((tm, d), lambda i: (i, ga_block + 1)),
            pl.BlockSpec((tm, d), lambda i: (i, 0)),
            pl.BlockSpec((None, None, 1, 6 * d), lambda i: (layer, mod_row(i, tm), 0, 0)),
            _resident((None, dm, d), wmap),
            _resident((None, df, d), wmap),
            _resident((None, d, d), wmap),
            _resident((None, None, 1, d), lambda i: (layer, 0, 0, 0)),
            _resident((None, None, 1, d), lambda i: (layer, 0, 0, 0)),
        ],
        out_specs=pl.BlockSpec((tm, d), lambda i: (i, 0)),
        out_shape=jax.ShapeDtypeStruct((rows, d), F32),
        compiler_params=_params("parallel"),
        name="merge",
    )(hg_t, fy, proj, proj, x, mod, wa, wb, wo, ln_g, ln_b)


def _ffn_kernel(x_ref, mod_ref, w1_ref, w2_ref, lg_ref, lb_ref, o_ref, h_ref, g_ref, *, d, alpha):
    dff = w2_ref.shape[0]
    m = mod_ref[...]
    x = x_ref[...]
    h_ref[...] = (x * (1.0 + m[:, 4 * d:5 * d]) + m[:, 3 * d:4 * d]).astype(BF16)
    tk = FFN_COLS
    for kk in range(dff // tk):
        a = _dot(h_ref[...], w1_ref[:, kk * tk:(kk + 1) * tk])
        u = _dot(h_ref[...], w1_ref[:, dff + kk * tk:dff + (kk + 1) * tk])
        g_ref[:, kk * tk:(kk + 1) * tk] = (a * jax.nn.sigmoid(a) * u).astype(BF16)
    f = _dot(g_ref[...], w2_ref[...])
    g2 = m[:, 5 * d:6 * d]
    o_ref[...] = _layer_norm(alpha * x + g2 * f, lg_ref[...], lb_ref[...])


def _ffn(x, mod, layer, mod_row, w1, w2, ln_g, ln_b, alpha):
    rows, d = x.shape
    dff = w2.shape[1]
    tm = FFN_ROW_TILE
    assert dff % FFN_COLS == 0
    return pl.pallas_call(
        functools.partial(_ffn_kernel, d=d, alpha=alpha),
        grid=(rows // tm,),
        in_specs=[
            pl.BlockSpec((tm, d), lambda i: (i, 0)),
            pl.BlockSpec((None, None, 1, 6 * d), lambda i: (layer, mod_row(i, tm), 0, 0)),
            _resident((None, d, 2 * dff), lambda i: (layer, 0, 0)),
            _resident((None, dff, d), lambda i: (layer, 0, 0)),
            _resident((None, None, 1, d), lambda i: (layer, 1, 0, 0)),
            _resident((None, None, 1, d), lambda i: (layer, 1, 0, 0)),
        ],
        out_specs=pl.BlockSpec((tm, d), lambda i: (i, 0)),
        out_shape=jax.ShapeDtypeStruct((rows, d), F32),
        scratch_shapes=[pltpu.VMEM((tm, d), BF16), pltpu.VMEM((tm, dff), BF16)],
        compiler_params=_params("parallel"),
        name="ffn",
    )(x, mod, w1, w2, ln_g, ln_b)


def kernel(x_prompt, x_sample, c, state_C, state_n, state_m, c_ctx, w_mod, b_mod, w_in, b_gate, mh_gain,
           w_branch_a, w_branch_b, w_out, ln_gain, ln_bias, w_ffn_in, w_ffn_out):
    batch, seq_ctx, d = x_prompt.shape
    dec_batch, seq_lat, _ = x_sample.shape
    depth = w_in.shape[0]
    dm = w_branch_a.shape[1]
    df = w_branch_b.shape[1]
    dv = dm // N_HEADS
    dqk = dv // 2
    dg = df // N_FGROUPS
    n_gates = 4 * N_HEADS
    q_end = N_HEADS * dqk
    k_end = 2 * q_end
    v_end = k_end + dm
    o_end = v_end + dm
    g_end = o_end + n_gates
    alpha = float((2 * depth) ** 0.25)

    mod_rows = 16
    cvec = jnp.zeros((mod_rows, d), F32).at[0].set(c_ctx).at[1:1 + dec_batch].set(c)
    mod = _modulation(cvec, w_mod, b_mod).reshape(depth, mod_rows, 1, 6 * d)

    w_in_t = jnp.swapaxes(w_in, 1, 2)
    w_qk, w_vo, w_tail = _split_projection_weights(w_in_t, k_end, o_end, g_end)
    gate_pad = ((0, 0), (0, 0), (0, GATE_STRIDE - N_HEADS), (0, 0))
    lane_pad = ((0, 0), (0, 0), (0, GATE_LANES - 4 * GATE_STRIDE))
    wg = jnp.pad(w_in_t[:, o_end:g_end, :].reshape(depth, 4, N_HEADS, d), gate_pad)
    wg = jnp.pad(jnp.swapaxes(wg.reshape(depth, 4 * GATE_STRIDE, d), 1, 2), lane_pad).astype(BF16)
    bg = jnp.pad(b_gate.reshape(depth, 4, N_HEADS, 1), gate_pad).reshape(depth, 1, 4 * GATE_STRIDE)
    bg = jnp.pad(bg, lane_pad)
    wa = w_branch_a.astype(BF16)
    wb = w_branch_b.astype(BF16)
    wo = w_out.astype(BF16)
    w1 = w_ffn_in.astype(BF16)
    w2 = w_ffn_out.astype(BF16)
    gain = jnp.broadcast_to(mh_gain.reshape(depth, N_HEADS, dv, 1), (depth, N_HEADS, dv, GATE_LANES))
    ln_g = ln_gain.reshape(depth, 2, 1, d)
    ln_b = ln_bias.reshape(depth, 2, 1, d)

    def slab_fn(col):
        if q_end <= col < k_end:
            return lambda t: t * (dqk ** -0.5)
        if v_end <= col < o_end or col >= o_end + df:
            return jax.nn.sigmoid
        return lambda t: t
    n_main = o_end + w_tail.shape[2]
    n_vo = o_end - k_end
    assert all(e % PROJ_COLS == 0 for e in (q_end, k_end, v_end, o_end, o_end + df, n_main))
    epilogues = tuple(slab_fn(j * PROJ_COLS) for j in range(n_main // PROJ_COLS))

    tables_ctx = _dft_tables(seq_ctx, dg, None)
    tables_lat = _dft_tables(seq_lat, dg, GRID_W)
    f_block0 = (o_end - n_vo) // dg
    ga_block = (o_end - n_vo + df) // d

    def ctx_row(i, tm):
        return 0

    def lat_row(i, tm):
        return 1 + (i * tm) // seq_lat

    def layer(x, l, seq, mod_row, tables, init, state_c):
        proj, proj_t, gcol, grow = _inproj(x, mod, l, mod_row, w_qk, w_vo, w_tail, wg, bg, epilogues)
        res = _mixers(proj, proj_t, gcol, grow, gain, tables, f_block0, df, seq, l, depth, init, state_c)
        x = _merge(res[1], res[0], proj, x, mod, l, mod_row, wa, wb, wo, ln_g, ln_b, ga_block, alpha)
        x = _ffn(x, mod, l, mod_row, w1, w2, ln_g, ln_b, alpha)
        return x, res[2:]

    xp = x_prompt.reshape(batch * seq_ctx, d)
    xs = x_sample.reshape(dec_batch * seq_lat, d)
    new_c, new_n, new_m = None, [], []
    for l in range(depth):
        xp, (new_c, nfin, mfin) = layer(xp, l, seq_ctx, ctx_row, tables_ctx, None, new_c)
        new_n.append(nfin.transpose(0, 2, 1, 3))
        new_m.append(mfin[..., 0].transpose(0, 2, 1))
        xs, _ = layer(xs, l, seq_lat, lat_row, tables_lat, (state_C, state_n, state_m), None)
    return (xp.reshape(batch, seq_ctx, d), xs.reshape(dec_batch, seq_lat, d),
            new_c, jnp.stack(new_n, axis=1), jnp.stack(new_m, axis=1))
```

```python
import functools
import math

import numpy as np
import jax
import jax.numpy as jnp
from jax import lax
from jax.experimental import pallas as pl
from jax.experimental.pallas import tpu as pltpu

F32 = jnp.float32
BF16 = jnp.bfloat16

N_HEADS = 4
N_FGROUPS = 4
GRID_W = 64
LN_EPS = 1e-5
MLSTM_CHUNK = 256
GATE_LANES = 128
GATE_STRIDE = 8
ROW_TILE = 512
FFN_ROW_TILE = 1024
PROJ_COLS = 1024
FFN_COLS = 256
LAT_HEADS_PER_STEP = 2
MERGE_PARTS = 2
CTX_SEQS_PER_STEP = 2
MOD_COLS = 1536
MOD_ROWS = 16
VMEM_LIMIT = 56 * 1024 * 1024
LOG2E = math.log2(math.e)

NT_DIMS = (((1,), (1,)), ((), ()))
TN_DIMS = (((0,), (0,)), ((), ()))


def _dot(a, b):
    return jnp.dot(a, b, preferred_element_type=F32)


def _dot_nt(a, b):
    return lax.dot_general(a, b, NT_DIMS, preferred_element_type=F32)


def _dot_tn(a, b):
    return lax.dot_general(a, b, TN_DIMS, preferred_element_type=F32)


def _split3(x):
    hi = x.astype(BF16)
    r = x - hi.astype(F32)
    mid = r.astype(BF16)
    lo = (r - mid.astype(F32)).astype(BF16)
    return hi, mid, lo


def _layer_norm(y, g, b):
    mu = jnp.mean(y, axis=-1, keepdims=True)
    yc = y - mu
    var = jnp.mean(yc * yc, axis=-1, keepdims=True)
    return yc * lax.rsqrt(var + LN_EPS) * g + b


def _log_sigmoid(x):
    return jnp.minimum(x, 0.0) - jnp.log(1.0 + jnp.exp(-jnp.abs(x)))


def _params(*sem):
    return pltpu.CompilerParams(dimension_semantics=sem, vmem_limit_bytes=VMEM_LIMIT)


def _resident(shape, index_map):
    return pl.BlockSpec(shape, index_map, pipeline_mode=pl.Buffered(1))


def _mod_kernel(c_ref, w_ref, b_ref, o_ref):
    c = c_ref[...]
    s = (c * jax.nn.sigmoid(c)).astype(BF16)
    o_ref[...] = _dot(s, w_ref[...].astype(BF16)) + b_ref[...]


def _modulation(cvec, w_mod, b_mod):
    depth, d, n6 = w_mod.shape
    rows = cvec.shape[0]
    tn = MOD_COLS
    return pl.pallas_call(
        _mod_kernel,
        grid=(depth, n6 // tn),
        in_specs=[
            pl.BlockSpec((rows, d), lambda l, j: (0, 0)),
            pl.BlockSpec((None, d, tn), lambda l, j: (l, 0, j)),
            pl.BlockSpec((None, 1, tn), lambda l, j: (l, 0, j)),
        ],
        out_specs=pl.BlockSpec((None, rows, tn), lambda l, j: (l, 0, j)),
        out_shape=jax.ShapeDtypeStruct((depth, rows, n6), F32),
        compiler_params=_params("parallel", "parallel"),
        name="modulation",
    )(cvec, w_mod, b_mod.reshape(depth, 1, n6))


def _cast_kernel(w_ref, o_ref):
    o_ref[...] = w_ref[...].astype(BF16)


def _transpose_cast_kernel(w_ref, o_ref):
    o_ref[...] = w_ref[...].T.astype(BF16)


def _shift_cast_kernel(a_ref, b_ref, o_ref, *, shift):
    o_ref[...] = jnp.concatenate([a_ref[shift:, :], b_ref[...]], axis=0).T.astype(BF16)


def _split_projection_weights(w_in_t, k_end, o_end, g_end):
    depth, n_in, d = w_in_t.shape
    tn = PROJ_COLS
    shift = g_end - o_end
    n_tail = n_in - g_end
    assert k_end % tn == 0 and o_end % tn == 0 and n_tail % tn == 0 and shift % 8 == 0 and tn % shift == 0
    w_qk = pl.pallas_call(
        _transpose_cast_kernel,
        grid=(depth, k_end // tn),
        in_specs=[pl.BlockSpec((None, tn, d), lambda l, j: (l, j, 0))],
        out_specs=pl.BlockSpec((None, d, tn), lambda l, j: (l, 0, j)),
        out_shape=jax.ShapeDtypeStruct((depth, d, k_end), BF16),
        compiler_params=_params("parallel", "parallel"),
        name="w_qk_cast",
    )(w_in_t)
    kb = k_end // tn
    w_vo = pl.pallas_call(
        _cast_kernel,
        grid=(depth, (o_end - k_end) // tn),
        in_specs=[pl.BlockSpec((None, tn, d), lambda l, j: (l, kb + j, 0))],
        out_specs=pl.BlockSpec((None, tn, d), lambda l, j: (l, j, 0)),
        out_shape=jax.ShapeDtypeStruct((depth, o_end - k_end, d), BF16),
        compiler_params=_params("parallel", "parallel"),
        name="w_vo_cast",
    )(w_in_t)
    base = o_end // tn
    per_slab = tn // shift
    tail = pl.pallas_call(
        functools.partial(_shift_cast_kernel, shift=shift),
        grid=(depth, n_tail // tn),
        in_specs=[
            pl.BlockSpec((None, tn, d), lambda l, j: (l, base + j, 0)),
            pl.BlockSpec((None, shift, d), lambda l, j: (l, (base + j + 1) * per_slab, 0)),
        ],
        out_specs=pl.BlockSpec((None, d, tn), lambda l, j: (l, 0, j)),
        out_shape=jax.ShapeDtypeStruct((depth, d, n_tail), BF16),
        compiler_params=_params("parallel", "parallel"),
        name="w_tail_cast",
    )(w_in_t, w_in_t)
    return w_qk, w_vo, tail


def _inproj_kernel(x_ref, mod_ref, wqk_ref, wvo_ref, wt_ref, wg_ref, bg_ref, proj_ref, proj_t_ref, gcol_ref,
                   grow_ref, h_ref, *, d, epilogues):
    tm = x_ref.shape[0]
    L = MLSTM_CHUNK
    G = GATE_STRIDE
    m = mod_ref[...]
    hb = (x_ref[...] * (1.0 + m[:, d:2 * d]) + m[:, 0:d]).astype(BF16)
    h_ref[...] = hb

    n_rows = 4 * G
    g = _dot(hb, wg_ref[...]) + bg_ref[...]
    lane = lax.broadcasted_iota(jnp.int32, g.shape, 1)
    g = jnp.where((lane & GATE_STRIDE) != 0, _log_sigmoid(g), g)
    gt = g.T[0:n_rows, :]
    r_i = lax.broadcasted_iota(jnp.int32, (L, L), 0)
    c_i = lax.broadcasted_iota(jnp.int32, (L, L), 1)
    upper_b = (r_i <= c_i).astype(BF16)
    lower_b = (r_i >= c_i).astype(BF16)

    def cumsum(x, tri):
        hi, mid, lo = _split3(x)
        return _dot(hi, tri) + _dot(mid, tri) + _dot(lo, tri)

    pieces = []
    for c in range(tm // L):
        xg = gt[:, c * L:(c + 1) * L]
        pre = cumsum(xg[G:2 * G], upper_b)
        suf = cumsum(xg[3 * G:4 * G], lower_b)
        pieces.append(jnp.concatenate([xg[0:G] - pre, pre, xg[2 * G:3 * G] - suf, suf], axis=0))
    gs = jnp.concatenate(pieces, axis=1)
    for qq in range(4):
        grow_ref[qq] = gs[qq * G:(qq + 1) * G, :]
    gcol_ref[...] = jnp.concatenate([gs, jnp.zeros((GATE_LANES - n_rows, tm), F32)], axis=0).T

    tn = PROJ_COLS
    n_qk = wqk_ref.shape[1] // tn
    n_vo = wvo_ref.shape[0] // tn
    for jn, fn in enumerate(epilogues):
        if jn < n_qk:
            proj_ref[:, jn * tn:(jn + 1) * tn] = fn(_dot(h_ref[...], wqk_ref[:, jn * tn:(jn + 1) * tn])).astype(BF16)
        elif jn < n_qk + n_vo:
            jv = jn - n_qk
            proj_t_ref[jv * tn:(jv + 1) * tn, :] = fn(_dot_nt(wvo_ref[jv * tn:(jv + 1) * tn, :], h_ref[...])).astype(BF16)
        else:
            jt = jn - n_qk - n_vo
            jo = jn - n_vo
            proj_ref[:, jo * tn:(jo + 1) * tn] = fn(_dot(h_ref[...], wt_ref[:, jt * tn:(jt + 1) * tn])).astype(BF16)


def _inproj(x, mod, layer, mod_row, w_qk, w_vo, w_tail, w_gate, b_gate, epilogues):
    rows, d = x.shape
    n_qk, n_vo, n_t = w_qk.shape[2], w_vo.shape[1], w_tail.shape[2]
    n = n_qk + n_t
    tm = ROW_TILE
    assert n + n_vo == PROJ_COLS * len(epilogues) and rows % tm == 0 and tm % MLSTM_CHUNK == 0
    return pl.pallas_call(
        functools.partial(_inproj_kernel, d=d, epilogues=epilogues),
        grid=(rows // tm,),
        in_specs=[
            pl.BlockSpec((tm, d), lambda i: (i, 0)),
            pl.BlockSpec((None, None, 1, 6 * d), lambda i: (layer, mod_row(i, tm), 0, 0)),
            _resident((None, d, n_qk), lambda i: (layer, 0, 0)),
            _resident((None, n_vo, d), lambda i: (layer, 0, 0)),
            _resident((None, d, n_t), lambda i: (layer, 0, 0)),
            _resident((None, d, GATE_LANES), lambda i: (layer, 0, 0)),
            _resident((None, 1, GATE_LANES), lambda i: (layer, 0, 0)),
        ],
        out_specs=[
            pl.BlockSpec((tm, n), lambda i: (i, 0)),
            pl.BlockSpec((n_vo, tm), lambda i: (0, i)),
            pl.BlockSpec((tm, GATE_LANES), lambda i: (i, 0)),
            pl.BlockSpec((4, GATE_STRIDE, tm), lambda i: (0, 0, i)),
        ],
        out_shape=[
            jax.ShapeDtypeStruct((rows, n), BF16),
            jax.ShapeDtypeStruct((n_vo, rows), BF16),
            jax.ShapeDtypeStruct((rows, GATE_LANES), F32),
            jax.ShapeDtypeStruct((4, GATE_STRIDE, rows), F32),
        ],
        scratch_shapes=[pltpu.VMEM((tm, d), BF16)],
        compiler_params=_params("parallel"),
        name="inproj",
    )(x, mod, w_qk, w_vo, w_tail, w_gate, b_gate)


def _mlstm_kernel(*refs, heads, n_in, n_out, seq, dg, **static):
    u_ref, fcc_ref, fsc_ref, fms_ref = refs[:4]
    fy_ref = refs[n_in]
    ab_ref = refs[n_in + n_out]
    mixer_refs = refs[4:n_in] + refs[n_in + 1:n_in + n_out] + refs[n_in + n_out + 1:]
    for si in range(u_ref.shape[0] // seq):
        tok = pl.ds(si * seq, seq)
        for hh in range(heads):
            _mlstm_head(si, hh, *mixer_refs, heads=heads, seq=seq, **static)
        for g in range(u_ref.shape[1] // dg):
            u = u_ref[tok, g * dg:(g + 1) * dg]
            ab_ref[si, 0:seq, g * dg:(g + 1) * dg] = _dot(u, fcc_ref[...]).astype(BF16)
            ab_ref[si, seq:2 * seq, g * dg:(g + 1) * dg] = _dot(u, fsc_ref[...]).astype(BF16)
        fy_ref[tok, :] = _dot(fms_ref[...], ab_ref[si]).astype(BF16)


def _mlstm_head(si, hh, *refs, seq, has_init, layer, alias_state, heads):
    if has_init:
        (q_ref, k_ref, vt_ref, ogt_ref, gcol_ref, grow_ref, gain_ref, c0_ref, n0_ref, m0_ref,
         out_ref, cc_ref) = refs
    elif alias_state:
        (q_ref, k_ref, vt_ref, ogt_ref, gcol_ref, grow_ref, gain_ref, _,
         out_ref, cfin_ref, nfin_ref, mfin_ref) = refs
    else:
        (q_ref, k_ref, vt_ref, ogt_ref, gcol_ref, grow_ref, gain_ref,
         out_ref, cfin_ref, nfin_ref, mfin_ref) = refs
    L = MLSTM_CHUNK
    G = GATE_STRIDE
    nc = seq // L
    tok0 = si * seq
    b_idx = pl.program_id(0)
    h_idx = pl.program_id(1) * heads + hh
    dqk = q_ref.shape[1] // heads
    dv = vt_ref.shape[0] // heads
    qc = slice(hh * dqk, (hh + 1) * dqk)
    vr = slice(hh * dv, (hh + 1) * dv)

    s_i = lax.broadcasted_iota(jnp.int32, (L, L), 0)
    t_i = lax.broadcasted_iota(jnp.int32, (L, L), 1)
    masks = (s_i <= t_i, s_i >= t_i)
    lane = lax.broadcasted_iota(jnp.int32, (L, GATE_LANES), 1)

    def gate_rows(c):
        sl = pl.ds(tok0 + c * L, L)
        hs = pl.ds(h_idx, 1)
        b_f = grow_ref[1, hs, sl]
        b_b = grow_ref[3, hs, sl]
        return ((grow_ref[0, hs, sl], b_f, b_f[:, L - 1:L]), (grow_ref[2, hs, sl], b_b, b_b[:, 0:1]))

    def e_column(c, d):
        gc = gcol_ref[pl.ds(tok0 + c * L, L), :]
        return jnp.sum(jnp.where(lane == 2 * d * G + h_idx, gc, 0.0), axis=1, keepdims=True)

    rows = [gate_rows(c) for c in range(nc)]

    def stabiliser_step(c, d, m_prev):
        e_row, _, b_tot = rows[c][d]
        carried = b_tot + m_prev
        m_new = jnp.maximum(carried, b_tot + jnp.max(e_row, axis=1, keepdims=True))
        return m_new, jnp.exp(carried - m_new)

    def state_contribution(c, d, m_new):
        e_row, _, b_tot = rows[c][d]
        w_row = jnp.exp((e_row + b_tot) - m_new).astype(BF16)
        k = k_ref[pl.ds(tok0 + c * L, L), qc]
        lhs = jnp.concatenate([vt_ref[vr, pl.ds(tok0 + c * L, L)] * w_row, jnp.broadcast_to(w_row, (2 * G, L))], axis=0)
        both = _dot(lhs, k)
        return both[0:dv, :], both[dv:dv + 1, :]

    m_pre = [[None] * nc for _ in range(2)]
    n_pre = [[None] * nc for _ in range(2)]
    if has_init:
        for d in range(2):
            order = list(range(nc)) if d == 0 else list(range(nc - 1, -1, -1))
            m = jnp.full((1, 1), m0_ref[b_idx, layer, d, h_idx], F32)
            decays = {}
            for c in order:
                m_pre[d][c] = m
                if c != order[-1]:
                    m, decays[c] = stabiliser_step(c, d, m)
            local = {c: state_contribution(c, d, m_pre[d][nxt]) for c, nxt in zip(order[:-1], order[1:])}
            n = n0_ref[d, pl.ds(h_idx, 1), :]
            cm = c0_ref[d, hh]
            for pos, c in enumerate(order):
                n_pre[d][c] = n
                cc_ref[hh, c, :, d * dqk:(d + 1) * dqk] = cm.astype(BF16)
                if pos == nc - 1:
                    break
                d_c, d_n = local[c]
                cm = decays[c] * cm + d_c
                n = decays[c] * n + d_n
    else:
        for d in range(2):
            m_pre[d][0] = jnp.zeros((1, 1), F32)
            m, _ = stabiliser_step(0, d, m_pre[d][0])
            c_new, n_new = state_contribution(0, d, m)
            if alias_state:
                cfin_ref[si, d, hh] = c_new
            else:
                for lyr in range(cfin_ref.shape[1]):
                    cfin_ref[si, lyr, d, hh] = c_new if lyr == layer else jnp.zeros_like(c_new)
            nfin_ref[si, hh, pl.ds(d, 1), :] = n_new
            mfin_ref[si, hh, pl.ds(d, 1), :] = jnp.broadcast_to(m, (1, GATE_LANES))

    gain = jnp.concatenate([gain_ref[hh]] * (L // GATE_LANES), axis=1)
    for c in range(nc):
        q = q_ref[pl.ds(tok0 + c * L, L), qc]
        k = k_ref[pl.ds(tok0 + c * L, L), qc]
        v_t = vt_ref[vr, pl.ds(tok0 + c * L, L)]
        if has_init:
            n_rows = jnp.concatenate([n_pre[0][c], n_pre[1][c], jnp.zeros((2 * G - 2, dqk), F32)], axis=0)
            kq = _dot_nt(jnp.concatenate([k, n_rows.astype(BF16)], axis=0), q)
            qk_t, qn_t = kq[0:L, :], kq[L:L + G, :]
        else:
            qk_t = _dot_nt(k, q)
        p_t = None
        scale_rows = []
        for d in range(2):
            _, b_row, _ = rows[c][d]
            em = jnp.where(masks[d], e_column(c, d) * LOG2E, -jnp.inf)
            b2 = b_row * LOG2E
            a2 = b2 + m_pre[d][c] * LOG2E
            mt2 = jnp.maximum(a2, b2 + jnp.max(em, axis=0, keepdims=True))
            s_t = qk_t * jnp.exp2((b2 - mt2) + em)
            den = jnp.sum(s_t, axis=0, keepdims=True)
            if has_init:
                inter = jnp.exp2(a2 - mt2)
                den = den + inter * qn_t[d:d + 1, :]
            r = 1.0 / jnp.maximum(jnp.abs(den), jnp.exp2(-mt2))
            p_t = s_t * r if p_t is None else p_t + s_t * r
            if has_init:
                scale_rows.append((inter * r).astype(BF16))
        h_t = _dot(v_t, p_t.astype(BF16))
        if has_init:
            q_t = q.T
            qs_t = jnp.concatenate([q_t * scale_rows[0], q_t * scale_rows[1]], axis=0)
            h_t = h_t + _dot(cc_ref[hh, c], qs_t)
        mu = jnp.mean(h_t, axis=0, keepdims=True)
        hc = h_t - mu
        var = jnp.mean(hc * hc, axis=0, keepdims=True)
        hn = hc * lax.rsqrt(var + LN_EPS) * gain
        out_ref[vr, pl.ds(tok0 + c * L, L)] = hn.astype(BF16) * ogt_ref[vr, pl.ds(tok0 + c * L, L)]


def _mixers(proj, proj_t, gcol, grow, gain, tables, f_block0, df, seq, layer, depth, init=None, state_c=None):
    rows = proj.shape[0]
    fcc, fsc, fms = tables
    dg = fcc.shape[0]
    n_seq = rows // seq
    dv = gain.shape[2]
    dm = N_HEADS * dv
    dqk = dv // 2
    has_init = init is not None
    hps = LAT_HEADS_PER_STEP if has_init else N_HEADS
    sps = CTX_SEQS_PER_STEP if (not has_init and state_c is not None) else 1
    blk = sps * seq
    hb = N_HEADS // hps
    assert seq % MLSTM_CHUNK == 0 and (has_init or seq == MLSTM_CHUNK) and dqk % GATE_LANES == 0
    assert df // dg == N_HEADS and f_block0 % hps == 0
    in_specs = [
        pl.BlockSpec((blk, hps * dg), lambda b, h: (b, f_block0 // hps + h)),
        _resident((dg, dg), lambda b, h: (0, 0)),
        _resident((dg, dg), lambda b, h: (0, 0)),
        _resident((seq, 2 * seq), lambda b, h: (0, 0)),
        pl.BlockSpec((blk, hps * dqk), lambda b, h: (b, h)),
        pl.BlockSpec((blk, hps * dqk), lambda b, h: (b, hb + h)),
        pl.BlockSpec((hps * dv, blk), lambda b, h: (h, b)),
        pl.BlockSpec((hps * dv, blk), lambda b, h: (hb + h, b)),
        pl.BlockSpec((blk, GATE_LANES), lambda b, h: (b, 0)),
        pl.BlockSpec((4, GATE_STRIDE, blk), lambda b, h: (0, 0, b)),
        pl.BlockSpec((None, hps, dv, GATE_LANES), lambda b, h: (layer, h, 0, 0)),
    ]
    args = [proj, fcc, fsc, fms, proj, proj, proj_t, proj_t, gcol, grow, gain]
    out_specs = [pl.BlockSpec((blk, hps * dg), lambda b, h: (b, h)),
                 pl.BlockSpec((hps * dv, blk), lambda b, h: (h, b))]
    out_shape = [jax.ShapeDtypeStruct((rows, df), BF16), jax.ShapeDtypeStruct((dm, rows), BF16)]
    scratch = [pltpu.VMEM((sps, 2 * seq, hps * dg), BF16)]
    aliases = {}
    if has_init:
        init_c, init_n, init_m = init
        in_specs += [
            pl.BlockSpec((None, None, 2, hps, dv, dqk), lambda b, h: (b, layer, 0, h, 0, 0)),
            pl.BlockSpec((None, None, 2, N_HEADS, dqk), lambda b, h: (b, layer, 0, 0, 0)),
            pl.BlockSpec(memory_space=pltpu.MemorySpace.SMEM),
        ]
        args += [init_c, init_n, init_m]
        scratch += [pltpu.VMEM((hps, seq // MLSTM_CHUNK, dv, 2 * dqk), BF16)]
    else:
        if state_c is not None:
            in_specs.append(pl.BlockSpec(memory_space=pl.ANY))
            args.append(state_c)
            aliases = {len(args) - 1: 2}
            state_spec = pl.BlockSpec((sps, None, 2, hps, dv, dqk), lambda b, h: (b, layer, 0, h, 0, 0))
        else:
            state_spec = pl.BlockSpec((sps, depth, 2, hps, dv, dqk), lambda b, h: (b, 0, 0, h, 0, 0))
        out_specs += [
            state_spec,
            pl.BlockSpec((sps, hps, 2, dqk), lambda b, h: (b, h, 0, 0)),
            pl.BlockSpec((sps, hps, 2, GATE_LANES), lambda b, h: (b, h, 0, 0)),
        ]
        out_shape += [
            jax.ShapeDtypeStruct((n_seq, depth, 2, N_HEADS, dv, dqk), F32),
            jax.ShapeDtypeStruct((n_seq, N_HEADS, 2, dqk), F32),
            jax.ShapeDtypeStruct((n_seq, N_HEADS, 2, GATE_LANES), F32),
        ]
    return pl.pallas_call(
        functools.partial(_mlstm_kernel, seq=seq, has_init=has_init, layer=layer, alias_state=state_c is not None,
                          heads=hps, n_in=len(args), n_out=len(out_specs), dg=dg),
        grid=(n_seq // sps, N_HEADS // hps),
        in_specs=in_specs,
        out_specs=out_specs,
        out_shape=out_shape,
        scratch_shapes=scratch,
        input_output_aliases=aliases,
        compiler_params=_params("parallel", "parallel"),
        name="mixers_lat" if has_init else "mixers_ctx",
    )(*args)


def _dft_tables(seq, dg, grid_w):
    ch = np.arange(dg)
    ang_c = 2.0 * np.pi * ((np.outer(ch, ch) % dg) / dg)
    sc_c = 1.0 / np.sqrt(dg)
    cc = np.cos(ang_c) * sc_c
    sc = np.sin(ang_c) * sc_c
    t = np.arange(seq)
    if grid_w is None:
        frac = (np.outer(t, t) % seq) / seq
    else:
        rows = seq // grid_w
        r, w = t // grid_w, t % grid_w
        frac = (np.outer(r, r) % rows) / rows + (np.outer(w, w) % grid_w) / grid_w
    ang_s = 2.0 * np.pi * frac
    sc_s = 1.0 / np.sqrt(seq)
    ms = np.concatenate([np.cos(ang_s) * sc_s, -np.sin(ang_s) * sc_s], axis=1)
    return (jnp.asarray(cc, F32).astype(BF16), jnp.asarray(sc, F32).astype(BF16),
            jnp.asarray(ms, F32).astype(BF16))


def _merge_kernel(hgt_ref, fy_ref, ga_ref, gb_ref, x_ref, mod_ref, wa_ref, wb_ref, wo_ref, lg_ref, lb_ref,
                  o_ref, *, d, alpha):
    g1 = mod_ref[...][:, 2 * d:3 * d]
    tm = x_ref.shape[0]
    for part in range(MERGE_PARTS):
        r = slice(part * tm // MERGE_PARTS, (part + 1) * tm // MERGE_PARTS)
        ya = _dot_tn(hgt_ref[:, r], wa_ref[...])
        yb = _dot(fy_ref[r, :], wb_ref[...])
        merged = ga_ref[r, :].astype(F32) * ya + gb_ref[r, :].astype(F32) * yb
        out = _dot(merged.astype(BF16), wo_ref[...])
        o_ref[r, :] = _layer_norm(alpha * x_ref[r, :] + g1 * out, lg_ref[...], lb_ref[...])


def _merge(hg_t, fy, proj, x, mod, layer, mod_row, wa, wb, wo, ln_g, ln_b, ga_block, alpha):
    rows, d = x.shape
    dm = hg_t.shape[0]
    df = fy.shape[1]
    tm = ROW_TILE
    wmap = lambda i: (layer, 0, 0)
    return pl.pallas_call(
        functools.partial(_merge_kernel, d=d, alpha=alpha),
        grid=(rows // tm,),
        in_specs=[
            pl.BlockSpec((dm, tm), lambda i: (0, i)),
            pl.BlockSpec((tm, df), lambda i: (i, 0)),
            pl.BlockSpec((tm, d), lambda i: (i, ga_block)),
            pl.BlockSpec((tm, d), lambda i: (i, ga_block + 1)),
            pl.BlockSpec((tm, d), lambda i: (i, 0)),
            pl.BlockSpec((None, None, 1, 6 * d), lambda i: (layer, mod_row(i, tm), 0, 0)),
            _resident((None, dm, d), wmap),
            _resident((None, df, d), wmap),
            _resident((None, d, d), wmap),
            _resident((None, None, 1, d), lambda i: (layer, 0, 0, 0)),
            _resident((None, None, 1, d), lambda i: (layer, 0, 0, 0)),
        ],
        out_specs=pl.BlockSpec((tm, d), lambda i: (i, 0)),
        out_shape=jax.ShapeDtypeStruct((rows, d), F32),
        compiler_params=_params("parallel"),
        name="merge",
    )(hg_t, fy, proj, proj, x, mod, wa, wb, wo, ln_g, ln_b)


def _ffn_kernel(x_ref, mod_ref, w1_ref, w2_ref, lg_ref, lb_ref, o_ref, h_ref, g_ref, *, d, alpha):
    dff = w2_ref.shape[0]
    m = mod_ref[...]
    x = x_ref[...]
    h_ref[...] = (x * (1.0 + m[:, 4 * d:5 * d]) + m[:, 3 * d:4 * d]).astype(BF16)
    tk = FFN_COLS
    for kk in range(dff // tk):
        a = _dot(h_ref[...], w1_ref[:, kk * tk:(kk + 1) * tk])
        u = _dot(h_ref[...], w1_ref[:, dff + kk * tk:dff + (kk + 1) * tk])
        g_ref[:, kk * tk:(kk + 1) * tk] = (a * jax.nn.sigmoid(a) * u).astype(BF16)
    f = _dot(g_ref[...], w2_ref[...])
    g2 = m[:, 5 * d:6 * d]
    o_ref[...] = _layer_norm(alpha * x + g2 * f, lg_ref[...], lb_ref[...])


def _ffn(x, mod, layer, mod_row, w1, w2, ln_g, ln_b, alpha):
    rows, d = x.shape
    dff = w2.shape[1]
    tm = FFN_ROW_TILE
    assert dff % FFN_COLS == 0
    return pl.pallas_call(
        functools.partial(_ffn_kernel, d=d, alpha=alpha),
        grid=(rows // tm,),
        in_specs=[
            pl.BlockSpec((tm, d), lambda i: (i, 0)),
            pl.BlockSpec((None, None, 1, 6 * d), lambda i: (layer, mod_row(i, tm), 0, 0)),
            _resident((None, d, 2 * dff), lambda i: (layer, 0, 0)),
            _resident((None, dff, d), lambda i: (layer, 0, 0)),
            _resident((None, None, 1, d), lambda i: (layer, 1, 0, 0)),
            _resident((None, None, 1, d), lambda i: (layer, 1, 0, 0)),
        ],
        out_specs=pl.BlockSpec((tm, d), lambda i: (i, 0)),
        out_shape=jax.ShapeDtypeStruct((rows, d), F32),
        scratch_shapes=[pltpu.VMEM((tm, d), BF16), pltpu.VMEM((tm, dff), BF16)],
        compiler_params=_params("parallel"),
        name="ffn",
    )(x, mod, w1, w2, ln_g, ln_b)


def kernel(x_prompt, x_sample, c, state_C, state_n, state_m, c_ctx, w_mod, b_mod, w_in, b_gate, mh_gain,
           w_branch_a, w_branch_b, w_out, ln_gain, ln_bias, w_ffn_in, w_ffn_out):
    batch, seq_ctx, d = x_prompt.shape
    dec_batch, seq_lat, _ = x_sample.shape
    depth = w_in.shape[0]
    dm = w_branch_a.shape[1]
    df = w_branch_b.shape[1]
    dv = dm // N_HEADS
    dqk = dv // 2
    dg = df // N_FGROUPS
    n_gates = 4 * N_HEADS
    q_end = N_HEADS * dqk
    k_end = 2 * q_end
    v_end = k_end + dm
    o_end = v_end + dm
    g_end = o_end + n_gates
    alpha = float((2 * depth) ** 0.25)

    mod_rows = MOD_ROWS
    assert 1 + dec_batch <= mod_rows
    cvec = jnp.zeros((mod_rows, d), F32).at[0].set(c_ctx).at[1:1 + dec_batch].set(c)
    mod = _modulation(cvec, w_mod, b_mod).reshape(depth, mod_rows, 1, 6 * d)

    w_in_t = jnp.swapaxes(w_in, 1, 2)
    w_qk, w_vo, w_tail = _split_projection_weights(w_in_t, k_end, o_end, g_end)
    gate_pad = ((0, 0), (0, 0), (0, GATE_STRIDE - N_HEADS), (0, 0))
    lane_pad = ((0, 0), (0, 0), (0, GATE_LANES - 4 * GATE_STRIDE))
    wg = jnp.pad(w_in_t[:, o_end:g_end, :].reshape(depth, 4, N_HEADS, d), gate_pad)
    wg = jnp.pad(jnp.swapaxes(wg.reshape(depth, 4 * GATE_STRIDE, d), 1, 2), lane_pad).astype(BF16)
    bg = jnp.pad(b_gate.reshape(depth, 4, N_HEADS, 1), gate_pad).reshape(depth, 1, 4 * GATE_STRIDE)
    bg = jnp.pad(bg, lane_pad)
    wa = w_branch_a.astype(BF16)
    wb = w_branch_b.astype(BF16)
    wo = w_out.astype(BF16)
    w1 = w_ffn_in.astype(BF16)
    w2 = w_ffn_out.astype(BF16)
    gain = jnp.broadcast_to(mh_gain.reshape(depth, N_HEADS, dv, 1), (depth, N_HEADS, dv, GATE_LANES))
    ln_g = ln_gain.reshape(depth, 2, 1, d)
    ln_b = ln_bias.reshape(depth, 2, 1, d)

    def slab_fn(col):
        if q_end <= col < k_end:
            return lambda t: t * (dqk ** -0.5)
        if v_end <= col < o_end or col >= o_end + df:
            return jax.nn.sigmoid
        return lambda t: t
    n_main = o_end + w_tail.shape[2]
    n_vo = o_end - k_end
    assert all(e % PROJ_COLS == 0 for e in (q_end, k_end, v_end, o_end, o_end + df, n_main))
    epilogues = tuple(slab_fn(j * PROJ_COLS) for j in range(n_main // PROJ_COLS))

    tables_ctx = _dft_tables(seq_ctx, dg, None)
    tables_lat = _dft_tables(seq_lat, dg, GRID_W)
    f_block0 = (o_end - n_vo) // dg
    ga_block = (o_end - n_vo + df) // d

    def ctx_row(i, tm):
        return 0

    def lat_row(i, tm):
        return 1 + (i * tm) // seq_lat

    def layer(x, l, seq, mod_row, tables, init, state_c):
        proj, proj_t, gcol, grow = _inproj(x, mod, l, mod_row, w_qk, w_vo, w_tail, wg, bg, epilogues)
        res = _mixers(proj, proj_t, gcol, grow, gain, tables, f_block0, df, seq, l, depth, init, state_c)
        x = _merge(res[1], res[0], proj, x, mod, l, mod_row, wa, wb, wo, ln_g, ln_b, ga_block, alpha)
        x = _ffn(x, mod, l, mod_row, w1, w2, ln_g, ln_b, alpha)
        return x, res[2:]

    xp = x_prompt.reshape(batch * seq_ctx, d)
    xs = x_sample.reshape(dec_batch * seq_lat, d)
    new_c, new_n, new_m = None, [], []
    for l in range(depth):
        xp, (new_c, nfin, mfin) = layer(xp, l, seq_ctx, ctx_row, tables_ctx, None, new_c)
        new_n.append(nfin.transpose(0, 2, 1, 3))
        new_m.append(mfin[..., 0].transpose(0, 2, 1))
        xs, _ = layer(xs, l, seq_lat, lat_row, tables_lat, (state_C, state_n, state_m), None)
    return (xp.reshape(batch, seq_ctx, d), xs.reshape(dec_batch, seq_lat, d),
            new_c, jnp.stack(new_n, axis=1), jnp.stack(new_m, axis=1))
```

```python
import functools
import math

import numpy as np
import jax
import jax.numpy as jnp
from jax import lax
from jax.experimental import pallas as pl
from jax.experimental.pallas import tpu as pltpu

F32 = jnp.float32
BF16 = jnp.bfloat16

N_HEADS = 4
N_FGROUPS = 4
GRID_W = 64
LN_EPS = 1e-5
MLSTM_CHUNK = 256
GATE_LANES = 128
GATE_STRIDE = 8
ROW_TILE = 512
WIDE_ROW_TILE = 1024
PROJ_COLS = 1024
FFN_COLS = 256
LAT_HEADS_PER_STEP = 2
MERGE_PARTS = 4
CTX_SEQS_PER_STEP = 2
MOD_COLS = 1536
MOD_ROWS = 16
VMEM_LIMIT = 56 * 1024 * 1024
LOG2E = math.log2(math.e)

NT_DIMS = (((1,), (1,)), ((), ()))
TN_DIMS = (((0,), (0,)), ((), ()))


def _dot(a, b):
    return jnp.dot(a, b, preferred_element_type=F32)


def _dot_nt(a, b):
    return lax.dot_general(a, b, NT_DIMS, preferred_element_type=F32)


def _dot_tn(a, b):
    return lax.dot_general(a, b, TN_DIMS, preferred_element_type=F32)


def _split3(x):
    hi = x.astype(BF16)
    r = x - hi.astype(F32)
    mid = r.astype(BF16)
    lo = (r - mid.astype(F32)).astype(BF16)
    return hi, mid, lo


def _layer_norm(y, g, b):
    mu = jnp.mean(y, axis=-1, keepdims=True)
    yc = y - mu
    var = jnp.mean(yc * yc, axis=-1, keepdims=True)
    return yc * lax.rsqrt(var + LN_EPS) * g + b


def _log_sigmoid(x):
    return jnp.minimum(x, 0.0) - jnp.log(1.0 + jnp.exp(-jnp.abs(x)))


def _params(*sem):
    return pltpu.CompilerParams(dimension_semantics=sem, vmem_limit_bytes=VMEM_LIMIT)


def _resident(shape, index_map):
    return pl.BlockSpec(shape, index_map, pipeline_mode=pl.Buffered(1))


def _mod_kernel(c_ref, w_ref, b_ref, o_ref):
    c = c_ref[...]
    s = (c * jax.nn.sigmoid(c)).astype(BF16)
    o_ref[...] = _dot(s, w_ref[...].astype(BF16)) + b_ref[...]


def _modulation(cvec, w_mod, b_mod):
    depth, d, n6 = w_mod.shape
    rows = cvec.shape[0]
    tn = MOD_COLS
    return pl.pallas_call(
        _mod_kernel,
        grid=(depth, n6 // tn),
        in_specs=[
            pl.BlockSpec((rows, d), lambda l, j: (0, 0)),
            pl.BlockSpec((None, d, tn), lambda l, j: (l, 0, j)),
            pl.BlockSpec((None, 1, tn), lambda l, j: (l, 0, j)),
        ],
        out_specs=pl.BlockSpec((None, rows, tn), lambda l, j: (l, 0, j)),
        out_shape=jax.ShapeDtypeStruct((depth, rows, n6), F32),
        compiler_params=_params("parallel", "parallel"),
        name="modulation",
    )(cvec, w_mod, b_mod.reshape(depth, 1, n6))


def _cast_kernel(w_ref, o_ref):
    o_ref[...] = w_ref[...].astype(BF16)


def _transpose_cast_kernel(w_ref, o_ref):
    o_ref[...] = w_ref[...].T.astype(BF16)


def _shift_cast_kernel(a_ref, b_ref, o_ref, *, shift):
    o_ref[...] = jnp.concatenate([a_ref[shift:, :], b_ref[...]], axis=0).T.astype(BF16)


def _split_projection_weights(w_in_t, k_end, o_end, g_end):
    depth, n_in, d = w_in_t.shape
    tn = PROJ_COLS
    shift = g_end - o_end
    n_tail = n_in - g_end
    assert k_end % tn == 0 and o_end % tn == 0 and n_tail % tn == 0 and shift % 8 == 0 and tn % shift == 0
    w_qk = pl.pallas_call(
        _transpose_cast_kernel,
        grid=(depth, k_end // tn),
        in_specs=[pl.BlockSpec((None, tn, d), lambda l, j: (l, j, 0))],
        out_specs=pl.BlockSpec((None, d, tn), lambda l, j: (l, 0, j)),
        out_shape=jax.ShapeDtypeStruct((depth, d, k_end), BF16),
        compiler_params=_params("parallel", "parallel"),
        name="w_qk_cast",
    )(w_in_t)
    kb = k_end // tn
    w_vo = pl.pallas_call(
        _cast_kernel,
        grid=(depth, (o_end - k_end) // tn),
        in_specs=[pl.BlockSpec((None, tn, d), lambda l, j: (l, kb + j, 0))],
        out_specs=pl.BlockSpec((None, tn, d), lambda l, j: (l, j, 0)),
        out_shape=jax.ShapeDtypeStruct((depth, o_end - k_end, d), BF16),
        compiler_params=_params("parallel", "parallel"),
        name="w_vo_cast",
    )(w_in_t)
    base = o_end // tn
    per_slab = tn // shift
    tail = pl.pallas_call(
        functools.partial(_shift_cast_kernel, shift=shift),
        grid=(depth, n_tail // tn),
        in_specs=[
            pl.BlockSpec((None, tn, d), lambda l, j: (l, base + j, 0)),
            pl.BlockSpec((None, shift, d), lambda l, j: (l, (base + j + 1) * per_slab, 0)),
        ],
        out_specs=pl.BlockSpec((None, d, tn), lambda l, j: (l, 0, j)),
        out_shape=jax.ShapeDtypeStruct((depth, d, n_tail), BF16),
        compiler_params=_params("parallel", "parallel"),
        name="w_tail_cast",
    )(w_in_t, w_in_t)
    return w_qk, w_vo, tail


def _inproj_kernel(x_ref, mod_ref, wqk_ref, wvo_ref, wt_ref, wg_ref, bg_ref, proj_ref, proj_t_ref, gcol_ref,
                   grow_ref, h_ref, *, d, epilogues):
    tm = x_ref.shape[0]
    L = MLSTM_CHUNK
    G = GATE_STRIDE
    m = mod_ref[...]
    hb = (x_ref[...] * (1.0 + m[:, d:2 * d]) + m[:, 0:d]).astype(BF16)
    h_ref[...] = hb

    n_rows = 4 * G
    g = _dot(hb, wg_ref[...]) + bg_ref[...]
    lane = lax.broadcasted_iota(jnp.int32, g.shape, 1)
    g = jnp.where((lane & GATE_STRIDE) != 0, _log_sigmoid(g), g)
    gt = g.T[0:n_rows, :]
    r_i = lax.broadcasted_iota(jnp.int32, (L, L), 0)
    c_i = lax.broadcasted_iota(jnp.int32, (L, L), 1)
    upper_b = (r_i <= c_i).astype(BF16)
    lower_b = (r_i >= c_i).astype(BF16)

    def cumsum(x, tri):
        hi, mid, lo = _split3(x)
        return _dot(hi, tri) + _dot(mid, tri) + _dot(lo, tri)

    pieces = []
    for c in range(tm // L):
        xg = gt[:, c * L:(c + 1) * L]
        pre = cumsum(xg[G:2 * G], upper_b)
        suf = cumsum(xg[3 * G:4 * G], lower_b)
        pieces.append(jnp.concatenate([xg[0:G] - pre, pre, xg[2 * G:3 * G] - suf, suf], axis=0))
    gs = jnp.concatenate(pieces, axis=1)
    for qq in range(4):
        grow_ref[qq] = gs[qq * G:(qq + 1) * G, :]
    gcol_ref[...] = jnp.concatenate([gs, jnp.zeros((GATE_LANES - n_rows, tm), F32)], axis=0).T

    tn = PROJ_COLS
    n_qk = wqk_ref.shape[1] // tn
    n_vo = wvo_ref.shape[0] // tn
    for jn, fn in enumerate(epilogues):
        if jn < n_qk:
            proj_ref[:, jn * tn:(jn + 1) * tn] = fn(_dot(h_ref[...], wqk_ref[:, jn * tn:(jn + 1) * tn])).astype(BF16)
        elif jn < n_qk + n_vo:
            jv = jn - n_qk
            proj_t_ref[jv * tn:(jv + 1) * tn, :] = fn(_dot_nt(wvo_ref[jv * tn:(jv + 1) * tn, :], h_ref[...])).astype(BF16)
        else:
            jt = jn - n_qk - n_vo
            jo = jn - n_vo
            proj_ref[:, jo * tn:(jo + 1) * tn] = fn(_dot(h_ref[...], wt_ref[:, jt * tn:(jt + 1) * tn])).astype(BF16)


def _inproj(x, mod, layer, mod_row, w_qk, w_vo, w_tail, w_gate, b_gate, epilogues):
    rows, d = x.shape
    n_qk, n_vo, n_t = w_qk.shape[2], w_vo.shape[1], w_tail.shape[2]
    n = n_qk + n_t
    tm = ROW_TILE
    assert n + n_vo == PROJ_COLS * len(epilogues) and rows % tm == 0 and tm % MLSTM_CHUNK == 0
    return pl.pallas_call(
        functools.partial(_inproj_kernel, d=d, epilogues=epilogues),
        grid=(rows // tm,),
        in_specs=[
            pl.BlockSpec((tm, d), lambda i: (i, 0)),
            pl.BlockSpec((None, None, 1, 6 * d), lambda i: (layer, mod_row(i, tm), 0, 0)),
            _resident((None, d, n_qk), lambda i: (layer, 0, 0)),
            _resident((None, n_vo, d), lambda i: (layer, 0, 0)),
            _resident((None, d, n_t), lambda i: (layer, 0, 0)),
            _resident((None, d, GATE_LANES), lambda i: (layer, 0, 0)),
            _resident((None, 1, GATE_LANES), lambda i: (layer, 0, 0)),
        ],
        out_specs=[
            pl.BlockSpec((tm, n), lambda i: (i, 0)),
            pl.BlockSpec((n_vo, tm), lambda i: (0, i)),
            pl.BlockSpec((tm, GATE_LANES), lambda i: (i, 0)),
            pl.BlockSpec((4, GATE_STRIDE, tm), lambda i: (0, 0, i)),
        ],
        out_shape=[
            jax.ShapeDtypeStruct((rows, n), BF16),
            jax.ShapeDtypeStruct((n_vo, rows), BF16),
            jax.ShapeDtypeStruct((rows, GATE_LANES), F32),
            jax.ShapeDtypeStruct((4, GATE_STRIDE, rows), F32),
        ],
        scratch_shapes=[pltpu.VMEM((tm, d), BF16)],
        compiler_params=_params("parallel"),
        name="inproj",
    )(x, mod, w_qk, w_vo, w_tail, w_gate, b_gate)


def _mlstm_kernel(*refs, heads, n_in, n_out, seq, dg, **static):
    u_ref, fcc_ref, fsc_ref, fms_ref = refs[:4]
    fy_ref = refs[n_in]
    ab_ref = refs[n_in + n_out]
    mixer_refs = refs[4:n_in] + refs[n_in + 1:n_in + n_out] + refs[n_in + n_out + 1:]
    for si in range(u_ref.shape[0] // seq):
        tok = pl.ds(si * seq, seq)
        for hh in range(heads):
            _mlstm_head(si, hh, *mixer_refs, heads=heads, seq=seq, **static)
        for g in range(u_ref.shape[1] // dg):
            u = u_ref[tok, g * dg:(g + 1) * dg]
            ab_ref[si, 0:seq, g * dg:(g + 1) * dg] = _dot(u, fcc_ref[...]).astype(BF16)
            ab_ref[si, seq:2 * seq, g * dg:(g + 1) * dg] = _dot(u, fsc_ref[...]).astype(BF16)
        fy_ref[tok, :] = _dot(fms_ref[...], ab_ref[si]).astype(BF16)


def _mlstm_head(si, hh, *refs, seq, has_init, layer, alias_state, heads):
    if has_init:
        (q_ref, k_ref, vt_ref, ogt_ref, gcol_ref, grow_ref, gain_ref, c0_ref, n0_ref, m0_ref,
         out_ref, cc_ref) = refs
    elif alias_state:
        (q_ref, k_ref, vt_ref, ogt_ref, gcol_ref, grow_ref, gain_ref, _,
         out_ref, cfin_ref, nfin_ref, mfin_ref) = refs
    else:
        (q_ref, k_ref, vt_ref, ogt_ref, gcol_ref, grow_ref, gain_ref,
         out_ref, cfin_ref, nfin_ref, mfin_ref) = refs
    L = MLSTM_CHUNK
    G = GATE_STRIDE
    nc = seq // L
    tok0 = si * seq
    b_idx = pl.program_id(0)
    h_idx = pl.program_id(1) * heads + hh
    dqk = q_ref.shape[1] // heads
    dv = vt_ref.shape[0] // heads
    qc = slice(hh * dqk, (hh + 1) * dqk)
    vr = slice(hh * dv, (hh + 1) * dv)

    s_i = lax.broadcasted_iota(jnp.int32, (L, L), 0)
    t_i = lax.broadcasted_iota(jnp.int32, (L, L), 1)
    masks = (s_i <= t_i, s_i >= t_i)
    lane = lax.broadcasted_iota(jnp.int32, (L, GATE_LANES), 1)

    def gate_rows(c):
        sl = pl.ds(tok0 + c * L, L)
        hs = pl.ds(h_idx, 1)
        b_f = grow_ref[1, hs, sl]
        b_b = grow_ref[3, hs, sl]
        return ((grow_ref[0, hs, sl], b_f, b_f[:, L - 1:L]), (grow_ref[2, hs, sl], b_b, b_b[:, 0:1]))

    def e_column(c, d):
        gc = gcol_ref[pl.ds(tok0 + c * L, L), :]
        return jnp.sum(jnp.where(lane == 2 * d * G + h_idx, gc, 0.0), axis=1, keepdims=True)

    rows = [gate_rows(c) for c in range(nc)]

    def stabiliser_step(c, d, m_prev):
        e_row, _, b_tot = rows[c][d]
        carried = b_tot + m_prev
        m_new = jnp.maximum(carried, b_tot + jnp.max(e_row, axis=1, keepdims=True))
        return m_new, jnp.exp(carried - m_new)

    def state_contribution(c, d, m_new):
        e_row, _, b_tot = rows[c][d]
        w_row = jnp.exp((e_row + b_tot) - m_new).astype(BF16)
        k = k_ref[pl.ds(tok0 + c * L, L), qc]
        lhs = jnp.concatenate([vt_ref[vr, pl.ds(tok0 + c * L, L)] * w_row, jnp.broadcast_to(w_row, (2 * G, L))], axis=0)
        both = _dot(lhs, k)
        return both[0:dv, :], both[dv:dv + 1, :]

    m_pre = [[None] * nc for _ in range(2)]
    n_pre = [[None] * nc for _ in range(2)]
    if has_init:
        for d in range(2):
            order = list(range(nc)) if d == 0 else list(range(nc - 1, -1, -1))
            m = jnp.full((1, 1), m0_ref[b_idx, layer, d, h_idx], F32)
            decays = {}
            for c in order:
                m_pre[d][c] = m
                if c != order[-1]:
                    m, decays[c] = stabiliser_step(c, d, m)
            local = {c: state_contribution(c, d, m_pre[d][nxt]) for c, nxt in zip(order[:-1], order[1:])}
            n = n0_ref[d, pl.ds(h_idx, 1), :]
            cm = c0_ref[d, hh]
            for pos, c in enumerate(order):
                n_pre[d][c] = n
                cc_ref[hh, c, :, d * dqk:(d + 1) * dqk] = cm.astype(BF16)
                if pos == nc - 1:
                    break
                d_c, d_n = local[c]
                cm = decays[c] * cm + d_c
                n = decays[c] * n + d_n
    else:
        for d in range(2):
            m_pre[d][0] = jnp.zeros((1, 1), F32)
            m, _ = stabiliser_step(0, d, m_pre[d][0])
            c_new, n_new = state_contribution(0, d, m)
            if alias_state:
                cfin_ref[si, d, hh] = c_new
            else:
                for lyr in range(cfin_ref.shape[1]):
                    cfin_ref[si, lyr, d, hh] = c_new if lyr == layer else jnp.zeros_like(c_new)
            nfin_ref[si, hh, pl.ds(d, 1), :] = n_new
            mfin_ref[si, hh, pl.ds(d, 1), :] = jnp.broadcast_to(m, (1, GATE_LANES))

    gain = jnp.concatenate([gain_ref[hh]] * (L // GATE_LANES), axis=1)
    for c in range(nc):
        q = q_ref[pl.ds(tok0 + c * L, L), qc]
        k = k_ref[pl.ds(tok0 + c * L, L), qc]
        v_t = vt_ref[vr, pl.ds(tok0 + c * L, L)]
        if has_init:
            n_rows = jnp.concatenate([n_pre[0][c], n_pre[1][c], jnp.zeros((2 * G - 2, dqk), F32)], axis=0)
            kq = _dot_nt(jnp.concatenate([k, n_rows.astype(BF16)], axis=0), q)
            qk_t, qn_t = kq[0:L, :], kq[L:L + G, :]
        else:
            qk_t = _dot_nt(k, q)
        p_t = None
        scale_rows = []
        for d in range(2):
            _, b_row, _ = rows[c][d]
            em = jnp.where(masks[d], e_column(c, d) * LOG2E, -jnp.inf)
            b2 = b_row * LOG2E
            a2 = b2 + m_pre[d][c] * LOG2E
            mt2 = jnp.maximum(a2, b2 + jnp.max(em, axis=0, keepdims=True))
            s_t = qk_t * jnp.exp2((b2 - mt2) + em)
            den = jnp.sum(s_t, axis=0, keepdims=True)
            if has_init:
                inter = jnp.exp2(a2 - mt2)
                den = den + inter * qn_t[d:d + 1, :]
            r = 1.0 / jnp.maximum(jnp.abs(den), jnp.exp2(-mt2))
            p_t = s_t * r if p_t is None else p_t + s_t * r
            if has_init:
                scale_rows.append((inter * r).astype(BF16))
        h_t = _dot(v_t, p_t.astype(BF16))
        if has_init:
            q_t = q.T
            qs_t = jnp.concatenate([q_t * scale_rows[0], q_t * scale_rows[1]], axis=0)
            h_t = h_t + _dot(cc_ref[hh, c], qs_t)
        mu = jnp.mean(h_t, axis=0, keepdims=True)
        hc = h_t - mu
        var = jnp.mean(hc * hc, axis=0, keepdims=True)
        hn = hc * lax.rsqrt(var + LN_EPS) * gain
        out_ref[vr, pl.ds(tok0 + c * L, L)] = hn.astype(BF16) * ogt_ref[vr, pl.ds(tok0 + c * L, L)]


def _mixers(proj, proj_t, gcol, grow, gain, tables, f_block0, df, seq, layer, depth, init=None, state_c=None):
    rows = proj.shape[0]
    fcc, fsc, fms = tables
    dg = fcc.shape[0]
    n_seq = rows // seq
    dv = gain.shape[2]
    dm = N_HEADS * dv
    dqk = dv // 2
    has_init = init is not None
    hps = LAT_HEADS_PER_STEP if has_init else N_HEADS
    sps = CTX_SEQS_PER_STEP if (not has_init and state_c is not None) else 1
    blk = sps * seq
    hb = N_HEADS // hps
    assert seq % MLSTM_CHUNK == 0 and (has_init or seq == MLSTM_CHUNK) and dqk % GATE_LANES == 0
    assert df // dg == N_HEADS and f_block0 % hps == 0
    in_specs = [
        pl.BlockSpec((blk, hps * dg), lambda b, h: (b, f_block0 // hps + h)),
        _resident((dg, dg), lambda b, h: (0, 0)),
        _resident((dg, dg), lambda b, h: (0, 0)),
        _resident((seq, 2 * seq), lambda b, h: (0, 0)),
        pl.BlockSpec((blk, hps * dqk), lambda b, h: (b, h)),
        pl.BlockSpec((blk, hps * dqk), lambda b, h: (b, hb + h)),
        pl.BlockSpec((hps * dv, blk), lambda b, h: (h, b)),
        pl.BlockSpec((hps * dv, blk), lambda b, h: (hb + h, b)),
        pl.BlockSpec((blk, GATE_LANES), lambda b, h: (b, 0)),
        pl.BlockSpec((4, GATE_STRIDE, blk), lambda b, h: (0, 0, b)),
        pl.BlockSpec((None, hps, dv, GATE_LANES), lambda b, h: (layer, h, 0, 0)),
    ]
    args = [proj, fcc, fsc, fms, proj, proj, proj_t, proj_t, gcol, grow, gain]
    out_specs = [pl.BlockSpec((blk, hps * dg), lambda b, h: (b, h)),
                 pl.BlockSpec((hps * dv, blk), lambda b, h: (h, b))]
    out_shape = [jax.ShapeDtypeStruct((rows, df), BF16), jax.ShapeDtypeStruct((dm, rows), BF16)]
    scratch = [pltpu.VMEM((sps, 2 * seq, hps * dg), BF16)]
    aliases = {}
    if has_init:
        init_c, init_n, init_m = init
        in_specs += [
            pl.BlockSpec((None, None, 2, hps, dv, dqk), lambda b, h: (b, layer, 0, h, 0, 0)),
            pl.BlockSpec((None, None, 2, N_HEADS, dqk), lambda b, h: (b, layer, 0, 0, 0)),
            pl.BlockSpec(memory_space=pltpu.MemorySpace.SMEM),
        ]
        args += [init_c, init_n, init_m]
        scratch += [pltpu.VMEM((hps, seq // MLSTM_CHUNK, dv, 2 * dqk), BF16)]
    else:
        if state_c is not None:
            in_specs.append(pl.BlockSpec(memory_space=pl.ANY))
            args.append(state_c)
            aliases = {len(args) - 1: 2}
            state_spec = pl.BlockSpec((sps, None, 2, hps, dv, dqk), lambda b, h: (b, layer, 0, h, 0, 0))
        else:
            state_spec = pl.BlockSpec((sps, depth, 2, hps, dv, dqk), lambda b, h: (b, 0, 0, h, 0, 0))
        out_specs += [
            state_spec,
            pl.BlockSpec((sps, hps, 2, dqk), lambda b, h: (b, h, 0, 0)),
            pl.BlockSpec((sps, hps, 2, GATE_LANES), lambda b, h: (b, h, 0, 0)),
        ]
        out_shape += [
            jax.ShapeDtypeStruct((n_seq, depth, 2, N_HEADS, dv, dqk), F32),
            jax.ShapeDtypeStruct((n_seq, N_HEADS, 2, dqk), F32),
            jax.ShapeDtypeStruct((n_seq, N_HEADS, 2, GATE_LANES), F32),
        ]
    return pl.pallas_call(
        functools.partial(_mlstm_kernel, seq=seq, has_init=has_init, layer=layer, alias_state=state_c is not None,
                          heads=hps, n_in=len(args), n_out=len(out_specs), dg=dg),
        grid=(n_seq // sps, N_HEADS // hps),
        in_specs=in_specs,
        out_specs=out_specs,
        out_shape=out_shape,
        scratch_shapes=scratch,
        input_output_aliases=aliases,
        compiler_params=_params("parallel", "parallel"),
        name="mixers_lat" if has_init else "mixers_ctx",
    )(*args)


def _dft_tables(seq, dg, grid_w):
    ch = np.arange(dg)
    ang_c = 2.0 * np.pi * ((np.outer(ch, ch) % dg) / dg)
    sc_c = 1.0 / np.sqrt(dg)
    cc = np.cos(ang_c) * sc_c
    sc = np.sin(ang_c) * sc_c
    t = np.arange(seq)
    if grid_w is None:
        frac = (np.outer(t, t) % seq) / seq
    else:
        rows = seq // grid_w
        r, w = t // grid_w, t % grid_w
        frac = (np.outer(r, r) % rows) / rows + (np.outer(w, w) % grid_w) / grid_w
    ang_s = 2.0 * np.pi * frac
    sc_s = 1.0 / np.sqrt(seq)
    ms = np.concatenate([np.cos(ang_s) * sc_s, -np.sin(ang_s) * sc_s], axis=1)
    return (jnp.asarray(cc, F32).astype(BF16), jnp.asarray(sc, F32).astype(BF16),
            jnp.asarray(ms, F32).astype(BF16))


def _merge_kernel(hgt_ref, fy_ref, ga_ref, gb_ref, x_ref, mod_ref, wa_ref, wb_ref, wo_ref, lg_ref, lb_ref,
                  o_ref, *, d, alpha):
    g1 = mod_ref[...][:, 2 * d:3 * d]
    tm = x_ref.shape[0]
    for part in range(MERGE_PARTS):
        r = slice(part * tm // MERGE_PARTS, (part + 1) * tm // MERGE_PARTS)
        ya = _dot_tn(hgt_ref[:, r], wa_ref[...])
        yb = _dot(fy_ref[r, :], wb_ref[...])
        merged = ga_ref[r, :].astype(F32) * ya + gb_ref[r, :].astype(F32) * yb
        out = _dot(merged.astype(BF16), wo_ref[...])
        o_ref[r, :] = _layer_norm(alpha * x_ref[r, :] + g1 * out, lg_ref[...], lb_ref[...])


def _merge(hg_t, fy, proj, x, mod, layer, mod_row, wa, wb, wo, ln_g, ln_b, ga_block, alpha):
    rows, d = x.shape
    dm = hg_t.shape[0]
    df = fy.shape[1]
    tm = WIDE_ROW_TILE
    wmap = lambda i: (layer, 0, 0)
    return pl.pallas_call(
        functools.partial(_merge_kernel, d=d, alpha=alpha),
        grid=(rows // tm,),
        in_specs=[
            pl.BlockSpec((dm, tm), lambda i: (0, i)),
            pl.BlockSpec((tm, df), lambda i: (i, 0)),
            pl.BlockSpec((tm, d), lambda i: (i, ga_block)),
            pl.BlockSpec((tm, d), lambda i: (i, ga_block + 1)),
            pl.BlockSpec((tm, d), lambda i: (i, 0)),
            pl.BlockSpec((None, None, 1, 6 * d), lambda i: (layer, mod_row(i, tm), 0, 0)),
            _resident((None, dm, d), wmap),
            _resident((None, df, d), wmap),
            _resident((None, d, d), wmap),
            _resident((None, None, 1, d), lambda i: (layer, 0, 0, 0)),
            _resident((None, None, 1, d), lambda i: (layer, 0, 0, 0)),
        ],
        out_specs=pl.BlockSpec((tm, d), lambda i: (i, 0)),
        out_shape=jax.ShapeDtypeStruct((rows, d), F32),
        compiler_params=_params("parallel"),
        name="merge",
    )(hg_t, fy, proj, proj, x, mod, wa, wb, wo, ln_g, ln_b)


def _ffn_kernel(x_ref, mod_ref, w1_ref, w2_ref, lg_ref, lb_ref, o_ref, h_ref, g_ref, *, d, alpha):
    dff = w2_ref.shape[0]
    m = mod_ref[...]
    x = x_ref[...]
    h_ref[...] = (x * (1.0 + m[:, 4 * d:5 * d]) + m[:, 3 * d:4 * d]).astype(BF16)
    tk = FFN_COLS
    for kk in range(dff // tk):
        a = _dot(h_ref[...], w1_ref[:, kk * tk:(kk + 1) * tk])
        u = _dot(h_ref[...], w1_ref[:, dff + kk * tk:dff + (kk + 1) * tk])
        g_ref[:, kk * tk:(kk + 1) * tk] = (a * jax.nn.sigmoid(a) * u).astype(BF16)
    f = _dot(g_ref[...], w2_ref[...])
    g2 = m[:, 5 * d:6 * d]
    o_ref[...] = _layer_norm(alpha * x + g2 * f, lg_ref[...], lb_ref[...])


def _ffn(x, mod, layer, mod_row, w1, w2, ln_g, ln_b, alpha):
    rows, d = x.shape
    dff = w2.shape[1]
    tm = WIDE_ROW_TILE
    assert dff % FFN_COLS == 0
    return pl.pallas_call(
        functools.partial(_ffn_kernel, d=d, alpha=alpha),
        grid=(rows // tm,),
        in_specs=[
            pl.BlockSpec((tm, d), lambda i: (i, 0)),
            pl.BlockSpec((None, None, 1, 6 * d), lambda i: (layer, mod_row(i, tm), 0, 0)),
            _resident((None, d, 2 * dff), lambda i: (layer, 0, 0)),
            _resident((None, dff, d), lambda i: (layer, 0, 0)),
            _resident((None, None, 1, d), lambda i: (layer, 1, 0, 0)),
            _resident((None, None, 1, d), lambda i: (layer, 1, 0, 0)),
        ],
        out_specs=pl.BlockSpec((tm, d), lambda i: (i, 0)),
        out_shape=jax.ShapeDtypeStruct((rows, d), F32),
        scratch_shapes=[pltpu.VMEM((tm, d), BF16), pltpu.VMEM((tm, dff), BF16)],
        compiler_params=_params("parallel"),
        name="ffn",
    )(x, mod, w1, w2, ln_g, ln_b)


def kernel(x_prompt, x_sample, c, state_C, state_n, state_m, c_ctx, w_mod, b_mod, w_in, b_gate, mh_gain,
           w_branch_a, w_branch_b, w_out, ln_gain, ln_bias, w_ffn_in, w_ffn_out):
    batch, seq_ctx, d = x_prompt.shape
    dec_batch, seq_lat, _ = x_sample.shape
    depth = w_in.shape[0]
    dm = w_branch_a.shape[1]
    df = w_branch_b.shape[1]
    dv = dm // N_HEADS
    dqk = dv // 2
    dg = df // N_FGROUPS
    n_gates = 4 * N_HEADS
    q_end = N_HEADS * dqk
    k_end = 2 * q_end
    v_end = k_end + dm
    o_end = v_end + dm
    g_end = o_end + n_gates
    alpha = float((2 * depth) ** 0.25)

    mod_rows = MOD_ROWS
    assert 1 + dec_batch <= mod_rows
    cvec = jnp.zeros((mod_rows, d), F32).at[0].set(c_ctx).at[1:1 + dec_batch].set(c)
    mod = _modulation(cvec, w_mod, b_mod).reshape(depth, mod_rows, 1, 6 * d)

    w_in_t = jnp.swapaxes(w_in, 1, 2)
    w_qk, w_vo, w_tail = _split_projection_weights(w_in_t, k_end, o_end, g_end)
    gate_pad = ((0, 0), (0, 0), (0, GATE_STRIDE - N_HEADS), (0, 0))
    lane_pad = ((0, 0), (0, 0), (0, GATE_LANES - 4 * GATE_STRIDE))
    wg = jnp.pad(w_in_t[:, o_end:g_end, :].reshape(depth, 4, N_HEADS, d), gate_pad)
    wg = jnp.pad(jnp.swapaxes(wg.reshape(depth, 4 * GATE_STRIDE, d), 1, 2), lane_pad).astype(BF16)
    bg = jnp.pad(b_gate.reshape(depth, 4, N_HEADS, 1), gate_pad).reshape(depth, 1, 4 * GATE_STRIDE)
    bg = jnp.pad(bg, lane_pad)
    wa = w_branch_a.astype(BF16)
    wb = w_branch_b.astype(BF16)
    wo = w_out.astype(BF16)
    w1 = w_ffn_in.astype(BF16)
    w2 = w_ffn_out.astype(BF16)
    gain = jnp.broadcast_to(mh_gain.reshape(depth, N_HEADS, dv, 1), (depth, N_HEADS, dv, GATE_LANES))
    ln_g = ln_gain.reshape(depth, 2, 1, d)
    ln_b = ln_bias.reshape(depth, 2, 1, d)

    def slab_fn(col):
        if q_end <= col < k_end:
            return lambda t: t * (dqk ** -0.5)
        if v_end <= col < o_end or col >= o_end + df:
            return jax.nn.sigmoid
        return lambda t: t
    n_main = o_end + w_tail.shape[2]
    n_vo = o_end - k_end
    assert all(e % PROJ_COLS == 0 for e in (q_end, k_end, v_end, o_end, o_end + df, n_main))
    epilogues = tuple(slab_fn(j * PROJ_COLS) for j in range(n_main // PROJ_COLS))

    tables_ctx = _dft_tables(seq_ctx, dg, None)
    tables_lat = _dft_tables(seq_lat, dg, GRID_W)
    f_block0 = (o_end - n_vo) // dg
    ga_block = (o_end - n_vo + df) // d

    def ctx_row(i, tm):
        return 0

    def lat_row(i, tm):
        return 1 + (i * tm) // seq_lat

    def layer(x, l, seq, mod_row, tables, init, state_c):
        proj, proj_t, gcol, grow = _inproj(x, mod, l, mod_row, w_qk, w_vo, w_tail, wg, bg, epilogues)
        res = _mixers(proj, proj_t, gcol, grow, gain, tables, f_block0, df, seq, l, depth, init, state_c)
        x = _merge(res[1], res[0], proj, x, mod, l, mod_row, wa, wb, wo, ln_g, ln_b, ga_block, alpha)
        x = _ffn(x, mod, l, mod_row, w1, w2, ln_g, ln_b, alpha)
        return x, res[2:]

    xp = x_prompt.reshape(batch * seq_ctx, d)
    xs = x_sample.reshape(dec_batch * seq_lat, d)
    new_c, new_n, new_m = None, [], []
    for l in range(depth):
        xp, (new_c, nfin, mfin) = layer(xp, l, seq_ctx, ctx_row, tables_ctx, None, new_c)
        new_n.append(nfin.transpose(0, 2, 1, 3))
        new_m.append(mfin[..., 0].transpose(0, 2, 1))
        xs, _ = layer(xs, l, seq_lat, lat_row, tables_lat, (state_C, state_n, state_m), None)
    return (xp.reshape(batch, seq_ctx, d), xs.reshape(dec_batch, seq_lat, d),
            new_c, jnp.stack(new_n, axis=1), jnp.stack(new_m, axis=1))
```

```python
import functools
import math

import numpy as np
import jax
import jax.numpy as jnp
from jax import lax
from jax.experimental import pallas as pl
from jax.experimental.pallas import tpu as pltpu

F32 = jnp.float32
BF16 = jnp.bfloat16

N_HEADS = 4
N_FGROUPS = 4
GRID_W = 64
LN_EPS = 1e-5
MLSTM_CHUNK = 256
GATE_LANES = 128
GATE_STRIDE = 8
ROW_TILE = 512
FFN_ROW_TILE = 1024
PROJ_COLS = 1024
FFN_COLS = 256
LAT_HEADS_PER_STEP = 2
MERGE_PARTS = 2
CTX_SEQS_PER_STEP = 2
MOD_COLS = 1536
MOD_ROWS = 16
VMEM_LIMIT = 56 * 1024 * 1024
LOG2E = math.log2(math.e)

NT_DIMS = (((1,), (1,)), ((), ()))
TN_DIMS = (((0,), (0,)), ((), ()))


def _dot(a, b):
    return jnp.dot(a, b, preferred_element_type=F32)


def _dot_nt(a, b):
    return lax.dot_general(a, b, NT_DIMS, preferred_element_type=F32)


def _dot_tn(a, b):
    return lax.dot_general(a, b, TN_DIMS, preferred_element_type=F32)


def _split3(x):
    hi = x.astype(BF16)
    r = x - hi.astype(F32)
    mid = r.astype(BF16)
    lo = (r - mid.astype(F32)).astype(BF16)
    return hi, mid, lo


def _layer_norm(y, g, b):
    mu = jnp.mean(y, axis=-1, keepdims=True)
    yc = y - mu
    var = jnp.mean(yc * yc, axis=-1, keepdims=True)
    return yc * lax.rsqrt(var + LN_EPS) * g + b


def _log_sigmoid(x):
    return jnp.minimum(x, 0.0) - jnp.log(1.0 + jnp.exp(-jnp.abs(x)))


def _params(*sem):
    return pltpu.CompilerParams(dimension_semantics=sem, vmem_limit_bytes=VMEM_LIMIT)


def _resident(shape, index_map):
    return pl.BlockSpec(shape, index_map, pipeline_mode=pl.Buffered(1))


def _mod_kernel(c_ref, w_ref, b_ref, o_ref):
    c = c_ref[...]
    s = (c * jax.nn.sigmoid(c)).astype(BF16)
    o_ref[...] = _dot(s, w_ref[...].astype(BF16)) + b_ref[...]


def _modulation(cvec, w_mod, b_mod):
    depth, d, n6 = w_mod.shape
    rows = cvec.shape[0]
    tn = MOD_COLS
    return pl.pallas_call(
        _mod_kernel,
        grid=(depth, n6 // tn),
        in_specs=[
            pl.BlockSpec((rows, d), lambda l, j: (0, 0)),
            pl.BlockSpec((None, d, tn), lambda l, j: (l, 0, j)),
            pl.BlockSpec((None, 1, tn), lambda l, j: (l, 0, j)),
        ],
        out_specs=pl.BlockSpec((None, rows, tn), lambda l, j: (l, 0, j)),
        out_shape=jax.ShapeDtypeStruct((depth, rows, n6), F32),
        compiler_params=_params("parallel", "parallel"),
        name="modulation",
    )(cvec, w_mod, b_mod.reshape(depth, 1, n6))


def _cast_kernel(w_ref, o_ref):
    o_ref[...] = w_ref[...].astype(BF16)


def _transpose_cast_kernel(w_ref, o_ref):
    o_ref[...] = w_ref[...].T.astype(BF16)


def _shift_cast_kernel(a_ref, b_ref, o_ref, *, shift):
    o_ref[...] = jnp.concatenate([a_ref[shift:, :], b_ref[...]], axis=0).T.astype(BF16)


def _split_projection_weights(w_in_t, k_end, o_end, g_end):
    depth, n_in, d = w_in_t.shape
    tn = PROJ_COLS
    shift = g_end - o_end
    n_tail = n_in - g_end
    assert k_end % tn == 0 and o_end % tn == 0 and n_tail % tn == 0 and shift % 8 == 0 and tn % shift == 0
    w_qk = pl.pallas_call(
        _transpose_cast_kernel,
        grid=(depth, k_end // tn),
        in_specs=[pl.BlockSpec((None, tn, d), lambda l, j: (l, j, 0))],
        out_specs=pl.BlockSpec((None, d, tn), lambda l, j: (l, 0, j)),
        out_shape=jax.ShapeDtypeStruct((depth, d, k_end), BF16),
        compiler_params=_params("parallel", "parallel"),
        name="w_qk_cast",
    )(w_in_t)
    kb = k_end // tn
    w_vo = pl.pallas_call(
        _cast_kernel,
        grid=(depth, (o_end - k_end) // tn),
        in_specs=[pl.BlockSpec((None, tn, d), lambda l, j: (l, kb + j, 0))],
        out_specs=pl.BlockSpec((None, tn, d), lambda l, j: (l, j, 0)),
        out_shape=jax.ShapeDtypeStruct((depth, o_end - k_end, d), BF16),
        compiler_params=_params("parallel", "parallel"),
        name="w_vo_cast",
    )(w_in_t)
    base = o_end // tn
    per_slab = tn // shift
    tail = pl.pallas_call(
        functools.partial(_shift_cast_kernel, shift=shift),
        grid=(depth, n_tail // tn),
        in_specs=[
            pl.BlockSpec((None, tn, d), lambda l, j: (l, base + j, 0)),
            pl.BlockSpec((None, shift, d), lambda l, j: (l, (base + j + 1) * per_slab, 0)),
        ],
        out_specs=pl.BlockSpec((None, d, tn), lambda l, j: (l, 0, j)),
        out_shape=jax.ShapeDtypeStruct((depth, d, n_tail), BF16),
        compiler_params=_params("parallel", "parallel"),
        name="w_tail_cast",
    )(w_in_t, w_in_t)
    return w_qk, w_vo, tail


def _inproj_kernel(x_ref, mod_ref, wqk_ref, wvo_ref, wt_ref, wg_ref, bg_ref, proj_ref, proj_t_ref, gcol_ref,
                   grow_ref, *rest, d, epilogues):
    h_ref = rest[-3] if len(rest) == 4 else rest[0]
    if len(rest) == 4:
        state_hbm, _, zero_ref, sem = rest
        zero_ref[...] = jnp.zeros_like(zero_ref)
        fills = [pltpu.make_async_copy(zero_ref, state_hbm.at[pl.program_id(0), lyr, dd], sem.at[lyr, dd])
                 for lyr in range(state_hbm.shape[1]) for dd in range(state_hbm.shape[2])]
        for cp in fills:
            cp.start()
    tm = x_ref.shape[0]
    L = MLSTM_CHUNK
    G = GATE_STRIDE
    m = mod_ref[...]
    hb = (x_ref[...] * (1.0 + m[:, d:2 * d]) + m[:, 0:d]).astype(BF16)
    h_ref[...] = hb

    n_rows = 4 * G
    g = _dot(hb, wg_ref[...]) + bg_ref[...]
    lane = lax.broadcasted_iota(jnp.int32, g.shape, 1)
    g = jnp.where((lane & GATE_STRIDE) != 0, _log_sigmoid(g), g)
    gt = g.T[0:n_rows, :]
    r_i = lax.broadcasted_iota(jnp.int32, (L, L), 0)
    c_i = lax.broadcasted_iota(jnp.int32, (L, L), 1)
    upper_b = (r_i <= c_i).astype(BF16)
    lower_b = (r_i >= c_i).astype(BF16)

    def cumsum(x, tri):
        hi, mid, lo = _split3(x)
        return _dot(hi, tri) + _dot(mid, tri) + _dot(lo, tri)

    pieces = []
    for c in range(tm // L):
        xg = gt[:, c * L:(c + 1) * L]
        pre = cumsum(xg[G:2 * G], upper_b)
        suf = cumsum(xg[3 * G:4 * G], lower_b)
        pieces.append(jnp.concatenate([xg[0:G] - pre, pre, xg[2 * G:3 * G] - suf, suf], axis=0))
    gs = jnp.concatenate(pieces, axis=1)
    for qq in range(4):
        grow_ref[qq] = gs[qq * G:(qq + 1) * G, :]
    gcol_ref[...] = jnp.concatenate([gs, jnp.zeros((GATE_LANES - n_rows, tm), F32)], axis=0).T

    tn = PROJ_COLS
    n_qk = wqk_ref.shape[1] // tn
    n_vo = wvo_ref.shape[0] // tn
    for jn, fn in enumerate(epilogues):
        if jn < n_qk:
            proj_ref[:, jn * tn:(jn + 1) * tn] = fn(_dot(h_ref[...], wqk_ref[:, jn * tn:(jn + 1) * tn])).astype(BF16)
        elif jn < n_qk + n_vo:
            jv = jn - n_qk
            proj_t_ref[jv * tn:(jv + 1) * tn, :] = fn(_dot_nt(wvo_ref[jv * tn:(jv + 1) * tn, :], h_ref[...])).astype(BF16)
        else:
            jt = jn - n_qk - n_vo
            jo = jn - n_vo
            proj_ref[:, jo * tn:(jo + 1) * tn] = fn(_dot(h_ref[...], wt_ref[:, jt * tn:(jt + 1) * tn])).astype(BF16)
    if len(rest) == 4:
        for cp in fills:
            cp.wait()


def _inproj(x, mod, layer, mod_row, w_qk, w_vo, w_tail, w_gate, b_gate, epilogues, new_state_shape=None):
    rows, d = x.shape
    n_qk, n_vo, n_t = w_qk.shape[2], w_vo.shape[1], w_tail.shape[2]
    n = n_qk + n_t
    tm = ROW_TILE
    assert n + n_vo == PROJ_COLS * len(epilogues) and rows % tm == 0 and tm % MLSTM_CHUNK == 0
    extra_out_specs, extra_out_shape, extra_scratch = [], [], []
    if new_state_shape is not None:
        assert new_state_shape[0] == rows // tm
        extra_out_specs = [pl.BlockSpec(memory_space=pl.ANY)]
        extra_out_shape = [jax.ShapeDtypeStruct(new_state_shape, F32)]
        extra_scratch = [pltpu.VMEM(new_state_shape[3:], F32), pltpu.SemaphoreType.DMA(new_state_shape[1:3])]
    return pl.pallas_call(
        functools.partial(_inproj_kernel, d=d, epilogues=epilogues),
        grid=(rows // tm,),
        in_specs=[
            pl.BlockSpec((tm, d), lambda i: (i, 0)),
            pl.BlockSpec((None, None, 1, 6 * d), lambda i: (layer, mod_row(i, tm), 0, 0)),
            _resident((None, d, n_qk), lambda i: (layer, 0, 0)),
            _resident((None, n_vo, d), lambda i: (layer, 0, 0)),
            _resident((None, d, n_t), lambda i: (layer, 0, 0)),
            _resident((None, d, GATE_LANES), lambda i: (layer, 0, 0)),
            _resident((None, 1, GATE_LANES), lambda i: (layer, 0, 0)),
        ],
        out_specs=[
            pl.BlockSpec((tm, n), lambda i: (i, 0)),
            pl.BlockSpec((n_vo, tm), lambda i: (0, i)),
            pl.BlockSpec((tm, GATE_LANES), lambda i: (i, 0)),
            pl.BlockSpec((4, GATE_STRIDE, tm), lambda i: (0, 0, i)),
        ] + extra_out_specs,
        out_shape=[
            jax.ShapeDtypeStruct((rows, n), BF16),
            jax.ShapeDtypeStruct((n_vo, rows), BF16),
            jax.ShapeDtypeStruct((rows, GATE_LANES), F32),
            jax.ShapeDtypeStruct((4, GATE_STRIDE, rows), F32),
        ] + extra_out_shape,
        scratch_shapes=[pltpu.VMEM((tm, d), BF16)] + extra_scratch,
        compiler_params=_params("parallel"),
        name="inproj",
    )(x, mod, w_qk, w_vo, w_tail, w_gate, b_gate)


def _mlstm_kernel(*refs, heads, n_in, n_out, seq, dg, **static):
    u_ref, fcc_ref, fsc_ref, fms_ref = refs[:4]
    fy_ref = refs[n_in]
    ab_ref = refs[n_in + n_out]
    mixer_refs = refs[4:n_in] + refs[n_in + 1:n_in + n_out] + refs[n_in + n_out + 1:]
    for si in range(u_ref.shape[0] // seq):
        tok = pl.ds(si * seq, seq)
        for hh in range(heads):
            _mlstm_head(si, hh, *mixer_refs, heads=heads, seq=seq, **static)
        for g in range(u_ref.shape[1] // dg):
            u = u_ref[tok, g * dg:(g + 1) * dg]
            ab_ref[si, 0:seq, g * dg:(g + 1) * dg] = _dot(u, fcc_ref[...]).astype(BF16)
            ab_ref[si, seq:2 * seq, g * dg:(g + 1) * dg] = _dot(u, fsc_ref[...]).astype(BF16)
        fy_ref[tok, :] = _dot(fms_ref[...], ab_ref[si]).astype(BF16)


def _mlstm_head(si, hh, *refs, seq, has_init, layer, heads):
    if has_init:
        (q_ref, k_ref, vt_ref, ogt_ref, gcol_ref, grow_ref, gain_ref, c0_ref, n0_ref, m0_ref,
         out_ref, cc_ref) = refs
    else:
        (q_ref, k_ref, vt_ref, ogt_ref, gcol_ref, grow_ref, gain_ref, _,
         out_ref, cfin_ref, nfin_ref, mfin_ref) = refs
    L = MLSTM_CHUNK
    G = GATE_STRIDE
    nc = seq // L
    tok0 = si * seq
    b_idx = pl.program_id(0)
    h_idx = pl.program_id(1) * heads + hh
    dqk = q_ref.shape[1] // heads
    dv = vt_ref.shape[0] // heads
    qc = slice(hh * dqk, (hh + 1) * dqk)
    vr = slice(hh * dv, (hh + 1) * dv)

    s_i = lax.broadcasted_iota(jnp.int32, (L, L), 0)
    t_i = lax.broadcasted_iota(jnp.int32, (L, L), 1)
    masks = (s_i <= t_i, s_i >= t_i)
    lane = lax.broadcasted_iota(jnp.int32, (L, GATE_LANES), 1)

    def gate_rows(c):
        sl = pl.ds(tok0 + c * L, L)
        hs = pl.ds(h_idx, 1)
        b_f = grow_ref[1, hs, sl]
        b_b = grow_ref[3, hs, sl]
        return ((grow_ref[0, hs, sl], b_f, b_f[:, L - 1:L]), (grow_ref[2, hs, sl], b_b, b_b[:, 0:1]))

    def e_column(c, d):
        gc = gcol_ref[pl.ds(tok0 + c * L, L), :]
        return jnp.sum(jnp.where(lane == 2 * d * G + h_idx, gc, 0.0), axis=1, keepdims=True)

    rows = [gate_rows(c) for c in range(nc)]

    def stabiliser_step(c, d, m_prev):
        e_row, _, b_tot = rows[c][d]
        carried = b_tot + m_prev
        m_new = jnp.maximum(carried, b_tot + jnp.max(e_row, axis=1, keepdims=True))
        return m_new, jnp.exp(carried - m_new)

    def state_contribution(c, d, m_new):
        e_row, _, b_tot = rows[c][d]
        w_row = jnp.exp((e_row + b_tot) - m_new).astype(BF16)
        k = k_ref[pl.ds(tok0 + c * L, L), qc]
        lhs = jnp.concatenate([vt_ref[vr, pl.ds(tok0 + c * L, L)] * w_row, jnp.broadcast_to(w_row, (2 * G, L))], axis=0)
        both = _dot(lhs, k)
        return both[0:dv, :], both[dv:dv + 1, :]

    m_pre = [[None] * nc for _ in range(2)]
    n_pre = [[None] * nc for _ in range(2)]
    if has_init:
        for d in range(2):
            order = list(range(nc)) if d == 0 else list(range(nc - 1, -1, -1))
            m = jnp.full((1, 1), m0_ref[b_idx, layer, d, h_idx], F32)
            decays = {}
            for c in order:
                m_pre[d][c] = m
                if c != order[-1]:
                    m, decays[c] = stabiliser_step(c, d, m)
            local = {c: state_contribution(c, d, m_pre[d][nxt]) for c, nxt in zip(order[:-1], order[1:])}
            n = n0_ref[d, pl.ds(h_idx, 1), :]
            cm = c0_ref[d, hh]
            for pos, c in enumerate(order):
                n_pre[d][c] = n
                cc_ref[hh, c, :, d * dqk:(d + 1) * dqk] = cm.astype(BF16)
                if pos == nc - 1:
                    break
                d_c, d_n = local[c]
                cm = decays[c] * cm + d_c
                n = decays[c] * n + d_n
    else:
        for d in range(2):
            m_pre[d][0] = jnp.zeros((1, 1), F32)
            m, _ = stabiliser_step(0, d, m_pre[d][0])
            c_new, n_new = state_contribution(0, d, m)
            cfin_ref[si, d, hh] = c_new
            nfin_ref[si, hh, pl.ds(d, 1), :] = n_new
            mfin_ref[si, hh, pl.ds(d, 1), :] = jnp.broadcast_to(m, (1, GATE_LANES))

    gain = jnp.concatenate([gain_ref[hh]] * (L // GATE_LANES), axis=1)
    for c in range(nc):
        q = q_ref[pl.ds(tok0 + c * L, L), qc]
        k = k_ref[pl.ds(tok0 + c * L, L), qc]
        v_t = vt_ref[vr, pl.ds(tok0 + c * L, L)]
        if has_init:
            n_rows = jnp.concatenate([n_pre[0][c], n_pre[1][c], jnp.zeros((2 * G - 2, dqk), F32)], axis=0)
            kq = _dot_nt(jnp.concatenate([k, n_rows.astype(BF16)], axis=0), q)
            qk_t, qn_t = kq[0:L, :], kq[L:L + G, :]
        else:
            qk_t = _dot_nt(k, q)
        p_t = None
        scale_rows = []
        for d in range(2):
            _, b_row, _ = rows[c][d]
            em = jnp.where(masks[d], e_column(c, d) * LOG2E, -jnp.inf)
            b2 = b_row * LOG2E
            a2 = b2 + m_pre[d][c] * LOG2E
            mt2 = jnp.maximum(a2, b2 + jnp.max(em, axis=0, keepdims=True))
            s_t = qk_t * jnp.exp2((b2 - mt2) + em)
            den = jnp.sum(s_t, axis=0, keepdims=True)
            if has_init:
                inter = jnp.exp2(a2 - mt2)
                den = den + inter * qn_t[d:d + 1, :]
            r = 1.0 / jnp.maximum(jnp.abs(den), jnp.exp2(-mt2))
            p_t = s_t * r if p_t is None else p_t + s_t * r
            if has_init:
                scale_rows.append((inter * r).astype(BF16))
        h_t = _dot(v_t, p_t.astype(BF16))
        if has_init:
            q_t = q.T
            qs_t = jnp.concatenate([q_t * scale_rows[0], q_t * scale_rows[1]], axis=0)
            h_t = h_t + _dot(cc_ref[hh, c], qs_t)
        mu = jnp.mean(h_t, axis=0, keepdims=True)
        hc = h_t - mu
        var = jnp.mean(hc * hc, axis=0, keepdims=True)
        hn = hc * lax.rsqrt(var + LN_EPS) * gain
        out_ref[vr, pl.ds(tok0 + c * L, L)] = hn.astype(BF16) * ogt_ref[vr, pl.ds(tok0 + c * L, L)]


def _mixers(proj, proj_t, gcol, grow, gain, tables, f_block0, df, seq, layer, depth, init=None, state_c=None):
    rows = proj.shape[0]
    fcc, fsc, fms = tables
    dg = fcc.shape[0]
    n_seq = rows // seq
    dv = gain.shape[2]
    dm = N_HEADS * dv
    dqk = dv // 2
    has_init = init is not None
    hps = LAT_HEADS_PER_STEP if has_init else N_HEADS
    assert has_init != (state_c is not None)
    sps = 1 if has_init else CTX_SEQS_PER_STEP
    blk = sps * seq
    hb = N_HEADS // hps
    assert seq % MLSTM_CHUNK == 0 and (has_init or seq == MLSTM_CHUNK) and dqk % GATE_LANES == 0
    assert df // dg == N_HEADS and f_block0 % hps == 0
    in_specs = [
        pl.BlockSpec((blk, hps * dg), lambda b, h: (b, f_block0 // hps + h)),
        _resident((dg, dg), lambda b, h: (0, 0)),
        _resident((dg, dg), lambda b, h: (0, 0)),
        _resident((seq, 2 * seq), lambda b, h: (0, 0)),
        pl.BlockSpec((blk, hps * dqk), lambda b, h: (b, h)),
        pl.BlockSpec((blk, hps * dqk), lambda b, h: (b, hb + h)),
        pl.BlockSpec((hps * dv, blk), lambda b, h: (h, b)),
        pl.BlockSpec((hps * dv, blk), lambda b, h: (hb + h, b)),
        pl.BlockSpec((blk, GATE_LANES), lambda b, h: (b, 0)),
        pl.BlockSpec((4, GATE_STRIDE, blk), lambda b, h: (0, 0, b)),
        pl.BlockSpec((None, hps, dv, GATE_LANES), lambda b, h: (layer, h, 0, 0)),
    ]
    args = [proj, fcc, fsc, fms, proj, proj, proj_t, proj_t, gcol, grow, gain]
    out_specs = [pl.BlockSpec((blk, hps * dg), lambda b, h: (b, h)),
                 pl.BlockSpec((hps * dv, blk), lambda b, h: (h, b))]
    out_shape = [jax.ShapeDtypeStruct((rows, df), BF16), jax.ShapeDtypeStruct((dm, rows), BF16)]
    scratch = [pltpu.VMEM((sps, 2 * seq, hps * dg), BF16)]
    aliases = {}
    if has_init:
        init_c, init_n, init_m = init
        in_specs += [
            pl.BlockSpec((None, None, 2, hps, dv, dqk), lambda b, h: (b, layer, 0, h, 0, 0)),
            pl.BlockSpec((None, None, 2, N_HEADS, dqk), lambda b, h: (b, layer, 0, 0, 0)),
            pl.BlockSpec(memory_space=pltpu.MemorySpace.SMEM),
        ]
        args += [init_c, init_n, init_m]
        scratch += [pltpu.VMEM((hps, seq // MLSTM_CHUNK, dv, 2 * dqk), BF16)]
    else:
        in_specs.append(pl.BlockSpec(memory_space=pl.ANY))
        args.append(state_c)
        aliases = {len(args) - 1: 2}
        out_specs += [
            pl.BlockSpec((sps, None, 2, hps, dv, dqk), lambda b, h: (b, layer, 0, h, 0, 0)),
            pl.BlockSpec((sps, hps, 2, dqk), lambda b, h: (b, h, 0, 0)),
            pl.BlockSpec((sps, hps, 2, GATE_LANES), lambda b, h: (b, h, 0, 0)),
        ]
        out_shape += [
            jax.ShapeDtypeStruct((n_seq, depth, 2, N_HEADS, dv, dqk), F32),
            jax.ShapeDtypeStruct((n_seq, N_HEADS, 2, dqk), F32),
            jax.ShapeDtypeStruct((n_seq, N_HEADS, 2, GATE_LANES), F32),
        ]
    return pl.pallas_call(
        functools.partial(_mlstm_kernel, seq=seq, has_init=has_init, layer=layer,
                          heads=hps, n_in=len(args), n_out=len(out_specs), dg=dg),
        grid=(n_seq // sps, N_HEADS // hps),
        in_specs=in_specs,
        out_specs=out_specs,
        out_shape=out_shape,
        scratch_shapes=scratch,
        input_output_aliases=aliases,
        compiler_params=_params("parallel", "parallel"),
        name="mixers_lat" if has_init else "mixers_ctx",
    )(*args)


def _dft_tables(seq, dg, grid_w):
    ch = np.arange(dg)
    ang_c = 2.0 * np.pi * ((np.outer(ch, ch) % dg) / dg)
    sc_c = 1.0 / np.sqrt(dg)
    cc = np.cos(ang_c) * sc_c
    sc = np.sin(ang_c) * sc_c
    t = np.arange(seq)
    if grid_w is None:
        frac = (np.outer(t, t) % seq) / seq
    else:
        rows = seq // grid_w
        r, w = t // grid_w, t % grid_w
        frac = (np.outer(r, r) % rows) / rows + (np.outer(w, w) % grid_w) / grid_w
    ang_s = 2.0 * np.pi * frac
    sc_s = 1.0 / np.sqrt(seq)
    ms = np.concatenate([np.cos(ang_s) * sc_s, -np.sin(ang_s) * sc_s], axis=1)
    return (jnp.asarray(cc, F32).astype(BF16), jnp.asarray(sc, F32).astype(BF16),
            jnp.asarray(ms, F32).astype(BF16))


def _merge_kernel(hgt_ref, fy_ref, ga_ref, gb_ref, x_ref, mod_ref, wa_ref, wb_ref, wo_ref, lg_ref, lb_ref,
                  o_ref, *, d, alpha):
    g1 = mod_ref[...][:, 2 * d:3 * d]
    tm = x_ref.shape[0]
    for part in range(MERGE_PARTS):
        r = slice(part * tm // MERGE_PARTS, (part + 1) * tm // MERGE_PARTS)
        ya = _dot_tn(hgt_ref[:, r], wa_ref[...])
        yb = _dot(fy_ref[r, :], wb_ref[...])
        merged = ga_ref[r, :].astype(F32) * ya + gb_ref[r, :].astype(F32) * yb
        out = _dot(merged.astype(BF16), wo_ref[...])
        o_ref[r, :] = _layer_norm(alpha * x_ref[r, :] + g1 * out, lg_ref[...], lb_ref[...])


def _merge(hg_t, fy, proj, x, mod, layer, mod_row, wa, wb, wo, ln_g, ln_b, ga_block, alpha):
    rows, d = x.shape
    dm = hg_t.shape[0]
    df = fy.shape[1]
    tm = ROW_TILE
    wmap = lambda i: (layer, 0, 0)
    return pl.pallas_call(
        functools.partial(_merge_kernel, d=d, alpha=alpha),
        grid=(rows // tm,),
        in_specs=[
            pl.BlockSpec((dm, tm), lambda i: (0, i)),
            pl.BlockSpec((tm, df), lambda i: (i, 0)),
            pl.BlockSpec((tm, d), lambda i: (i, ga_block)),
            pl.BlockSpec((tm, d), lambda i: (i, ga_block + 1)),
            pl.BlockSpec((tm, d), lambda i: (i, 0)),
            pl.BlockSpec((None, None, 1, 6 * d), lambda i: (layer, mod_row(i, tm), 0, 0)),
            _resident((None, dm, d), wmap),
            _resident((None, df, d), wmap),
            _resident((None, d, d), wmap),
            _resident((None, None, 1, d), lambda i: (layer, 0, 0, 0)),
            _resident((None, None, 1, d), lambda i: (layer, 0, 0, 0)),
        ],
        out_specs=pl.BlockSpec((tm, d), lambda i: (i, 0)),
        out_shape=jax.ShapeDtypeStruct((rows, d), F32),
        compiler_params=_params("parallel"),
        name="merge",
    )(hg_t, fy, proj, proj, x, mod, wa, wb, wo, ln_g, ln_b)


def _ffn_kernel(x_ref, mod_ref, w1_ref, w2_ref, lg_ref, lb_ref, o_ref, h_ref, g_ref, *, d, alpha):
    dff = w2_ref.shape[0]
    m = mod_ref[...]
    x = x_ref[...]
    h_ref[...] = (x * (1.0 + m[:, 4 * d:5 * d]) + m[:, 3 * d:4 * d]).astype(BF16)
    tk = FFN_COLS
    for kk in range(dff // tk):
        a = _dot(h_ref[...], w1_ref[:, kk * tk:(kk + 1) * tk])
        u = _dot(h_ref[...], w1_ref[:, dff + kk * tk:dff + (kk + 1) * tk])
        g_ref[:, kk * tk:(kk + 1) * tk] = (a * jax.nn.sigmoid(a) * u).astype(BF16)
    f = _dot(g_ref[...], w2_ref[...])
    g2 = m[:, 5 * d:6 * d]
    o_ref[...] = _layer_norm(alpha * x + g2 * f, lg_ref[...], lb_ref[...])


def _ffn(x, mod, layer, mod_row, w1, w2, ln_g, ln_b, alpha):
    rows, d = x.shape
    dff = w2.shape[1]
    tm = FFN_ROW_TILE
    assert dff % FFN_COLS == 0
    return pl.pallas_call(
        functools.partial(_ffn_kernel, d=d, alpha=alpha),
        grid=(rows // tm,),
        in_specs=[
            pl.BlockSpec((tm, d), lambda i: (i, 0)),
            pl.BlockSpec((None, None, 1, 6 * d), lambda i: (layer, mod_row(i, tm), 0, 0)),
            _resident((None, d, 2 * dff), lambda i: (layer, 0, 0)),
            _resident((None, dff, d), lambda i: (layer, 0, 0)),
            _resident((None, None, 1, d), lambda i: (layer, 1, 0, 0)),
            _resident((None, None, 1, d), lambda i: (layer, 1, 0, 0)),
        ],
        out_specs=pl.BlockSpec((tm, d), lambda i: (i, 0)),
        out_shape=jax.ShapeDtypeStruct((rows, d), F32),
        scratch_shapes=[pltpu.VMEM((tm, d), BF16), pltpu.VMEM((tm, dff), BF16)],
        compiler_params=_params("parallel"),
        name="ffn",
    )(x, mod, w1, w2, ln_g, ln_b)


def kernel(x_prompt, x_sample, c, state_C, state_n, state_m, c_ctx, w_mod, b_mod, w_in, b_gate, mh_gain,
           w_branch_a, w_branch_b, w_out, ln_gain, ln_bias, w_ffn_in, w_ffn_out):
    batch, seq_ctx, d = x_prompt.shape
    dec_batch, seq_lat, _ = x_sample.shape
    depth = w_in.shape[0]
    dm = w_branch_a.shape[1]
    df = w_branch_b.shape[1]
    dv = dm // N_HEADS
    dqk = dv // 2
    dg = df // N_FGROUPS
    n_gates = 4 * N_HEADS
    q_end = N_HEADS * dqk
    k_end = 2 * q_end
    v_end = k_end + dm
    o_end = v_end + dm
    g_end = o_end + n_gates
    alpha = float((2 * depth) ** 0.25)

    mod_rows = MOD_ROWS
    assert 1 + dec_batch <= mod_rows
    cvec = jnp.zeros((mod_rows, d), F32).at[0].set(c_ctx).at[1:1 + dec_batch].set(c)
    mod = _modulation(cvec, w_mod, b_mod).reshape(depth, mod_rows, 1, 6 * d)

    w_in_t = jnp.swapaxes(w_in, 1, 2)
    w_qk, w_vo, w_tail = _split_projection_weights(w_in_t, k_end, o_end, g_end)
    gate_pad = ((0, 0), (0, 0), (0, GATE_STRIDE - N_HEADS), (0, 0))
    lane_pad = ((0, 0), (0, 0), (0, GATE_LANES - 4 * GATE_STRIDE))
    wg = jnp.pad(w_in_t[:, o_end:g_end, :].reshape(depth, 4, N_HEADS, d), gate_pad)
    wg = jnp.pad(jnp.swapaxes(wg.reshape(depth, 4 * GATE_STRIDE, d), 1, 2), lane_pad).astype(BF16)
    bg = jnp.pad(b_gate.reshape(depth, 4, N_HEADS, 1), gate_pad).reshape(depth, 1, 4 * GATE_STRIDE)
    bg = jnp.pad(bg, lane_pad)
    wa = w_branch_a.astype(BF16)
    wb = w_branch_b.astype(BF16)
    wo = w_out.astype(BF16)
    w1 = w_ffn_in.astype(BF16)
    w2 = w_ffn_out.astype(BF16)
    gain = jnp.broadcast_to(mh_gain.reshape(depth, N_HEADS, dv, 1), (depth, N_HEADS, dv, GATE_LANES))
    ln_g = ln_gain.reshape(depth, 2, 1, d)
    ln_b = ln_bias.reshape(depth, 2, 1, d)

    def slab_fn(col):
        if q_end <= col < k_end:
            return lambda t: t * (dqk ** -0.5)
        if v_end <= col < o_end or col >= o_end + df:
            return jax.nn.sigmoid
        return lambda t: t
    n_main = o_end + w_tail.shape[2]
    n_vo = o_end - k_end
    assert all(e % PROJ_COLS == 0 for e in (q_end, k_end, v_end, o_end, o_end + df, n_main))
    epilogues = tuple(slab_fn(j * PROJ_COLS) for j in range(n_main // PROJ_COLS))

    tables_ctx = _dft_tables(seq_ctx, dg, None)
    tables_lat = _dft_tables(seq_lat, dg, GRID_W)
    f_block0 = (o_end - n_vo) // dg
    ga_block = (o_end - n_vo + df) // d

    def ctx_row(i, tm):
        return 0

    def lat_row(i, tm):
        return 1 + (i * tm) // seq_lat

    def project(x, l, mod_row, new_state_shape=None):
        return _inproj(x, mod, l, mod_row, w_qk, w_vo, w_tail, wg, bg, epilogues, new_state_shape)

    def mix_and_feed(x, projected, l, seq, mod_row, tables, init, state_c):
        proj, proj_t, gcol, grow = projected
        res = _mixers(proj, proj_t, gcol, grow, gain, tables, f_block0, df, seq, l, depth, init, state_c)
        x = _merge(res[1], res[0], proj, x, mod, l, mod_row, wa, wb, wo, ln_g, ln_b, ga_block, alpha)
        x = _ffn(x, mod, l, mod_row, w1, w2, ln_g, ln_b, alpha)
        return x, res[2:]

    xp = x_prompt.reshape(batch * seq_ctx, d)
    xs = x_sample.reshape(dec_batch * seq_lat, d)
    new_n, new_m = [], []
    *lat_projected, new_c = project(xs, 0, lat_row, (batch, depth, 2, N_HEADS, dv, dqk))
    for l in range(depth):
        xp, (new_c, nfin, mfin) = mix_and_feed(xp, project(xp, l, ctx_row), l, seq_ctx, ctx_row, tables_ctx, None, new_c)
        new_n.append(nfin.transpose(0, 2, 1, 3))
        new_m.append(mfin[..., 0].transpose(0, 2, 1))
        if l > 0:
            lat_projected = project(xs, l, lat_row)
        xs, _ = mix_and_feed(xs, lat_projected, l, seq_lat, lat_row, tables_lat, (state_C, state_n, state_m), None)
    return (xp.reshape(batch, seq_ctx, d), xs.reshape(dec_batch, seq_lat, d),
            new_c, jnp.stack(new_n, axis=1), jnp.stack(new_m, axis=1))
```

```python
import functools
import math

import numpy as np
import jax
import jax.numpy as jnp
from jax import lax
from jax.experimental import pallas as pl
from jax.experimental.pallas import tpu as pltpu

F32 = jnp.float32
BF16 = jnp.bfloat16

N_HEADS = 4
N_FGROUPS = 4
GRID_W = 64
LN_EPS = 1e-5
MLSTM_CHUNK = 256
GATE_LANES = 128
GATE_STRIDE = 8
ROW_TILE = 512
FFN_ROW_TILE = 1024
PROJ_COLS = 1024
FFN_COLS = 256
LAT_HEADS_PER_STEP = 2
MERGE_PARTS = 2
CTX_SEQS_PER_STEP = 2
MOD_COLS = 1536
MOD_ROWS = 16
VMEM_LIMIT = 56 * 1024 * 1024
LOG2E = math.log2(math.e)

NT_DIMS = (((1,), (1,)), ((), ()))
TN_DIMS = (((0,), (0,)), ((), ()))


def _dot(a, b):
    return jnp.dot(a, b, preferred_element_type=F32)


def _dot_nt(a, b):
    return lax.dot_general(a, b, NT_DIMS, preferred_element_type=F32)


def _dot_tn(a, b):
    return lax.dot_general(a, b, TN_DIMS, preferred_element_type=F32)


def _split3(x):
    hi = x.astype(BF16)
    r = x - hi.astype(F32)
    mid = r.astype(BF16)
    lo = (r - mid.astype(F32)).astype(BF16)
    return hi, mid, lo


def _layer_norm(y, g, b):
    mu = jnp.mean(y, axis=-1, keepdims=True)
    yc = y - mu
    var = jnp.mean(yc * yc, axis=-1, keepdims=True)
    return yc * lax.rsqrt(var + LN_EPS) * g + b


def _log_sigmoid(x):
    return jnp.minimum(x, 0.0) - jnp.log(1.0 + jnp.exp(-jnp.abs(x)))


def _params(*sem):
    return pltpu.CompilerParams(dimension_semantics=sem, vmem_limit_bytes=VMEM_LIMIT)


def _resident(shape, index_map):
    return pl.BlockSpec(shape, index_map, pipeline_mode=pl.Buffered(1))


def _mod_kernel(c_ref, w_ref, b_ref, o_ref):
    c = c_ref[...]
    s = (c * jax.nn.sigmoid(c)).astype(BF16)
    o_ref[...] = _dot(s, w_ref[...].astype(BF16)) + b_ref[...]


def _modulation(cvec, w_mod, b_mod):
    depth, d, n6 = w_mod.shape
    rows = cvec.shape[0]
    tn = MOD_COLS
    return pl.pallas_call(
        _mod_kernel,
        grid=(depth, n6 // tn),
        in_specs=[
            pl.BlockSpec((rows, d), lambda l, j: (0, 0)),
            pl.BlockSpec((None, d, tn), lambda l, j: (l, 0, j)),
            pl.BlockSpec((None, 1, tn), lambda l, j: (l, 0, j)),
        ],
        out_specs=pl.BlockSpec((None, rows, tn), lambda l, j: (l, 0, j)),
        out_shape=jax.ShapeDtypeStruct((depth, rows, n6), F32),
        compiler_params=_params("parallel", "parallel"),
        name="modulation",
    )(cvec, w_mod, b_mod.reshape(depth, 1, n6))


def _cast_kernel(w_ref, o_ref):
    o_ref[...] = w_ref[...].astype(BF16)


def _transpose_cast_kernel(w_ref, o_ref):
    o_ref[...] = w_ref[...].T.astype(BF16)


def _shift_cast_kernel(a_ref, b_ref, o_ref, *, shift):
    o_ref[...] = jnp.concatenate([a_ref[shift:, :], b_ref[...]], axis=0).T.astype(BF16)


def _split_projection_weights(w_in_t, k_end, o_end, g_end):
    depth, n_in, d = w_in_t.shape
    tn = PROJ_COLS
    shift = g_end - o_end
    n_tail = n_in - g_end
    assert k_end % tn == 0 and o_end % tn == 0 and n_tail % tn == 0 and shift % 8 == 0 and tn % shift == 0
    w_qk = pl.pallas_call(
        _transpose_cast_kernel,
        grid=(depth, k_end // tn),
        in_specs=[pl.BlockSpec((None, tn, d), lambda l, j: (l, j, 0))],
        out_specs=pl.BlockSpec((None, d, tn), lambda l, j: (l, 0, j)),
        out_shape=jax.ShapeDtypeStruct((depth, d, k_end), BF16),
        compiler_params=_params("parallel", "parallel"),
        name="w_qk_cast",
    )(w_in_t)
    kb = k_end // tn
    w_vo = pl.pallas_call(
        _cast_kernel,
        grid=(depth, (o_end - k_end) // tn),
        in_specs=[pl.BlockSpec((None, tn, d), lambda l, j: (l, kb + j, 0))],
        out_specs=pl.BlockSpec((None, tn, d), lambda l, j: (l, j, 0)),
        out_shape=jax.ShapeDtypeStruct((depth, o_end - k_end, d), BF16),
        compiler_params=_params("parallel", "parallel"),
        name="w_vo_cast",
    )(w_in_t)
    base = o_end // tn
    per_slab = tn // shift
    tail = pl.pallas_call(
        functools.partial(_shift_cast_kernel, shift=shift),
        grid=(depth, n_tail // tn),
        in_specs=[
            pl.BlockSpec((None, tn, d), lambda l, j: (l, base + j, 0)),
            pl.BlockSpec((None, shift, d), lambda l, j: (l, (base + j + 1) * per_slab, 0)),
        ],
        out_specs=pl.BlockSpec((None, d, tn), lambda l, j: (l, 0, j)),
        out_shape=jax.ShapeDtypeStruct((depth, d, n_tail), BF16),
        compiler_params=_params("parallel", "parallel"),
        name="w_tail_cast",
    )(w_in_t, w_in_t)
    return w_qk, w_vo, tail


def _inproj_kernel(x_ref, mod_ref, wqk_ref, wvo_ref, wt_ref, wg_ref, bg_ref, proj_ref, proj_t_ref, gcol_ref,
                   grow_ref, *rest, d, epilogues):
    h_ref = rest[-3] if len(rest) == 4 else rest[0]
    if len(rest) == 4:
        state_hbm, _, zero_ref, sem = rest
        step = pl.program_id(0)

        def fills(b):
            return [pltpu.make_async_copy(zero_ref, state_hbm.at[b, lyr, dd], sem.at[lyr, dd])
                    for lyr in range(state_hbm.shape[1]) for dd in range(state_hbm.shape[2])]

        @pl.when(step == 0)
        def _():
            zero_ref[...] = jnp.zeros_like(zero_ref)

        @pl.when(step > 0)
        def _():
            for cp in fills(step - 1):
                cp.wait()

        for cp in fills(step):
            cp.start()
    tm = x_ref.shape[0]
    L = MLSTM_CHUNK
    G = GATE_STRIDE
    m = mod_ref[...]
    hb = (x_ref[...] * (1.0 + m[:, d:2 * d]) + m[:, 0:d]).astype(BF16)
    h_ref[...] = hb

    n_rows = 4 * G
    g = _dot(hb, wg_ref[...]) + bg_ref[...]
    lane = lax.broadcasted_iota(jnp.int32, g.shape, 1)
    g = jnp.where((lane & GATE_STRIDE) != 0, _log_sigmoid(g), g)
    gt = g.T[0:n_rows, :]
    r_i = lax.broadcasted_iota(jnp.int32, (L, L), 0)
    c_i = lax.broadcasted_iota(jnp.int32, (L, L), 1)
    upper_b = (r_i <= c_i).astype(BF16)
    lower_b = (r_i >= c_i).astype(BF16)

    def cumsum(x, tri):
        hi, mid, lo = _split3(x)
        return _dot(hi, tri) + _dot(mid, tri) + _dot(lo, tri)

    pieces = []
    for c in range(tm // L):
        xg = gt[:, c * L:(c + 1) * L]
        pre = cumsum(xg[G:2 * G], upper_b)
        suf = cumsum(xg[3 * G:4 * G], lower_b)
        pieces.append(jnp.concatenate([xg[0:G] - pre, pre, xg[2 * G:3 * G] - suf, suf], axis=0))
    gs = jnp.concatenate(pieces, axis=1)
    for qq in range(4):
        grow_ref[qq] = gs[qq * G:(qq + 1) * G, :]
    gcol_ref[...] = jnp.concatenate([gs, jnp.zeros((GATE_LANES - n_rows, tm), F32)], axis=0).T

    tn = PROJ_COLS
    n_qk = wqk_ref.shape[1] // tn
    n_vo = wvo_ref.shape[0] // tn
    for jn, fn in enumerate(epilogues):
        if jn < n_qk:
            proj_ref[:, jn * tn:(jn + 1) * tn] = fn(_dot(h_ref[...], wqk_ref[:, jn * tn:(jn + 1) * tn])).astype(BF16)
        elif jn < n_qk + n_vo:
            jv = jn - n_qk
            proj_t_ref[jv * tn:(jv + 1) * tn, :] = fn(_dot_nt(wvo_ref[jv * tn:(jv + 1) * tn, :], h_ref[...])).astype(BF16)
        else:
            jt = jn - n_qk - n_vo
            jo = jn - n_vo
            proj_ref[:, jo * tn:(jo + 1) * tn] = fn(_dot(h_ref[...], wt_ref[:, jt * tn:(jt + 1) * tn])).astype(BF16)
    if len(rest) == 4:
        @pl.when(step == pl.num_programs(0) - 1)
        def _():
            for cp in fills(step):
                cp.wait()


def _inproj(x, mod, layer, mod_row, w_qk, w_vo, w_tail, w_gate, b_gate, epilogues, new_state_shape=None):
    rows, d = x.shape
    n_qk, n_vo, n_t = w_qk.shape[2], w_vo.shape[1], w_tail.shape[2]
    n = n_qk + n_t
    tm = ROW_TILE
    assert n + n_vo == PROJ_COLS * len(epilogues) and rows % tm == 0 and tm % MLSTM_CHUNK == 0
    extra_out_specs, extra_out_shape, extra_scratch = [], [], []
    if new_state_shape is not None:
        assert new_state_shape[0] == rows // tm
        extra_out_specs = [pl.BlockSpec(memory_space=pl.ANY)]
        extra_out_shape = [jax.ShapeDtypeStruct(new_state_shape, F32)]
        extra_scratch = [pltpu.VMEM(new_state_shape[3:], F32), pltpu.SemaphoreType.DMA(new_state_shape[1:3])]
    return pl.pallas_call(
        functools.partial(_inproj_kernel, d=d, epilogues=epilogues),
        grid=(rows // tm,),
        in_specs=[
            pl.BlockSpec((tm, d), lambda i: (i, 0)),
            pl.BlockSpec((None, None, 1, 6 * d), lambda i: (layer, mod_row(i, tm), 0, 0)),
            _resident((None, d, n_qk), lambda i: (layer, 0, 0)),
            _resident((None, n_vo, d), lambda i: (layer, 0, 0)),
            _resident((None, d, n_t), lambda i: (layer, 0, 0)),
            _resident((None, d, GATE_LANES), lambda i: (layer, 0, 0)),
            _resident((None, 1, GATE_LANES), lambda i: (layer, 0, 0)),
        ],
        out_specs=[
            pl.BlockSpec((tm, n), lambda i: (i, 0)),
            pl.BlockSpec((n_vo, tm), lambda i: (0, i)),
            pl.BlockSpec((tm, GATE_LANES), lambda i: (i, 0)),
            pl.BlockSpec((4, GATE_STRIDE, tm), lambda i: (0, 0, i)),
        ] + extra_out_specs,
        out_shape=[
            jax.ShapeDtypeStruct((rows, n), BF16),
            jax.ShapeDtypeStruct((n_vo, rows), BF16),
            jax.ShapeDtypeStruct((rows, GATE_LANES), F32),
            jax.ShapeDtypeStruct((4, GATE_STRIDE, rows), F32),
        ] + extra_out_shape,
        scratch_shapes=[pltpu.VMEM((tm, d), BF16)] + extra_scratch,
        compiler_params=_params("arbitrary" if new_state_shape is not None else "parallel"),
        name="inproj",
    )(x, mod, w_qk, w_vo, w_tail, w_gate, b_gate)


def _mlstm_kernel(*refs, heads, n_in, n_out, seq, dg, **static):
    u_ref, fcc_ref, fsc_ref, fms_ref = refs[:4]
    fy_ref = refs[n_in]
    ab_ref = refs[n_in + n_out]
    mixer_refs = refs[4:n_in] + refs[n_in + 1:n_in + n_out] + refs[n_in + n_out + 1:]
    for si in range(u_ref.shape[0] // seq):
        tok = pl.ds(si * seq, seq)
        for hh in range(heads):
            _mlstm_head(si, hh, *mixer_refs, heads=heads, seq=seq, **static)
        for g in range(u_ref.shape[1] // dg):
            u = u_ref[tok, g * dg:(g + 1) * dg]
            ab_ref[si, 0:seq, g * dg:(g + 1) * dg] = _dot(u, fcc_ref[...]).astype(BF16)
            ab_ref[si, seq:2 * seq, g * dg:(g + 1) * dg] = _dot(u, fsc_ref[...]).astype(BF16)
        fy_ref[tok, :] = _dot(fms_ref[...], ab_ref[si]).astype(BF16)


def _mlstm_head(si, hh, *refs, seq, has_init, layer, heads):
    if has_init:
        (q_ref, k_ref, vt_ref, ogt_ref, gcol_ref, grow_ref, gain_ref, c0_ref, n0_ref, m0_ref,
         out_ref, cc_ref) = refs
    else:
        (q_ref, k_ref, vt_ref, ogt_ref, gcol_ref, grow_ref, gain_ref, _,
         out_ref, cfin_ref, nfin_ref, mfin_ref) = refs
    L = MLSTM_CHUNK
    G = GATE_STRIDE
    nc = seq // L
    tok0 = si * seq
    b_idx = pl.program_id(0)
    h_idx = pl.program_id(1) * heads + hh
    dqk = q_ref.shape[1] // heads
    dv = vt_ref.shape[0] // heads
    qc = slice(hh * dqk, (hh + 1) * dqk)
    vr = slice(hh * dv, (hh + 1) * dv)

    s_i = lax.broadcasted_iota(jnp.int32, (L, L), 0)
    t_i = lax.broadcasted_iota(jnp.int32, (L, L), 1)
    masks = (s_i <= t_i, s_i >= t_i)
    lane = lax.broadcasted_iota(jnp.int32, (L, GATE_LANES), 1)

    def gate_rows(c):
        sl = pl.ds(tok0 + c * L, L)
        hs = pl.ds(h_idx, 1)
        b_f = grow_ref[1, hs, sl]
        b_b = grow_ref[3, hs, sl]
        return ((grow_ref[0, hs, sl], b_f, b_f[:, L - 1:L]), (grow_ref[2, hs, sl], b_b, b_b[:, 0:1]))

    def e_column(c, d):
        gc = gcol_ref[pl.ds(tok0 + c * L, L), :]
        return jnp.sum(jnp.where(lane == 2 * d * G + h_idx, gc, 0.0), axis=1, keepdims=True)

    rows = [gate_rows(c) for c in range(nc)]

    def stabiliser_step(c, d, m_prev):
        e_row, _, b_tot = rows[c][d]
        carried = b_tot + m_prev
        m_new = jnp.maximum(carried, b_tot + jnp.max(e_row, axis=1, keepdims=True))
        return m_new, jnp.exp(carried - m_new)

    def state_contribution(c, d, m_new):
        e_row, _, b_tot = rows[c][d]
        w_row = jnp.exp((e_row + b_tot) - m_new).astype(BF16)
        k = k_ref[pl.ds(tok0 + c * L, L), qc]
        lhs = jnp.concatenate([vt_ref[vr, pl.ds(tok0 + c * L, L)] * w_row, jnp.broadcast_to(w_row, (2 * G, L))], axis=0)
        both = _dot(lhs, k)
        return both[0:dv, :], both[dv:dv + 1, :]

    m_pre = [[None] * nc for _ in range(2)]
    n_pre = [[None] * nc for _ in range(2)]
    if has_init:
        for d in range(2):
            order = list(range(nc)) if d == 0 else list(range(nc - 1, -1, -1))
            m = jnp.full((1, 1), m0_ref[b_idx, layer, d, h_idx], F32)
            decays = {}
            for c in order:
                m_pre[d][c] = m
                if c != order[-1]:
                    m, decays[c] = stabiliser_step(c, d, m)
            local = {c: state_contribution(c, d, m_pre[d][nxt]) for c, nxt in zip(order[:-1], order[1:])}
            n = n0_ref[d, pl.ds(h_idx, 1), :]
            cm = c0_ref[d, hh]
            for pos, c in enumerate(order):
                n_pre[d][c] = n
                cc_ref[hh, c, :, d * dqk:(d + 1) * dqk] = cm.astype(BF16)
                if pos == nc - 1:
                    break
                d_c, d_n = local[c]
                cm = decays[c] * cm + d_c
                n = decays[c] * n + d_n
    else:
        for d in range(2):
            m_pre[d][0] = jnp.zeros((1, 1), F32)
            m, _ = stabiliser_step(0, d, m_pre[d][0])
            c_new, n_new = state_contribution(0, d, m)
            cfin_ref[si, d, hh] = c_new
            nfin_ref[si, hh, pl.ds(d, 1), :] = n_new
            mfin_ref[si, hh, pl.ds(d, 1), :] = jnp.broadcast_to(m, (1, GATE_LANES))

    gain = jnp.concatenate([gain_ref[hh]] * (L // GATE_LANES), axis=1)
    for c in range(nc):
        q = q_ref[pl.ds(tok0 + c * L, L), qc]
        k = k_ref[pl.ds(tok0 + c * L, L), qc]
        v_t = vt_ref[vr, pl.ds(tok0 + c * L, L)]
        if has_init:
            n_rows = jnp.concatenate([n_pre[0][c], n_pre[1][c], jnp.zeros((2 * G - 2, dqk), F32)], axis=0)
            kq = _dot_nt(jnp.concatenate([k, n_rows.astype(BF16)], axis=0), q)
            qk_t, qn_t = kq[0:L, :], kq[L:L + G, :]
        else:
            qk_t = _dot_nt(k, q)
        p_t = None
        scale_rows = []
        for d in range(2):
            _, b_row, _ = rows[c][d]
            em = jnp.where(masks[d], e_column(c, d) * LOG2E, -jnp.inf)
            b2 = b_row * LOG2E
            a2 = b2 + m_pre[d][c] * LOG2E
            mt2 = jnp.maximum(a2, b2 + jnp.max(em, axis=0, keepdims=True))
            s_t = qk_t * jnp.exp2((b2 - mt2) + em)
            den = jnp.sum(s_t, axis=0, keepdims=True)
            if has_init:
                inter = jnp.exp2(a2 - mt2)
                den = den + inter * qn_t[d:d + 1, :]
            r = 1.0 / jnp.maximum(jnp.abs(den), jnp.exp2(-mt2))
            p_t = s_t * r if p_t is None else p_t + s_t * r
            if has_init:
                scale_rows.append((inter * r).astype(BF16))
        h_t = _dot(v_t, p_t.astype(BF16))
        if has_init:
            q_t = q.T
            qs_t = jnp.concatenate([q_t * scale_rows[0], q_t * scale_rows[1]], axis=0)
            h_t = h_t + _dot(cc_ref[hh, c], qs_t)
        mu = jnp.mean(h_t, axis=0, keepdims=True)
        hc = h_t - mu
        var = jnp.mean(hc * hc, axis=0, keepdims=True)
        hn = hc * lax.rsqrt(var + LN_EPS) * gain
        out_ref[vr, pl.ds(tok0 + c * L, L)] = hn.astype(BF16) * ogt_ref[vr, pl.ds(tok0 + c * L, L)]


def _mixers(proj, proj_t, gcol, grow, gain, tables, f_block0, df, seq, layer, depth, init=None, state_c=None):
    rows = proj.shape[0]
    fcc, fsc, fms = tables
    dg = fcc.shape[0]
    n_seq = rows // seq
    dv = gain.shape[2]
    dm = N_HEADS * dv
    dqk = dv // 2
    has_init = init is not None
    hps = LAT_HEADS_PER_STEP if has_init else N_HEADS
    assert has_init != (state_c is not None)
    sps = 1 if has_init else CTX_SEQS_PER_STEP
    blk = sps * seq
    hb = N_HEADS // hps
    assert seq % MLSTM_CHUNK == 0 and (has_init or seq == MLSTM_CHUNK) and dqk % GATE_LANES == 0
    assert df // dg == N_HEADS and f_block0 % hps == 0
    in_specs = [
        pl.BlockSpec((blk, hps * dg), lambda b, h: (b, f_block0 // hps + h)),
        _resident((dg, dg), lambda b, h: (0, 0)),
        _resident((dg, dg), lambda b, h: (0, 0)),
        _resident((seq, 2 * seq), lambda b, h: (0, 0)),
        pl.BlockSpec((blk, hps * dqk), lambda b, h: (b, h)),
        pl.BlockSpec((blk, hps * dqk), lambda b, h: (b, hb + h)),
        pl.BlockSpec((hps * dv, blk), lambda b, h: (h, b)),
        pl.BlockSpec((hps * dv, blk), lambda b, h: (hb + h, b)),
        pl.BlockSpec((blk, GATE_LANES), lambda b, h: (b, 0)),
        pl.BlockSpec((4, GATE_STRIDE, blk), lambda b, h: (0, 0, b)),
        pl.BlockSpec((None, hps, dv, GATE_LANES), lambda b, h: (layer, h, 0, 0)),
    ]
    args = [proj, fcc, fsc, fms, proj, proj, proj_t, proj_t, gcol, grow, gain]
    out_specs = [pl.BlockSpec((blk, hps * dg), lambda b, h: (b, h)),
                 pl.BlockSpec((hps * dv, blk), lambda b, h: (h, b))]
    out_shape = [jax.ShapeDtypeStruct((rows, df), BF16), jax.ShapeDtypeStruct((dm, rows), BF16)]
    scratch = [pltpu.VMEM((sps, 2 * seq, hps * dg), BF16)]
    aliases = {}
    if has_init:
        init_c, init_n, init_m = init
        in_specs += [
            pl.BlockSpec((None, None, 2, hps, dv, dqk), lambda b, h: (b, layer, 0, h, 0, 0)),
            pl.BlockSpec((None, None, 2, N_HEADS, dqk), lambda b, h: (b, layer, 0, 0, 0)),
            pl.BlockSpec(memory_space=pltpu.MemorySpace.SMEM),
        ]
        args += [init_c, init_n, init_m]
        scratch += [pltpu.VMEM((hps, seq // MLSTM_CHUNK, dv, 2 * dqk), BF16)]
    else:
        in_specs.append(pl.BlockSpec(memory_space=pl.ANY))
        args.append(state_c)
        aliases = {len(args) - 1: 2}
        out_specs += [
            pl.BlockSpec((sps, None, 2, hps, dv, dqk), lambda b, h: (b, layer, 0, h, 0, 0)),
            pl.BlockSpec((sps, hps, 2, dqk), lambda b, h: (b, h, 0, 0)),
            pl.BlockSpec((sps, hps, 2, GATE_LANES), lambda b, h: (b, h, 0, 0)),
        ]
        out_shape += [
            jax.ShapeDtypeStruct((n_seq, depth, 2, N_HEADS, dv, dqk), F32),
            jax.ShapeDtypeStruct((n_seq, N_HEADS, 2, dqk), F32),
            jax.ShapeDtypeStruct((n_seq, N_HEADS, 2, GATE_LANES), F32),
        ]
    return pl.pallas_call(
        functools.partial(_mlstm_kernel, seq=seq, has_init=has_init, layer=layer,
                          heads=hps, n_in=len(args), n_out=len(out_specs), dg=dg),
        grid=(n_seq // sps, N_HEADS // hps),
        in_specs=in_specs,
        out_specs=out_specs,
        out_shape=out_shape,
        scratch_shapes=scratch,
        input_output_aliases=aliases,
        compiler_params=_params("parallel", "parallel"),
        name="mixers_lat" if has_init else "mixers_ctx",
    )(*args)


def _dft_tables(seq, dg, grid_w):
    ch = np.arange(dg)
    ang_c = 2.0 * np.pi * ((np.outer(ch, ch) % dg) / dg)
    sc_c = 1.0 / np.sqrt(dg)
    cc = np.cos(ang_c) * sc_c
    sc = np.sin(ang_c) * sc_c
    t = np.arange(seq)
    if grid_w is None:
        frac = (np.outer(t, t) % seq) / seq
    else:
        rows = seq // grid_w
        r, w = t // grid_w, t % grid_w
        frac = (np.outer(r, r) % rows) / rows + (np.outer(w, w) % grid_w) / grid_w
    ang_s = 2.0 * np.pi * frac
    sc_s = 1.0 / np.sqrt(seq)
    ms = np.concatenate([np.cos(ang_s) * sc_s, -np.sin(ang_s) * sc_s], axis=1)
    return (jnp.asarray(cc, F32).astype(BF16), jnp.asarray(sc, F32).astype(BF16),
            jnp.asarray(ms, F32).astype(BF16))


def _merge_kernel(hgt_ref, fy_ref, ga_ref, gb_ref, x_ref, mod_ref, wa_ref, wb_ref, wo_ref, lg_ref, lb_ref,
                  o_ref, *, d, alpha):
    g1 = mod_ref[...][:, 2 * d:3 * d]
    tm = x_ref.shape[0]
    for part in range(MERGE_PARTS):
        r = slice(part * tm // MERGE_PARTS, (part + 1) * tm // MERGE_PARTS)
        ya = _dot_tn(hgt_ref[:, r], wa_ref[...])
        yb = _dot(fy_ref[r, :], wb_ref[...])
        merged = ga_ref[r, :].astype(F32) * ya + gb_ref[r, :].astype(F32) * yb
        out = _dot(merged.astype(BF16), wo_ref[...])
        o_ref[r, :] = _layer_norm(alpha * x_ref[r, :] + g1 * out, lg_ref[...], lb_ref[...])


def _merge(hg_t, fy, proj, x, mod, layer, mod_row, wa, wb, wo, ln_g, ln_b, ga_block, alpha):
    rows, d = x.shape
    dm = hg_t.shape[0]
    df = fy.shape[1]
    tm = ROW_TILE
    wmap = lambda i: (layer, 0, 0)
    return pl.pallas_call(
        functools.partial(_merge_kernel, d=d, alpha=alpha),
        grid=(rows // tm,),
        in_specs=[
            pl.BlockSpec((dm, tm), lambda i: (0, i)),
            pl.BlockSpec((tm, df), lambda i: (i, 0)),
            pl.BlockSpec((tm, d), lambda i: (i, ga_block)),
            pl.BlockSpec((tm, d), lambda i: (i, ga_block + 1)),
            pl.BlockSpec((tm, d), lambda i: (i, 0)),
            pl.BlockSpec((None, None, 1, 6 * d), lambda i: (layer, mod_row(i, tm), 0, 0)),
            _resident((None, dm, d), wmap),
            _resident((None, df, d), wmap),
            _resident((None, d, d), wmap),
            _resident((None, None, 1, d), lambda i: (layer, 0, 0, 0)),
            _resident((None, None, 1, d), lambda i: (layer, 0, 0, 0)),
        ],
        out_specs=pl.BlockSpec((tm, d), lambda i: (i, 0)),
        out_shape=jax.ShapeDtypeStruct((rows, d), F32),
        compiler_params=_params("parallel"),
        name="merge",
    )(hg_t, fy, proj, proj, x, mod, wa, wb, wo, ln_g, ln_b)


def _ffn_kernel(x_ref, mod_ref, w1_ref, w2_ref, lg_ref, lb_ref, o_ref, h_ref, g_ref, *, d, alpha):
    dff = w2_ref.shape[0]
    m = mod_ref[...]
    x = x_ref[...]
    h_ref[...] = (x * (1.0 + m[:, 4 * d:5 * d]) + m[:, 3 * d:4 * d]).astype(BF16)
    tk = FFN_COLS
    for kk in range(dff // tk):
        a = _dot(h_ref[...], w1_ref[:, kk * tk:(kk + 1) * tk])
        u = _dot(h_ref[...], w1_ref[:, dff + kk * tk:dff + (kk + 1) * tk])
        g_ref[:, kk * tk:(kk + 1) * tk] = (a * jax.nn.sigmoid(a) * u).astype(BF16)
    f = _dot(g_ref[...], w2_ref[...])
    g2 = m[:, 5 * d:6 * d]
    o_ref[...] = _layer_norm(alpha * x + g2 * f, lg_ref[...], lb_ref[...])


def _ffn(x, mod, layer, mod_row, w1, w2, ln_g, ln_b, alpha):
    rows, d = x.shape
    dff = w2.shape[1]
    tm = FFN_ROW_TILE
    assert dff % FFN_COLS == 0
    return pl.pallas_call(
        functools.partial(_ffn_kernel, d=d, alpha=alpha),
        grid=(rows // tm,),
        in_specs=[
            pl.BlockSpec((tm, d), lambda i: (i, 0)),
            pl.BlockSpec((None, None, 1, 6 * d), lambda i: (layer, mod_row(i, tm), 0, 0)),
            _resident((None, d, 2 * dff), lambda i: (layer, 0, 0)),
            _resident((None, dff, d), lambda i: (layer, 0, 0)),
            _resident((None, None, 1, d), lambda i: (layer, 1, 0, 0)),
            _resident((None, None, 1, d), lambda i: (layer, 1, 0, 0)),
        ],
        out_specs=pl.BlockSpec((tm, d), lambda i: (i, 0)),
        out_shape=jax.ShapeDtypeStruct((rows, d), F32),
        scratch_shapes=[pltpu.VMEM((tm, d), BF16), pltpu.VMEM((tm, dff), BF16)],
        compiler_params=_params("parallel"),
        name="ffn",
    )(x, mod, w1, w2, ln_g, ln_b)


def kernel(x_prompt, x_sample, c, state_C, state_n, state_m, c_ctx, w_mod, b_mod, w_in, b_gate, mh_gain,
           w_branch_a, w_branch_b, w_out, ln_gain, ln_bias, w_ffn_in, w_ffn_out):
    batch, seq_ctx, d = x_prompt.shape
    dec_batch, seq_lat, _ = x_sample.shape
    depth = w_in.shape[0]
    dm = w_branch_a.shape[1]
    df = w_branch_b.shape[1]
    dv = dm // N_HEADS
    dqk = dv // 2
    dg = df // N_FGROUPS
    n_gates = 4 * N_HEADS
    q_end = N_HEADS * dqk
    k_end = 2 * q_end
    v_end = k_end + dm
    o_end = v_end + dm
    g_end = o_end + n_gates
    alpha = float((2 * depth) ** 0.25)

    mod_rows = MOD_ROWS
    assert 1 + dec_batch <= mod_rows
    cvec = jnp.zeros((mod_rows, d), F32).at[0].set(c_ctx).at[1:1 + dec_batch].set(c)
    mod = _modulation(cvec, w_mod, b_mod).reshape(depth, mod_rows, 1, 6 * d)

    w_in_t = jnp.swapaxes(w_in, 1, 2)
    w_qk, w_vo, w_tail = _split_projection_weights(w_in_t, k_end, o_end, g_end)
    gate_pad = ((0, 0), (0, 0), (0, GATE_STRIDE - N_HEADS), (0, 0))
    lane_pad = ((0, 0), (0, 0), (0, GATE_LANES - 4 * GATE_STRIDE))
    wg = jnp.pad(w_in_t[:, o_end:g_end, :].reshape(depth, 4, N_HEADS, d), gate_pad)
    wg = jnp.pad(jnp.swapaxes(wg.reshape(depth, 4 * GATE_STRIDE, d), 1, 2), lane_pad).astype(BF16)
    bg = jnp.pad(b_gate.reshape(depth, 4, N_HEADS, 1), gate_pad).reshape(depth, 1, 4 * GATE_STRIDE)
    bg = jnp.pad(bg, lane_pad)
    wa = w_branch_a.astype(BF16)
    wb = w_branch_b.astype(BF16)
    wo = w_out.astype(BF16)
    w1 = w_ffn_in.astype(BF16)
    w2 = w_ffn_out.astype(BF16)
    gain = jnp.broadcast_to(mh_gain.reshape(depth, N_HEADS, dv, 1), (depth, N_HEADS, dv, GATE_LANES))
    ln_g = ln_gain.reshape(depth, 2, 1, d)
    ln_b = ln_bias.reshape(depth, 2, 1, d)

    def slab_fn(col):
        if q_end <= col < k_end:
            return lambda t: t * (dqk ** -0.5)
        if v_end <= col < o_end or col >= o_end + df:
            return jax.nn.sigmoid
        return lambda t: t
    n_main = o_end + w_tail.shape[2]
    n_vo = o_end - k_end
    assert all(e % PROJ_COLS == 0 for e in (q_end, k_end, v_end, o_end, o_end + df, n_main))
    epilogues = tuple(slab_fn(j * PROJ_COLS) for j in range(n_main // PROJ_COLS))

    tables_ctx = _dft_tables(seq_ctx, dg, None)
    tables_lat = _dft_tables(seq_lat, dg, GRID_W)
    f_block0 = (o_end - n_vo) // dg
    ga_block = (o_end - n_vo + df) // d

    def ctx_row(i, tm):
        return 0

    def lat_row(i, tm):
        return 1 + (i * tm) // seq_lat

    def project(x, l, mod_row, new_state_shape=None):
        return _inproj(x, mod, l, mod_row, w_qk, w_vo, w_tail, wg, bg, epilogues, new_state_shape)

    def mix_and_feed(x, projected, l, seq, mod_row, tables, init, state_c):
        proj, proj_t, gcol, grow = projected
        res = _mixers(proj, proj_t, gcol, grow, gain, tables, f_block0, df, seq, l, depth, init, state_c)
        x = _merge(res[1], res[0], proj, x, mod, l, mod_row, wa, wb, wo, ln_g, ln_b, ga_block, alpha)
        x = _ffn(x, mod, l, mod_row, w1, w2, ln_g, ln_b, alpha)
        return x, res[2:]

    xp = x_prompt.reshape(batch * seq_ctx, d)
    xs = x_sample.reshape(dec_batch * seq_lat, d)
    new_n, new_m = [], []
    *lat_projected, new_c = project(xs, 0, lat_row, (batch, depth, 2, N_HEADS, dv, dqk))
    for l in range(depth):
        xp, (new_c, nfin, mfin) = mix_and_feed(xp, project(xp, l, ctx_row), l, seq_ctx, ctx_row, tables_ctx, None, new_c)
        new_n.append(nfin.transpose(0, 2, 1, 3))
        new_m.append(mfin[..., 0].transpose(0, 2, 1))
        if l > 0:
            lat_projected = project(xs, l, lat_row)
        xs, _ = mix_and_feed(xs, lat_projected, l, seq_lat, lat_row, tables_lat, (state_C, state_n, state_m), None)
    return (xp.reshape(batch, seq_ctx, d), xs.reshape(dec_batch, seq_lat, d),
            new_c, jnp.stack(new_n, axis=1), jnp.stack(new_m, axis=1))
```
